```python
import numpy as np
import jax
import jax.numpy as jnp
from jax import lax

D_MODEL = 2048
BATCH = 8
SEQ = 2048
DEPTH = 2

GRID_W = 64
CTX_LEN = 256
EPS = 1e-6
ROPE_BASE = 10000.0
Q_BLOCK = 128
NEG_INF = -1e30

GLA_HEADS = 4
GLA_DK = 64
GLA_DV = 128
GLA_GATE_RANK = 16
GLA_TAU = 16.0
GLA_CHUNK = 64
NA_HEADS = 8
NA_HEAD_DIM = 64
NA_KH = 8
NA_KW = 16
NA_QW = 16
NA_BAND = NA_QW + NA_KW
NA_SCALE = NA_HEAD_DIM ** -0.5
MLA_HEADS = 8
MLA_Q_RANK = 512
MLA_KV_RANK = 512
MLA_NOPE = 128
MLA_ROPE = 64
MLA_V = 128
MLA_SCALE = (MLA_NOPE + MLA_ROPE) ** -0.5
N_BRANCH = 3
N_EXPERTS = 32
TOP_K = 4
EXPERT_FF = D_MODEL
SWIGLU_LIMIT = 7.0
SWIGLU_ALPHA = 1.702

GLA_QK_W = GLA_HEADS * GLA_DK
GLA_V_W = GLA_HEADS * GLA_DV
NA_W = NA_HEADS * NA_HEAD_DIM
MLA_V_W = MLA_HEADS * MLA_V
IN_SPLITS = (GLA_QK_W, GLA_QK_W, GLA_V_W, GLA_V_W, GLA_GATE_RANK, GLA_GATE_RANK,
             NA_W, NA_W, NA_W, MLA_Q_RANK, MLA_KV_RANK, MLA_ROPE, N_BRANCH * D_MODEL)
IN_WIDTH = sum(IN_SPLITS)

kernel_name = 'hybrid_gla_natten_mla_moe_prefix_dit'


def _rmsnorm(x, g):
    xf = x.astype(jnp.float32)
    y = xf * lax.rsqrt(jnp.mean(xf * xf, axis=-1, keepdims=True) + EPS)
    return (y * g.astype(jnp.float32)).astype(x.dtype)


def _split_proj(p):
    points = [int(v) for v in np.cumsum(IN_SPLITS)[:-1]]
    return jnp.split(p, points, axis=-1)


def _to_heads(t, dh):
    b_, n, _ = t.shape
    return t.reshape(b_, n, -1, dh).transpose(0, 2, 1, 3)


def _axial_rope(x):
    n = x.shape[1]
    t = jnp.arange(n)
    half = x.shape[-1] // 2
    nf = half // 2
    freqs = ROPE_BASE ** (-jnp.arange(nf, dtype=jnp.float32) / nf)
    out = []
    for a, pos in enumerate((t // GRID_W, t % GRID_W)):
        ang = pos.astype(jnp.float32)[:, None] * freqs[None, :]
        cos = jnp.cos(ang)[None, :, None, :].astype(x.dtype)
        sin = jnp.sin(ang)[None, :, None, :].astype(x.dtype)
        xa = x[..., a * half:(a + 1) * half]
        x1, x2 = xa[..., :nf], xa[..., nf:]
        out += [x1 * cos - x2 * sin, x1 * sin + x2 * cos]
    return jnp.concatenate(out, axis=-1)


def _blocked_attention(q, k, v, scale):
    b_, n, h_, dq = q.shape
    blk = min(Q_BLOCK, n)
    nb = n // blk
    qb = jnp.moveaxis(q.reshape(b_, nb, blk, h_, dq), 1, 0)

    def one_block(qi):
        s = jnp.einsum('bqhd,bkhd->bhqk', qi, k).astype(jnp.float32) * scale
        p = jax.nn.softmax(s, axis=-1).astype(v.dtype)
        return jnp.einsum('bhqk,bkhd->bqhd', p, v)

    o = lax.map(one_block, qb)
    return jnp.moveaxis(o, 0, 1).reshape(b_, n, h_, v.shape[-1])


def _gla_chunked(q, k, v, log_a, s0):
    b_, h_, n, _ = q.shape
    dv = v.shape[-1]
    nc = n // GLA_CHUNK

    def chunks(t):
        return jnp.moveaxis(t.astype(jnp.float32).reshape(b_, h_, nc, GLA_CHUNK, t.shape[-1]), 2, 0)

    qc, kc, vc = chunks(q), chunks(k), chunks(v)
    bc = jnp.cumsum(chunks(log_a), axis=-2)
    lower = jnp.tril(jnp.ones((GLA_CHUNK, GLA_CHUNK), jnp.float32))

    def step(s, inp):
        qi, ki, vi, bi = inp
        q_dec = qi * jnp.exp(bi)
        k_inv = ki * jnp.exp(-bi)
        att = jnp.einsum('bhtd,bhsd->bhts', q_dec, k_inv) * lower
        o = jnp.einsum('bhtd,bhde->bhte', q_dec, s) + jnp.einsum('bhts,bhse->bhte', att, vi)
        b_last = bi[:, :, -1:, :]
        k_dec = ki * jnp.exp(b_last - bi)
        s_new = jnp.exp(b_last[:, :, 0, :, None]) * s + jnp.einsum('bhsd,bhse->bhde', k_dec, vi)
        return s_new, o

    s_fin, o = lax.scan(step, s0, (qc, kc, vc, bc))
    o = jnp.moveaxis(o, 0, 2).reshape(b_, h_, n, dv)
    return o, s_fin


def _gla_bidir(q, k, v, a_f, a_b, wa, ba, s_f0, s_b0):
    log_f = jax.nn.log_sigmoid((a_f @ wa[0] + ba[0]).astype(jnp.float32)) / GLA_TAU
    log_b = jax.nn.log_sigmoid((a_b @ wa[1] + ba[1]).astype(jnp.float32)) / GLA_TAU
    qh = _to_heads(q * GLA_DK ** -0.5, GLA_DK)
    kh = _to_heads(k, GLA_DK)
    vh = _to_heads(v, GLA_DV)
    lf = _to_heads(log_f, GLA_DK)
    lb = _to_heads(log_b, GLA_DK)
    o_f, s_f = _gla_chunked(qh, kh, vh, lf, s_f0)
    o_b, s_b = _gla_chunked(jnp.flip(qh, 2), jnp.flip(kh, 2), jnp.flip(vh, 2), jnp.flip(lb, 2), s_b0)
    return o_f + jnp.flip(o_b, 2), (s_f, s_b)


def _gla_output(o, r, g):
    b_, h_, n, dv = o.shape
    o = _rmsnorm(jnp.transpose(o, (0, 2, 1, 3)), g.reshape(h_, dv)).reshape(b_, n, h_ * dv).astype(r.dtype)
    return o * jax.nn.silu(r)


def _na_latent(q, k, v, k_ctx, v_ctx, rpb):
    b_, n, h_, dh = q.shape
    rows = n // GRID_W
    kh = min(NA_KH, rows)
    nj = GRID_W // NA_QW
    kg = k.reshape(b_, rows, GRID_W, h_, dh)
    vg = v.reshape(b_, rows, GRID_W, h_, dh)
    qg = jnp.moveaxis(q.reshape(b_, rows, nj, NA_QW, h_, dh), 1, 0)
    qcol = jnp.arange(GRID_W).reshape(nj, NA_QW)
    band = jnp.clip(jnp.arange(nj) * NA_QW - NA_KW // 2, 0, GRID_W - NA_BAND)[:, None] + jnp.arange(NA_BAND)
    wstart = jnp.clip(qcol - NA_KW // 2, 0, GRID_W - NA_KW)
    in_win = (band[:, None, :] >= wstart[..., None]) & (band[:, None, :] < wstart[..., None] + NA_KW)
    dc_idx = jnp.clip(band[:, None, :] - qcol[:, :, None] + NA_KW - 1, 0, 2 * NA_KW - 2)

    def one_row(args):
        r, q_r = args
        rs = jnp.clip(r - kh // 2, 0, rows - kh)
        k_blk = lax.dynamic_slice_in_dim(kg, rs, kh, axis=1)[:, :, band]
        v_blk = lax.dynamic_slice_in_dim(vg, rs, kh, axis=1)[:, :, band]
        s_loc = jnp.einsum('bjqhd,bijkhd->bhjqik', q_r, k_blk).astype(jnp.float32) * NA_SCALE
        dr_idx = rs + jnp.arange(kh) - r + NA_KH - 1
        bias = rpb[:, dr_idx[None, None, :, None], dc_idx[:, :, None, :]].astype(jnp.float32)
        s_loc = jnp.where(in_win[:, :, None, :], s_loc + bias, NEG_INF)
        s_ctx = jnp.einsum('bjqhd,bkhd->bhjqk', q_r, k_ctx).astype(jnp.float32) * NA_SCALE
        s = jnp.concatenate([s_loc.reshape(b_, h_, nj, NA_QW, kh * NA_BAND), s_ctx], axis=-1)
        p = jax.nn.softmax(s, axis=-1).astype(v.dtype)
        p_loc = p[..., :kh * NA_BAND].reshape(b_, h_, nj, NA_QW, kh, NA_BAND)
        p_ctx = p[..., kh * NA_BAND:]
        return (jnp.einsum('bhjqik,bijkhd->bjqhd', p_loc, v_blk)
                + jnp.einsum('bhjqk,bkhd->bjqhd', p_ctx, v_ctx))

    o = lax.map(one_row, (jnp.arange(rows), qg))
    return jnp.moveaxis(o, 0, 1).reshape(b_, n, h_ * dh)


def _mla_project(cq, ckv, kr, p, rotate):
    b_, n, _ = cq.shape
    q = (_rmsnorm(cq, p['mla_q_norm_g']) @ p['mla_w_q_up']).reshape(b_, n, MLA_HEADS, MLA_NOPE + MLA_ROPE)
    kv = (_rmsnorm(ckv, p['mla_kv_norm_g']) @ p['mla_w_kv_up']).reshape(b_, n, MLA_HEADS, MLA_NOPE + MLA_V)
    q_nope, q_rope = q[..., :MLA_NOPE], q[..., MLA_NOPE:]
    k_nope, v = kv[..., :MLA_NOPE], kv[..., MLA_NOPE:]
    k_rope = kr[:, :, None, :]
    if rotate:
        q_rope = _axial_rope(q_rope)
        k_rope = _axial_rope(k_rope)
    q = jnp.concatenate([q_nope, q_rope], axis=-1)
    k = jnp.concatenate([k_nope, jnp.broadcast_to(k_rope, (b_, n, MLA_HEADS, MLA_ROPE))], axis=-1)
    return q, k, v


def _merge(br_a, br_b, br_c, gate_cols, b_gate, w_out):
    g = jax.nn.sigmoid(gate_cols + b_gate)
    ga, gb, gc = jnp.split(g, N_BRANCH, axis=-1)
    return (ga * br_a + gb * br_b + gc * br_c) @ w_out


def _moe(h, router_w, router_b, w1, b1, w2, b2):
    logits = (h @ router_w + router_b).astype(jnp.float32)
    top_val, top_idx = lax.top_k(logits, TOP_K)
    top_w = jax.nn.softmax(top_val, axis=-1)
    combine = jnp.sum(jax.nn.one_hot(top_idx, N_EXPERTS, dtype=jnp.float32) * top_w[..., None], axis=1).astype(h.dtype)
    out = jnp.zeros_like(h)
    for e in range(N_EXPERTS):
        u = h @ w1[e] + b1[e]
        x_glu = jnp.minimum(u[:, :EXPERT_FF], SWIGLU_LIMIT)
        x_lin = jnp.clip(u[:, EXPERT_FF:], -SWIGLU_LIMIT, SWIGLU_LIMIT)
        act = x_glu * jax.nn.sigmoid(SWIGLU_ALPHA * x_glu) * (x_lin + 1)
        out = out + combine[:, e:e + 1] * (act @ w2[e] + b2[e])
    return out


def _layer(xl, xc, c, c_ctx, p, last):
    b_, n, d = xl.shape
    lc = xc.shape[1]
    mod_l = jax.nn.silu(c) @ p['ada_w'] + p['ada_b']
    mod_c = jax.nn.silu(c_ctx) @ p['ada_w'] + p['ada_b']
    sh1_l, sc1_l, gt1_l, sh2_l, sc2_l, gt2_l = jnp.split(mod_l[:, None, :], 6, axis=-1)
    sh1_c, sc1_c, gt1_c, sh2_c, sc2_c, gt2_c = jnp.split(mod_c, 6, axis=-1)

    hl = _rmsnorm(xl, p['norm1_g']) * (1 + sc1_l) + sh1_l
    hc = _rmsnorm(xc, p['norm1_g']) * (1 + sc1_c) + sh1_c
    (gq_l, gk_l, gv_l, gr_l, gaf_l, gab_l, nq_l, nk_l, nv_l, mcq_l, mckv_l, mkr_l, gate_l) = _split_proj(hl @ p['w_in'])
    (gq_c, gk_c, gv_c, gr_c, gaf_c, gab_c, nq_c, nk_c, nv_c, mcq_c, mckv_c, mkr_c, gate_c) = _split_proj(hc @ p['w_in'])

    zero = jnp.zeros((b_, GLA_HEADS, GLA_DK, GLA_DV), jnp.float32)
    o_ctx_a, ctx_states = _gla_bidir(gq_c, gk_c, gv_c, gaf_c, gab_c, p['gla_wa'], p['gla_ba'], zero, zero)
    o_lat_a, _ = _gla_bidir(gq_l, gk_l, gv_l, gaf_l, gab_l, p['gla_wa'], p['gla_ba'], ctx_states[0], ctx_states[1])
    br_a = _gla_output(o_lat_a, gr_l, p['gla_norm_g']) @ p['w_branch_gla']

    nh = lambda t: t.reshape(t.shape[0], t.shape[1], NA_HEADS, NA_HEAD_DIM)
    br_b = _na_latent(nh(nq_l), nh(nk_l), nh(nv_l), nh(nk_c), nh(nv_c), p['na_rpb']) @ p['w_branch_na']

    q_l, k_l, v_l = _mla_project(mcq_l, mckv_l, mkr_l, p, True)
    q_c, k_c, v_c = _mla_project(mcq_c, mckv_c, mkr_c, p, False)
    o_c = _blocked_attention(q_l, jnp.concatenate([k_l, k_c], axis=1), jnp.concatenate([v_l, v_c], axis=1), MLA_SCALE)
    br_c = o_c.reshape(b_, n, MLA_V_W) @ p['w_branch_mla']

    xl = xl + gt1_l * _merge(br_a, br_b, br_c, gate_l, p['b_gate'], p['w_out'])
    hl2 = (_rmsnorm(xl, p['norm2_g']) * (1 + sc2_l) + sh2_l).reshape(b_ * n, d)

    if last:
        xl = xl + gt2_l * _moe(hl2, p['router_w'], p['router_b'], p['moe_w1'], p['moe_b1'],
                               p['moe_w2'], p['moe_b2']).reshape(b_, n, d)
        return xl, xc

    cbr_a = _gla_output(o_ctx_a, gr_c, p['gla_norm_g']) @ p['w_branch_gla']
    cbr_b = _blocked_attention(nh(nq_c), nh(nk_c), nh(nv_c), NA_SCALE).reshape(b_, lc, NA_W) @ p['w_branch_na']
    cbr_c = _blocked_attention(q_c, k_c, v_c, MLA_SCALE).reshape(b_, lc, MLA_V_W) @ p['w_branch_mla']
    xc = xc + gt1_c * _merge(cbr_a, cbr_b, cbr_c, gate_c, p['b_gate'], p['w_out'])
    hc2 = (_rmsnorm(xc, p['norm2_g']) * (1 + sc2_c) + sh2_c).reshape(b_ * lc, d)

    ffn = _moe(jnp.concatenate([hl2, hc2], axis=0), p['router_w'], p['router_b'], p['moe_w1'],
               p['moe_b1'], p['moe_w2'], p['moe_b2'])
    xl = xl + gt2_l * ffn[:b_ * n].reshape(b_, n, d)
    xc = xc + gt2_c * ffn[b_ * n:].reshape(b_, lc, d)
    return xl, xc


def setup_inputs(seed: int = 0) -> dict:
    key = jax.random.key(seed)
    ks = jax.random.split(key, 29)
    L, D = DEPTH, D_MODEL
    nrm = lambda k, shape, s: jax.random.normal(k, shape, jnp.float32) * s
    gain = lambda k, shape: 1.0 + 0.02 * jax.random.normal(k, shape, jnp.float32)
    return {
        'x': nrm(ks[0], (BATCH, SEQ, D), 1.0),
        'c': nrm(ks[1], (BATCH, D), 1.0),
        'ctx': nrm(ks[2], (BATCH, CTX_LEN, D), 1.0),
        'c_ctx': nrm(ks[3], (D,), 1.0),
        'norm1_g': gain(ks[4], (L, D)),
        'norm2_g': gain(ks[5], (L, D)),
        'ada_w': nrm(ks[6], (L, D, 6 * D), 0.5 * D ** -0.5),
        'ada_b': nrm(ks[7], (L, 6 * D), 0.02),
        'w_in': nrm(ks[8], (L, D, IN_WIDTH), D ** -0.5),
        'b_gate': nrm(ks[9], (L, N_BRANCH * D), 0.02),
        'gla_wa': nrm(ks[10], (L, 2, GLA_GATE_RANK, GLA_QK_W), GLA_GATE_RANK ** -0.5),
        'gla_ba': nrm(ks[11], (L, 2, GLA_QK_W), 0.02),
        'gla_norm_g': gain(ks[12], (L, GLA_V_W)),
        'na_rpb': nrm(ks[13], (L, NA_HEADS, 2 * NA_KH - 1, 2 * NA_KW - 1), 0.1),
        'mla_q_norm_g': gain(ks[14], (L, MLA_Q_RANK)),
        'mla_w_q_up': nrm(ks[15], (L, MLA_Q_RANK, MLA_HEADS * (MLA_NOPE + MLA_ROPE)), MLA_Q_RANK ** -0.5),
        'mla_kv_norm_g': gain(ks[16], (L, MLA_KV_RANK)),
        'mla_w_kv_up': nrm(ks[17], (L, MLA_KV_RANK, MLA_HEADS * (MLA_NOPE + MLA_V)), MLA_KV_RANK ** -0.5),
        'w_branch_gla': nrm(ks[18], (L, GLA_V_W, D), GLA_V_W ** -0.5),
        'w_branch_na': nrm(ks[19], (L, NA_W, D), NA_W ** -0.5),
        'w_branch_mla': nrm(ks[20], (L, MLA_V_W, D), MLA_V_W ** -0.5),
        'w_out': nrm(ks[21], (L, D, D), D ** -0.5),
        'router_w': nrm(ks[22], (L, D, N_EXPERTS), D ** -0.5),
        'router_b': nrm(ks[23], (L, N_EXPERTS), 0.01),
        'moe_w1': nrm(ks[24], (L, N_EXPERTS, D, 2 * EXPERT_FF), D ** -0.5),
        'moe_b1': nrm(ks[25], (L, N_EXPERTS, 2 * EXPERT_FF), 0.02),
        'moe_w2': nrm(ks[26], (L, N_EXPERTS, EXPERT_FF, D), EXPERT_FF ** -0.5),
        'moe_b2': nrm(ks[27], (L, N_EXPERTS, D), 0.02),
        'final_norm_g': gain(ks[28], (D,)),
    }


def reference(x, c, ctx, c_ctx, norm1_g, norm2_g, ada_w, ada_b, w_in, b_gate, gla_wa, gla_ba, gla_norm_g,
              na_rpb, mla_q_norm_g, mla_w_q_up, mla_kv_norm_g, mla_w_kv_up, w_branch_gla, w_branch_na,
              w_branch_mla, w_out, router_w, router_b, moe_w1, moe_b1, moe_w2, moe_b2, final_norm_g):
    xl, xc = x, ctx
    for l in range(DEPTH):
        p = {
            'norm1_g': norm1_g[l], 'norm2_g': norm2_g[l], 'ada_w': ada_w[l], 'ada_b': ada_b[l],
            'w_in': w_in[l], 'b_gate': b_gate[l], 'gla_wa': gla_wa[l], 'gla_ba': gla_ba[l],
            'gla_norm_g': gla_norm_g[l], 'na_rpb': na_rpb[l], 'mla_q_norm_g': mla_q_norm_g[l],
            'mla_w_q_up': mla_w_q_up[l], 'mla_kv_norm_g': mla_kv_norm_g[l], 'mla_w_kv_up': mla_w_kv_up[l],
            'w_branch_gla': w_branch_gla[l], 'w_branch_na': w_branch_na[l], 'w_branch_mla': w_branch_mla[l],
            'w_out': w_out[l], 'router_w': router_w[l], 'router_b': router_b[l], 'moe_w1': moe_w1[l],
            'moe_b1': moe_b1[l], 'moe_w2': moe_w2[l], 'moe_b2': moe_b2[l],
        }
        xl, xc = _layer(xl, xc, c, c_ctx, p, l == DEPTH - 1)
    return _rmsnorm(xl, final_norm_g)
```

```python
import functools

import numpy as np
import jax
import jax.numpy as jnp
from jax import lax
from jax.experimental import pallas as pl
from jax.experimental.pallas import tpu as pltpu

F32 = jnp.float32
BF16 = jnp.bfloat16

D_MODEL = 2048
DEPTH = 2
GRID_W = 64
CTX_LEN = 256
EPS = 1e-6
ROPE_BASE = 10000.0
NEG_INF = -1e30

GLA_HEADS = 4
GLA_DK = 64
GLA_DV = 128
GLA_GATE_RANK = 16
GLA_TAU = 16.0
GLA_CHUNK = 64
NA_HEADS = 8
NA_HEAD_DIM = 64
NA_KH = 8
NA_KW = 16
NA_SCALE = NA_HEAD_DIM ** -0.5
MLA_HEADS = 8
MLA_Q_RANK = 512
MLA_KV_RANK = 512
MLA_NOPE = 128
MLA_ROPE = 64
MLA_V = 128
MLA_SCALE = (MLA_NOPE + MLA_ROPE) ** -0.5
N_BRANCH = 3
N_EXPERTS = 32
TOP_K = 4
EXPERT_FF = D_MODEL
SWIGLU_LIMIT = 7.0
SWIGLU_ALPHA = 1.702

GLA_QK_W = GLA_HEADS * GLA_DK
GLA_V_W = GLA_HEADS * GLA_DV
NA_W = NA_HEADS * NA_HEAD_DIM
MLA_V_W = MLA_HEADS * MLA_V

LANE = 128
SUBLANE = 8
ROW_TILE = 256
VMEM_LIMIT = 56 * 1024 * 1024

COL_GATE = 0
COL_GQ = COL_GATE + N_BRANCH * D_MODEL
COL_GK = COL_GQ + GLA_QK_W
COL_GV = COL_GK + GLA_QK_W
COL_GR = COL_GV + GLA_V_W
COL_NQ = COL_GR + GLA_V_W
COL_NK = COL_NQ + NA_W
COL_NV = COL_NK + NA_W
COL_MCQ = COL_NV + NA_W
COL_MCKV = COL_MCQ + MLA_Q_RANK
COL_KR = COL_MCKV + MLA_KV_RANK
COL_KRP = COL_KR + LANE
COL_GA = COL_KRP + LANE
PROJ_TN = 1536
PROJ_W = 7 * PROJ_TN
assert COL_GA + LANE <= PROJ_W

MOE_G = 1024
MOE_SB = 256
MOE_FC = 256
DISPATCH_TT = 256
COMBINE_TT = 128
ROUTE_TT = 256


def _cparams(sem):
    return pltpu.CompilerParams(dimension_semantics=sem, vmem_limit_bytes=VMEM_LIMIT)


def _nt(a, b):
    return lax.dot_general(a, b, (((1,), (1,)), ((), ())), preferred_element_type=F32)


def _tn(a, b):
    return lax.dot_general(a, b, (((0,), (0,)), ((), ())), preferred_element_type=F32)


def _mm(a, b):
    return jnp.dot(a, b, preferred_element_type=F32)


def _sigmoid(x):
    return 1.0 / (1.0 + jnp.exp(-x))


def _split_hi_lo(x):
    hi = x.astype(BF16)
    lo = (x - hi.astype(F32)).astype(BF16)
    return hi, lo


def _pack_bf16_pair(x):
    n = x.shape[1] // 2
    r = x.astype(BF16).astype(F32)
    lo = pltpu.bitcast(r[:, :n], jnp.uint32) >> 16
    hi = pltpu.bitcast(r[:, n:], jnp.uint32) & jnp.uint32(0xFFFF0000)
    return hi | lo


def _unpack_bf16_pair(w):
    lo = pltpu.bitcast(w << 16, F32)
    hi = pltpu.bitcast(w & jnp.uint32(0xFFFF0000), F32)
    return lo, hi


def _ada_kernel(c_ref, w_ref, b_ref, o_ref):
    c = c_ref[...]
    s = (c * _sigmoid(c)).astype(BF16)
    o_ref[0] = _mm(s, w_ref[0].astype(BF16)) + b_ref[0]


def _ada_mod(cc, ada_w, ada_b):
    nl, d, n6 = ada_w.shape
    tn = 1024
    return pl.pallas_call(
        _ada_kernel,
        grid=(nl, n6 // tn),
        in_specs=[pl.BlockSpec((16, d), lambda l, j: (0, 0)),
                  pl.BlockSpec((1, d, tn), lambda l, j: (l, 0, j)),
                  pl.BlockSpec((1, 1, tn), lambda l, j: (l, 0, j))],
        out_specs=pl.BlockSpec((1, 16, tn), lambda l, j: (l, 0, j)),
        out_shape=jax.ShapeDtypeStruct((nl, 16, n6), F32),
        compiler_params=_cparams(("arbitrary", "arbitrary")),
        name="ada_mod",
    )(cc, ada_w, ada_b.reshape(nl, 1, n6))


def _mod_spec(part, ctx_first):
    def imap(b, j):
        row = jnp.where(j == 0, 8, b) if ctx_first else b
        return (row * 6 + part, 0, 0)
    return pl.BlockSpec((1, 1, D_MODEL), imap)


def _norm_mod_kernel(x_ref, g_ref, sc_ref, sh_ref, o_ref):
    x = x_ref[0]
    ms = jnp.mean(x * x, axis=-1, keepdims=True)
    y = x * lax.rsqrt(ms + EPS) * g_ref[...]
    o_ref[0] = (y * (1.0 + sc_ref[0]) + sh_ref[0]).astype(o_ref.dtype)


def _norm_mod(x, g, mod):
    b, rb, d = x.shape
    return pl.pallas_call(
        _norm_mod_kernel,
        grid=(b, rb // ROW_TILE),
        in_specs=[pl.BlockSpec((1, ROW_TILE, d), lambda i, j: (i, j, 0)),
                  pl.BlockSpec((1, d), lambda i, j: (0, 0)),
                  _mod_spec(1, True), _mod_spec(0, True)],
        out_specs=pl.BlockSpec((1, ROW_TILE, d), lambda i, j: (i, j, 0)),
        out_shape=jax.ShapeDtypeStruct((b, rb, d), BF16),
        compiler_params=_cparams(("arbitrary", "arbitrary")),
        name="norm_mod",
    )(x, g.reshape(1, d), mod, mod)


def _matmul_kernel(x_ref, w_ref, o_ref):
    o_ref[...] = _mm(x_ref[...], w_ref[...]).astype(o_ref.dtype)


def _matmul(x, w, tm, tn, out_dtype):
    m, k = x.shape
    n = w.shape[1]
    return pl.pallas_call(
        _matmul_kernel,
        grid=(n // tn, m // tm),
        in_specs=[pl.BlockSpec((tm, k), lambda j, i: (i, 0)),
                  pl.BlockSpec((k, tn), lambda j, i: (0, j))],
        out_specs=pl.BlockSpec((tm, tn), lambda j, i: (i, j)),
        out_shape=jax.ShapeDtypeStruct((m, n), out_dtype),
        compiler_params=_cparams(("arbitrary", "arbitrary")),
        name="in_proj",
    )(x, w)


def _gla_kernel(q_ref, k_ref, v_ref, r_ref, ab_ref, waf_ref, wab_ref, ba_ref, g_ref, o_ref,
                laf_ref, lab_ref, of_ref, ob_ref, st_ref, *, n_ctx_chunks, n_chunks):
    c = GLA_CHUNK
    ab = ab_ref[0]
    zf = _mm(ab, waf_ref[...]) + ba_ref[0:1, :]
    zb = _mm(ab, wab_ref[...]) + ba_ref[1:2, :]
    laf_ref[...] = (jnp.minimum(zf, 0.0) - jnp.log(1.0 + jnp.exp(-jnp.abs(zf)))) * (1.0 / GLA_TAU)
    lab_ref[...] = (jnp.minimum(zb, 0.0) - jnp.log(1.0 + jnp.exp(-jnp.abs(zb)))) * (1.0 / GLA_TAU)
    st_ref[...] = jnp.zeros_like(st_ref)

    ri = lax.broadcasted_iota(jnp.int32, (c, c), 0)
    ci = lax.broadcasted_iota(jnp.int32, (c, c), 1)
    tri = [(ri >= ci), (ri <= ci)]
    tri_bf = [t.astype(F32).astype(BF16) for t in tri]
    tri4 = [jnp.concatenate([t] * GLA_HEADS, axis=0) for t in tri]
    lane_q = lax.broadcasted_iota(jnp.int32, (c, GLA_QK_W), 1) // GLA_DK
    lane_v = lax.broadcasted_iota(jnp.int32, (c, GLA_V_W), 1) // GLA_DV
    st_row_h = lax.broadcasted_iota(jnp.int32, (GLA_V_W, GLA_QK_W), 0) // GLA_DV
    st_col_h = lax.broadcasted_iota(jnp.int32, (GLA_V_W, GLA_QK_W), 1) // GLA_DK
    st_mask = st_row_h == st_col_h

    def one_dir(d, chunk, la_ref, out_ref):
        r0 = pl.multiple_of(chunk * c, c)
        la = la_ref[pl.ds(r0, c), :]
        hi, lo = _split_hi_lo(la)
        bc = _mm(tri_bf[d], hi) + _mm(tri_bf[d], lo)
        btot = bc[c - 1:c, :] if d == 0 else bc[0:1, :]
        q = q_ref[0, pl.ds(r0, c), :].astype(F32) * (GLA_DK ** -0.5)
        k = k_ref[0, pl.ds(r0, c), :].astype(F32)
        v = v_ref[0, pl.ds(r0, c), :]
        qd = (q * jnp.exp(bc)).astype(BF16)
        ki = (k * jnp.exp(-bc)).astype(BF16)
        kd = (k * jnp.exp(btot - bc)).astype(BF16)
        zero = jnp.zeros_like(qd)
        qs = jnp.concatenate([jnp.where(lane_q == h, qd, zero) for h in range(GLA_HEADS)], axis=0)
        att = _nt(qs, ki)
        att = jnp.where(tri4[d], att, 0.0).astype(BF16)
        rr = _mm(att, v)
        o = _nt(qd, st_ref[d].astype(BF16))
        for h in range(GLA_HEADS):
            o = o + jnp.where(lane_v == h, rr[h * c:(h + 1) * c, :], 0.0)
        out_ref[pl.ds(r0, c), :] = o
        upd = _tn(v, kd)
        st_ref[d] = jnp.where(st_mask, st_ref[d] * jnp.exp(btot) + upd, 0.0)

    def step(i, carry):
        one_dir(0, i, laf_ref, of_ref)
        cb = jnp.where(i < n_ctx_chunks, n_ctx_chunks - 1 - i, n_chunks + n_ctx_chunks - 1 - i)
        one_dir(1, cb, lab_ref, ob_ref)
        return carry

    lax.fori_loop(0, n_chunks, step, 0)

    def epilogue(j, carry):
        r0 = pl.multiple_of(j * ROW_TILE, ROW_TILE)
        o = of_ref[pl.ds(r0, ROW_TILE), :] + ob_ref[pl.ds(r0, ROW_TILE), :]
        r = r_ref[0, pl.ds(r0, ROW_TILE), :].astype(F32)
        gate = r * _sigmoid(r)
        for h in range(GLA_HEADS):
            sl = slice(h * GLA_DV, (h + 1) * GLA_DV)
            oh = o[:, sl]
            ms = jnp.mean(oh * oh, axis=-1, keepdims=True)
            y = oh * lax.rsqrt(ms + EPS) * g_ref[:, sl]
            o_ref[0, pl.ds(r0, ROW_TILE), sl] = (y * gate[:, sl]).astype(o_ref.dtype)
        return carry

    lax.fori_loop(0, (n_chunks * c) // ROW_TILE, epilogue, 0)


def _gla(p3, waf, wab, ba, g):
    b, rb, _ = p3.shape
    n_chunks = rb // GLA_CHUNK
    kern = functools.partial(_gla_kernel, n_ctx_chunks=CTX_LEN // GLA_CHUNK, n_chunks=n_chunks)

    def col(width, off):
        return pl.BlockSpec((1, rb, width), lambda i: (i, 0, off // width))

    return pl.pallas_call(
        kern,
        grid=(b,),
        in_specs=[col(GLA_QK_W, COL_GQ), col(GLA_QK_W, COL_GK), col(GLA_V_W, COL_GV), col(GLA_V_W, COL_GR),
                  col(LANE, COL_GA),
                  pl.BlockSpec((LANE, GLA_QK_W), lambda i: (0, 0)),
                  pl.BlockSpec((LANE, GLA_QK_W), lambda i: (0, 0)),
                  pl.BlockSpec((2, GLA_QK_W), lambda i: (0, 0)),
                  pl.BlockSpec((1, GLA_V_W), lambda i: (0, 0))],
        out_specs=pl.BlockSpec((1, rb, GLA_V_W), lambda i: (i, 0, 0)),
        out_shape=jax.ShapeDtypeStruct((b, rb, GLA_V_W), BF16),
        scratch_shapes=[pltpu.VMEM((rb, GLA_QK_W), F32), pltpu.VMEM((rb, GLA_QK_W), F32),
                        pltpu.VMEM((rb, GLA_V_W), F32), pltpu.VMEM((rb, GLA_V_W), F32),
                        pltpu.VMEM((2, GLA_V_W, GLA_QK_W), F32)],
        compiler_params=_cparams(("arbitrary",)),
        name="gla",
    )(p3, p3, p3, p3, p3, waf, wab, ba, g)


def _softmax_pv(parts):
    m = parts[0][0].max(axis=-1, keepdims=True)
    for s, _ in parts[1:]:
        m = jnp.maximum(m, s.max(axis=-1, keepdims=True))
    acc = None
    den = None
    for s, v in parts:
        e = jnp.exp(s - m)
        l = e.sum(axis=-1, keepdims=True)
        pv = _mm(e.astype(BF16), v)
        acc = pv if acc is None else acc + pv
        den = l if den is None else den + l
    return acc / den


def _na_kernel(q_ref, k_ref, v_ref, bias_ref, o_ref, *, n_rows):
    kh = NA_KH
    lane = lax.broadcasted_iota(jnp.int32, (GRID_W, LANE), 1)
    head_mask = [lane < NA_HEAD_DIM, lane >= NA_HEAD_DIM]
    kc = k_ref[0, 0:CTX_LEN, :]
    vc = v_ref[0, 0:CTX_LEN, :]

    def row(r, carry):
        rs = jnp.clip(r - kh // 2, 0, n_rows - kh)
        off = rs - r + (NA_KH - 1)
        q0 = pl.multiple_of(CTX_LEN + r * GRID_W, GRID_W)
        k0 = pl.multiple_of(CTX_LEN + rs * GRID_W, GRID_W)
        q = q_ref[0, pl.ds(q0, GRID_W), :]
        kl = k_ref[0, pl.ds(k0, kh * GRID_W), :]
        vl = v_ref[0, pl.ds(k0, kh * GRID_W), :]
        outs = []
        for hh in range(2):
            qm = jnp.where(head_mask[hh], q, jnp.zeros_like(q))
            s_loc = _nt(qm, kl) * NA_SCALE + bias_ref[hh, off]
            s_ctx = _nt(qm, kc) * NA_SCALE
            outs.append(_softmax_pv([(s_loc, vl), (s_ctx, vc)]))
        o_ref[0, pl.ds(q0, GRID_W), :] = jnp.where(head_mask[0], outs[0], outs[1]).astype(o_ref.dtype)
        return carry

    lax.fori_loop(0, n_rows, row, 0)

    qc = q_ref[0, 0:CTX_LEN, :]
    lane_c = lax.broadcasted_iota(jnp.int32, (CTX_LEN, LANE), 1)
    outs = []
    for hh in range(2):
        msk = (lane_c < NA_HEAD_DIM) if hh == 0 else (lane_c >= NA_HEAD_DIM)
        qm = jnp.where(msk, qc, jnp.zeros_like(qc))
        outs.append(_softmax_pv([(_nt(qm, kc) * NA_SCALE, vc)]))
    o_ref[0, 0:CTX_LEN, :] = jnp.where(lane_c < NA_HEAD_DIM, outs[0], outs[1]).astype(o_ref.dtype)


def _na_bias_table(rpb):
    o = np.arange(NA_KH)[:, None] + np.arange(NA_KH)[None, :]
    qcol = np.arange(GRID_W)[:, None]
    kcol = np.arange(GRID_W)[None, :]
    wstart = np.clip(qcol - NA_KW // 2, 0, GRID_W - NA_KW)
    in_win = (kcol >= wstart) & (kcol < wstart + NA_KW)
    dc = np.clip(kcol - qcol + NA_KW - 1, 0, 2 * NA_KW - 2)
    t = rpb[:, o]
    t = t[:, :, :, dc]
    t = jnp.where(in_win[None, None, None], t, NEG_INF)
    t = jnp.transpose(t, (0, 1, 3, 2, 4))
    return t.reshape(NA_HEADS, NA_KH, GRID_W, NA_KH * GRID_W).astype(F32)


def _na(p3, bias):
    b, rb, _ = p3.shape
    n_rows = (rb - CTX_LEN) // GRID_W
    kern = functools.partial(_na_kernel, n_rows=n_rows)

    def col(off):
        return pl.BlockSpec((1, rb, LANE), lambda p, i: (i, 0, off // LANE + p))

    return pl.pallas_call(
        kern,
        grid=(NA_HEADS // 2, b),
        in_specs=[col(COL_NQ), col(COL_NK), col(COL_NV),
                  pl.BlockSpec((2, NA_KH, GRID_W, NA_KH * GRID_W), lambda p, i: (p, 0, 0, 0))],
        out_specs=pl.BlockSpec((1, rb, LANE), lambda p, i: (i, 0, p)),
        out_shape=jax.ShapeDtypeStruct((b, rb, NA_W), BF16),
        compiler_params=_cparams(("arbitrary", "arbitrary")),
        name="na",
    )(p3, p3, p3, bias)


def _mla_proj_kernel(cq_ref, ckv_ref, kr_ref, krp_ref, cos_ref, sin_ref, gq_ref, gkv_ref,
                     wq_ref, wq2_ref, wkv_ref, q_ref, k_ref, v_ref):
    def norm(x, g):
        x = x.astype(F32)
        ms = jnp.mean(x * x, axis=-1, keepdims=True)
        return (x * lax.rsqrt(ms + EPS) * g).astype(BF16)

    cos = cos_ref[...]
    sin = sin_ref[...]
    nq = norm(cq_ref[0], gq_ref[...])
    yq = _mm(nq, wq_ref[...])
    yq2 = _mm(nq, wq2_ref[...])
    nkv = norm(ckv_ref[0], gkv_ref[...])
    ykv = _mm(nkv, wkv_ref[...])
    k_rot = (kr_ref[0].astype(F32) * cos + krp_ref[0].astype(F32) * sin).astype(BF16)
    for h in range(MLA_HEADS):
        a = 2 * h * LANE
        q_ref[0, :, a:a + LANE] = (yq[:, a:a + LANE] * MLA_SCALE).astype(BF16)
        rot = yq[:, a + LANE:a + 2 * LANE] * cos + yq2[:, h * LANE:(h + 1) * LANE] * sin
        q_ref[0, :, a + LANE:a + 2 * LANE] = (rot * MLA_SCALE).astype(BF16)
        k_ref[0, :, a:a + LANE] = ykv[:, h * LANE:(h + 1) * LANE].astype(BF16)
        k_ref[0, :, a + LANE:a + 2 * LANE] = k_rot
    v_ref[0] = ykv[:, MLA_HEADS * MLA_NOPE:].astype(BF16)


def _mla_proj(p3, cos_t, sin_t, gq, gkv, wq, wq2, wkv):
    b, rb, _ = p3.shape
    tm = ROW_TILE

    def col(width, off):
        return pl.BlockSpec((1, tm, width), lambda i, j: (i, j, off // width))

    def full(a):
        return pl.BlockSpec(a.shape, lambda i, j: (0, 0))

    hw = MLA_HEADS * 2 * LANE
    return pl.pallas_call(
        _mla_proj_kernel,
        grid=(b, rb // tm),
        in_specs=[col(MLA_Q_RANK, COL_MCQ), col(MLA_KV_RANK, COL_MCKV), col(LANE, COL_KR), col(LANE, COL_KRP),
                  pl.BlockSpec((tm, LANE), lambda i, j: (j, 0)), pl.BlockSpec((tm, LANE), lambda i, j: (j, 0)),
                  full(gq), full(gkv), full(wq), full(wq2), full(wkv)],
        out_specs=[pl.BlockSpec((1, tm, hw), lambda i, j: (i, j, 0)),
                   pl.BlockSpec((1, tm, hw), lambda i, j: (i, j, 0)),
                   pl.BlockSpec((1, tm, MLA_V_W), lambda i, j: (i, j, 0))],
        out_shape=[jax.ShapeDtypeStruct((b, rb, hw), BF16), jax.ShapeDtypeStruct((b, rb, hw), BF16),
                   jax.ShapeDtypeStruct((b, rb, MLA_V_W), BF16)],
        compiler_params=_cparams(("arbitrary", "arbitrary")),
        name="mla_proj",
    )(p3, p3, p3, p3, cos_t, sin_t, gq, gkv, wq, wq2, wkv)


def _mla_attn_kernel(q_ref, k_ref, v_ref, o_ref):
    j = pl.program_id(2)
    q = q_ref[0]

    @pl.when(j > 0)
    def _():
        o_ref[0] = _softmax_pv([(_nt(q, k_ref[0]), v_ref[0])]).astype(o_ref.dtype)

    @pl.when(j == 0)
    def _():
        o_ref[0] = _softmax_pv([(_nt(q, k_ref[0, 0:CTX_LEN, :]), v_ref[0, 0:CTX_LEN, :])]).astype(o_ref.dtype)


def _mla_attn(q, k, v):
    b, rb, _ = q.shape
    tq = ROW_TILE
    return pl.pallas_call(
        _mla_attn_kernel,
        grid=(b, MLA_HEADS, rb // tq),
        in_specs=[pl.BlockSpec((1, tq, 2 * LANE), lambda i, h, j: (i, j, h)),
                  pl.BlockSpec((1, rb, 2 * LANE), lambda i, h, j: (i, 0, h)),
                  pl.BlockSpec((1, rb, MLA_V), lambda i, h, j: (i, 0, h))],
        out_specs=pl.BlockSpec((1, tq, MLA_V), lambda i, h, j: (i, j, h)),
        out_shape=jax.ShapeDtypeStruct((b, rb, MLA_V_W), BF16),
        compiler_params=_cparams(("arbitrary", "arbitrary", "arbitrary")),
        name="mla_attn",
    )(q, k, v)


def _merge_kernel(x_ref, oa_ref, ob_ref, oc_ref, ga_ref, gb_ref, gc_ref, bg_ref,
                  wa_ref, wb_ref, wc_ref, wo_ref, gt1_ref, sc2_ref, sh2_ref, g2_ref, rw_ref, rb_ref,
                  xo_ref, h_ref, lg_ref):
    d = D_MODEL

    def gate(g_ref, k):
        return _sigmoid(g_ref[0].astype(F32) + bg_ref[:, k * d:(k + 1) * d])

    m = gate(ga_ref, 0) * _mm(oa_ref[0], wa_ref[...])
    m = m + gate(gb_ref, 1) * _mm(ob_ref[0], wb_ref[...])
    m = m + gate(gc_ref, 2) * _mm(oc_ref[0], wc_ref[...])
    y = _mm(m.astype(BF16), wo_ref[...])
    x = x_ref[0] + gt1_ref[0] * y
    xo_ref[0] = x
    ms = jnp.mean(x * x, axis=-1, keepdims=True)
    h = x * lax.rsqrt(ms + EPS) * g2_ref[...]
    h = h * (1.0 + sc2_ref[0]) + sh2_ref[0]
    h_ref[...] = _pack_bf16_pair(h)
    hh, hl = _split_hi_lo(h)
    wh, wl = _split_hi_lo(rw_ref[...])
    lg_ref[...] = _nt(wh, hh) + _nt(wh, hl) + _nt(wl, hh) + rb_ref[:, 0:1]


def _merge(x, p3, og, on, om, bg, wa, wb, wc, wo, mod, g2, rwt, rbias, skip_ctx):
    b, rb, d = x.shape
    tm = ROW_TILE
    jo = 1 if skip_ctx else 0
    nj = rb // tm - jo
    rows_out = nj * tm

    def rows(width, cblk=0):
        return pl.BlockSpec((1, tm, width), lambda i, j: (i, j + jo, cblk))

    def full(a):
        return pl.BlockSpec(a.shape, lambda i, j: (0,) * a.ndim, pipeline_mode=pl.Buffered(1))

    def mod_spec(part):
        def imap(i, j):
            row = i if skip_ctx else jnp.where(j == 0, 8, i)
            return (row * 6 + part, 0, 0)
        return pl.BlockSpec((1, 1, d), imap)

    return pl.pallas_call(
        _merge_kernel,
        grid=(b, nj),
        in_specs=[rows(d), rows(GLA_V_W), rows(NA_W), rows(MLA_V_W),
                  rows(d, 0), rows(d, 1), rows(d, 2), full(bg),
                  full(wa), full(wb), full(wc), full(wo),
                  mod_spec(2), mod_spec(4), mod_spec(3), full(g2), full(rwt), full(rbias)],
        out_specs=[pl.BlockSpec((1, tm, d), lambda i, j: (i, j, 0)),
                   pl.BlockSpec((tm, d // 2), lambda i, j: (i * nj + j, 0)),
                   pl.BlockSpec((N_EXPERTS, tm), lambda i, j: (0, i * nj + j))],
        out_shape=[jax.ShapeDtypeStruct((b, rows_out, d), F32),
                   jax.ShapeDtypeStruct((b * rows_out, d // 2), jnp.uint32),
                   jax.ShapeDtypeStruct((N_EXPERTS, b * rows_out), F32)],
        compiler_params=_cparams(("arbitrary", "arbitrary")),
        name="merge",
    )(x, og, on, om, p3, p3, p3, bg, wa, wb, wc, wo, mod, mod, mod, g2, rwt, rbias)


def _route_kernel(l_ref, idx_ref, w_ref, rank_ref, cnt_ref, carry_ref):
    i = pl.program_id(0)
    tt = l_ref.shape[1]

    @pl.when(i == 0)
    def _():
        carry_ref[...] = jnp.zeros_like(carry_ref)

    l = l_ref[...]
    eio = lax.broadcasted_iota(jnp.int32, (N_EXPERTS, tt), 0)
    vals, idxs = [], []
    for _ in range(TOP_K):
        m = l.max(axis=0, keepdims=True)
        ik = jnp.min(jnp.where(l == m, eio, N_EXPERTS), axis=0, keepdims=True)
        vals.append(m)
        idxs.append(ik)
        l = jnp.where(eio == ik, -jnp.inf, l)
    es = [jnp.exp(v - vals[0]) for v in vals]
    den = es[0] + es[1] + es[2] + es[3]
    sel = jnp.zeros((N_EXPERTS, tt), F32)
    for ik in idxs:
        sel = sel + (eio == ik).astype(F32)
    si = lax.broadcasted_iota(jnp.int32, (tt, tt), 0)
    ti = lax.broadcasted_iota(jnp.int32, (tt, tt), 1)
    before = (si < ti).astype(F32).astype(BF16)
    rank_full = _mm(sel.astype(BF16), before) + carry_ref[:, 0:1]
    for k in range(TOP_K):
        idx_ref[k:k + 1, :] = idxs[k]
        w_ref[k:k + 1, :] = es[k] / den
        rk = jnp.sum(jnp.where(eio == idxs[k], rank_full, 0.0), axis=0, keepdims=True)
        rank_ref[k:k + 1, :] = rk.astype(jnp.int32)
    carry_ref[...] = carry_ref[...] + jnp.sum(sel, axis=1, keepdims=True)
    cnt_ref[...] = carry_ref[...]


def _route(logits_t):
    ne, t = logits_t.shape
    tt = ROUTE_TT
    spec4 = pl.BlockSpec((TOP_K, tt), lambda i: (0, i))
    return pl.pallas_call(
        _route_kernel,
        grid=(t // tt,),
        in_specs=[pl.BlockSpec((ne, tt), lambda i: (0, i))],
        out_specs=[spec4, spec4, spec4, pl.BlockSpec((ne, LANE), lambda i: (0, 0))],
        out_shape=[jax.ShapeDtypeStruct((TOP_K, t), jnp.int32), jax.ShapeDtypeStruct((TOP_K, t), F32),
                   jax.ShapeDtypeStruct((TOP_K, t), jnp.int32), jax.ShapeDtypeStruct((ne, LANE), F32)],
        scratch_shapes=[pltpu.VMEM((ne, LANE), F32)],
        compiler_params=_cparams(("arbitrary",)),
        name="route",
    )(logits_t)


def _pad_fill(ps_ref, pn_ref, zero_ref, xg_ref, sem, wait):
    def copy(pos, rows):
        cp = pltpu.make_async_copy(zero_ref.at[pl.ds(0, rows), :], xg_ref.at[pl.ds(pos, rows), :], sem)
        cp.wait() if wait else cp.start()

    def per_expert(e, carry):
        pos = ps_ref[e]
        head = (-pos) & (SUBLANE - 1)
        for r in range(SUBLANE - 1):
            @pl.when(r < head)
            def _(r=r):
                copy(pos + r, 1)

        pos = pos + head
        n = pn_ref[e] - head
        bit = MOE_G // 2
        while bit >= SUBLANE:
            on = (n & bit) != 0

            @pl.when(on)
            def _(pos=pos, bit=bit):
                copy(pl.multiple_of(pos, SUBLANE), bit)

            pos = pos + jnp.where(on, bit, 0)
            bit //= 2
        return carry

    lax.fori_loop(0, N_EXPERTS, per_expert, 0)

    zr = zero_ref.shape[0]

    def tail(i, carry):
        pos = pl.multiple_of(ps_ref[N_EXPERTS] + i * zr, zr)
        cp = pltpu.make_async_copy(zero_ref, xg_ref.at[pl.ds(pos, zr), :], sem)
        cp.wait() if wait else cp.start()
        return carry

    lax.fori_loop(0, pn_ref[N_EXPERTS], tail, 0)


def _dispatch_kernel(ps_ref, pn_ref, dest_ref, h_ref, xg_ref, zero_ref, sem, zsem):
    tt = h_ref.shape[0]

    @pl.when(pl.program_id(0) == 0)
    def _():
        zero_ref[...] = jnp.zeros_like(zero_ref)
        _pad_fill(ps_ref, pn_ref, zero_ref, xg_ref, zsem, wait=False)

    def issue(t, carry):
        for k in range(TOP_K):
            pltpu.make_async_copy(h_ref.at[pl.ds(t, 1), :], xg_ref.at[pl.ds(dest_ref[0, k, t], 1), :], sem).start()
        return carry

    lax.fori_loop(0, tt, issue, 0)
    for k in range(TOP_K):
        pltpu.make_async_copy(h_ref, xg_ref.at[pl.ds(0, tt), :], sem).wait()

    @pl.when(pl.program_id(0) == 0)
    def _():
        _pad_fill(ps_ref, pn_ref, zero_ref, xg_ref, zsem, wait=True)


def _dispatch(hp, dest, pad_start, pad_len, n_slots):
    t, w = hp.shape
    tt = DISPATCH_TT
    dest3 = dest.reshape(TOP_K, t // tt, tt).transpose(1, 0, 2)
    grid_spec = pltpu.PrefetchScalarGridSpec(
        num_scalar_prefetch=2,
        grid=(t // tt,),
        in_specs=[pl.BlockSpec((1, TOP_K, tt), lambda i, ps, pn: (i, 0, 0), memory_space=pltpu.SMEM),
                  pl.BlockSpec((tt, w), lambda i, ps, pn: (i, 0))],
        out_specs=pl.BlockSpec(memory_space=pl.ANY),
        scratch_shapes=[pltpu.VMEM((MOE_G // 2, w), jnp.uint32), pltpu.SemaphoreType.DMA(()),
                        pltpu.SemaphoreType.DMA(())],
    )
    return pl.pallas_call(
        _dispatch_kernel,
        grid_spec=grid_spec,
        out_shape=jax.ShapeDtypeStruct((n_slots, w), jnp.uint32),
        compiler_params=_cparams(("arbitrary",)),
        name="dispatch",
    )(pad_start, pad_len, dest3, hp)


def _ffn_kernel(te_ref, tv_ref, x_ref, w1g_ref, w1l_ref, b1g_ref, b1l_ref, w2_ref, b2_ref, y_ref,
                xb_ref, acc_ref, *, n_fc):
    i = pl.program_id(0)
    j = pl.program_id(1)
    valid = tv_ref[i]
    g = x_ref.shape[0]
    half = x_ref.shape[1]

    @pl.when(valid > 0)
    def _():
        wg = w1g_ref[0].astype(BF16)
        wl = w1l_ref[0].astype(BF16)
        w2 = w2_ref[0].astype(BF16)
        for sb in range(g // MOE_SB):
            rows = pl.ds(sb * MOE_SB, MOE_SB)

            @pl.when(sb * MOE_SB < valid)
            def _():
                @pl.when(j == 0)
                def _():
                    lo, hi = _unpack_bf16_pair(x_ref[rows, :])
                    rid = lax.broadcasted_iota(jnp.int32, (MOE_SB, half), 0) + sb * MOE_SB
                    ok = rid < valid
                    xb_ref[rows, 0:half] = jnp.where(ok, lo, 0.0).astype(BF16)
                    xb_ref[rows, half:2 * half] = jnp.where(ok, hi, 0.0).astype(BF16)

                x = xb_ref[rows, :]
                ug = _mm(x, wg) + b1g_ref[0]
                ul = _mm(x, wl) + b1l_ref[0]
                xg = jnp.minimum(ug, SWIGLU_LIMIT)
                xl = jnp.clip(ul, -SWIGLU_LIMIT, SWIGLU_LIMIT)
                act = xg * _sigmoid(SWIGLU_ALPHA * xg) * (xl + 1.0)
                contrib = _mm(act.astype(BF16), w2)

                @pl.when(j == 0)
                def _():
                    acc_ref[rows, :] = contrib + b2_ref[0]

                @pl.when(j > 0)
                def _():
                    acc_ref[rows, :] = acc_ref[rows, :] + contrib

                @pl.when(j == n_fc - 1)
                def _():
                    y_ref[rows, :] = _pack_bf16_pair(acc_ref[rows, :])

    @pl.when(j == n_fc - 1)
    def _():
        for sb in range(g // MOE_SB):
            @pl.when(sb * MOE_SB >= valid)
            def _():
                y_ref[pl.ds(sb * MOE_SB, MOE_SB), :] = jnp.zeros((MOE_SB, half), jnp.uint32)


def _ffn(tile_expert, tile_valid, xg, w1, b1, w2, b2):
    n_slots, half = xg.shape
    ne, d, ff2 = w1.shape
    ff = ff2 // 2
    n_fc = ff // MOE_FC
    n_tiles = n_slots // MOE_G
    kern = functools.partial(_ffn_kernel, n_fc=n_fc)

    def jj(j, tv, i):
        return jnp.where(tv[i] > 0, j, n_fc - 1)

    grid_spec = pltpu.PrefetchScalarGridSpec(
        num_scalar_prefetch=2,
        grid=(n_tiles, n_fc),
        in_specs=[pl.BlockSpec((MOE_G, half), lambda i, j, te, tv: (i, 0)),
                  pl.BlockSpec((1, d, MOE_FC), lambda i, j, te, tv: (te[i], 0, jj(j, tv, i))),
                  pl.BlockSpec((1, d, MOE_FC), lambda i, j, te, tv: (te[i], 0, jj(j, tv, i) + n_fc)),
                  pl.BlockSpec((1, 1, MOE_FC), lambda i, j, te, tv: (te[i], 0, jj(j, tv, i))),
                  pl.BlockSpec((1, 1, MOE_FC), lambda i, j, te, tv: (te[i], 0, jj(j, tv, i) + n_fc)),
                  pl.BlockSpec((1, MOE_FC, d), lambda i, j, te, tv: (te[i], jj(j, tv, i), 0)),
                  pl.BlockSpec((1, 1, d), lambda i, j, te, tv: (te[i], 0, 0))],
        out_specs=pl.BlockSpec((MOE_G, half), lambda i, j, te, tv: (i, 0)),
        scratch_shapes=[pltpu.VMEM((MOE_G, d), BF16), pltpu.VMEM((MOE_G, d), F32)],
    )
    return pl.pallas_call(
        kern,
        grid_spec=grid_spec,
        out_shape=jax.ShapeDtypeStruct((n_slots, half), jnp.uint32),
        compiler_params=_cparams(("arbitrary", "arbitrary")),
        name="moe_ffn",
    )(tile_expert, tile_valid, xg, w1, w1, b1.reshape(ne, 1, ff2), b1.reshape(ne, 1, ff2), w2,
      b2.reshape(ne, 1, d))


def _combine_kernel(dest_ref, destn_ref, x_ref, wt_ref, gt2_ref, gf_ref, yg_ref, o_ref, buf_ref, sem,
                    *, final_norm):
    i = pl.program_id(0)
    n = pl.num_programs(0)
    tt = x_ref.shape[1]
    slot = i % 2

    def issue(d_ref, s):
        def body(t, carry):
            for k in range(TOP_K):
                pltpu.make_async_copy(yg_ref.at[pl.ds(d_ref[0, k, t], 1), :],
                                      buf_ref.at[s, k, pl.ds(t, 1), :], sem.at[s]).start()
            return carry
        lax.fori_loop(0, tt, body, 0)

    @pl.when(i == 0)
    def _():
        issue(dest_ref, 0)

    @pl.when(i + 1 < n)
    def _():
        issue(destn_ref, 1 - slot)

    for k in range(TOP_K):
        pltpu.make_async_copy(yg_ref.at[pl.ds(0, tt), :], buf_ref.at[slot, k], sem.at[slot]).wait()

    wt = wt_ref[...]
    acc_lo = None
    acc_hi = None
    for k in range(TOP_K):
        lo, hi = _unpack_bf16_pair(buf_ref[slot, k])
        wk = wt[:, k:k + 1]
        acc_lo = lo * wk if acc_lo is None else acc_lo + lo * wk
        acc_hi = hi * wk if acc_hi is None else acc_hi + hi * wk
    y = jnp.concatenate([acc_lo, acc_hi], axis=1)
    x = x_ref[0] + gt2_ref[0] * y
    if final_norm:
        ms = jnp.mean(x * x, axis=-1, keepdims=True)
        x = x * lax.rsqrt(ms + EPS) * gf_ref[...]
    o_ref[0] = x


def _combine(x, yg, dest, wts, mod, gf, ctx_first, final_norm):
    b, rows, d = x.shape
    tt = COMBINE_TT
    nj = rows // tt
    t = b * rows
    nt = t // tt
    dest3 = dest.reshape(TOP_K, nt, tt).transpose(1, 0, 2)
    wt = wts.T
    kern = functools.partial(_combine_kernel, final_norm=final_norm)

    def mod_imap(i):
        bi = i // nj
        row = jnp.where((i % nj) * tt < CTX_LEN, 8, bi) if ctx_first else bi
        return (row * 6 + 5, 0, 0)

    return pl.pallas_call(
        kern,
        grid=(nt,),
        in_specs=[pl.BlockSpec((1, TOP_K, tt), lambda i: (i, 0, 0), memory_space=pltpu.SMEM),
                  pl.BlockSpec((1, TOP_K, tt), lambda i: (jnp.minimum(i + 1, nt - 1), 0, 0),
                               memory_space=pltpu.SMEM),
                  pl.BlockSpec((1, tt, d), lambda i: (i // nj, i % nj, 0)),
                  pl.BlockSpec((tt, TOP_K), lambda i: (i, 0)),
                  pl.BlockSpec((1, 1, d), mod_imap),
                  pl.BlockSpec((1, d), lambda i: (0, 0)),
                  pl.BlockSpec(memory_space=pl.ANY)],
        out_specs=pl.BlockSpec((1, tt, d), lambda i: (i // nj, i % nj, 0)),
        out_shape=jax.ShapeDtypeStruct((b, rows, d), F32),
        scratch_shapes=[pltpu.VMEM((2, TOP_K, tt, d // 2), jnp.uint32), pltpu.SemaphoreType.DMA((2,))],
        compiler_params=_cparams(("arbitrary",)),
        name="combine",
    )(dest3, dest3, x, wt, mod, gf.reshape(1, d), yg)


def _proj_weight(w_in):
    d = w_in.shape[0]
    splits = (GLA_QK_W, GLA_QK_W, GLA_V_W, GLA_V_W, GLA_GATE_RANK, GLA_GATE_RANK,
              NA_W, NA_W, NA_W, MLA_Q_RANK, MLA_KV_RANK, MLA_ROPE, N_BRANCH * D_MODEL)
    pts = np.cumsum((0,) + splits)
    (gq, gk, gv, gr, gaf, gab, nq, nk, nv, mcq, mckv, mkr, gate) = [w_in[:, pts[i]:pts[i + 1]] for i in range(13)]
    q16 = MLA_ROPE // 4
    mkrp = jnp.concatenate([mkr[:, q16:2 * q16], mkr[:, :q16], mkr[:, 3 * q16:], mkr[:, 2 * q16:3 * q16]], axis=1)
    z = lambda n: jnp.zeros((d, n), w_in.dtype)
    cols = [gate, gq, gk, gv, gr, nq, nk, nv, mcq, mckv,
            mkr, z(LANE - MLA_ROPE), mkrp, z(LANE - MLA_ROPE),
            gaf, gab, z(LANE - 2 * GLA_GATE_RANK)]
    w = jnp.concatenate(cols, axis=1)
    w = jnp.concatenate([w, z(PROJ_W - w.shape[1])], axis=1)
    return w.astype(BF16)


def _rope_tables(rb):
    n = rb - CTX_LEN
    t = np.arange(n)
    nf = MLA_ROPE // 4
    freqs = ROPE_BASE ** (-np.arange(nf, dtype=np.float64) / nf)
    cos = np.zeros((rb, LANE), np.float32)
    sin = np.zeros((rb, LANE), np.float32)
    cos[:CTX_LEN, :MLA_ROPE] = 1.0
    for a, pos in enumerate((t // GRID_W, t % GRID_W)):
        ang = (pos.astype(np.float32)[:, None] * freqs.astype(np.float32)[None, :]).astype(np.float32)
        c, s = np.cos(ang), np.sin(ang)
        base = a * 2 * nf
        cos[CTX_LEN:, base:base + nf] = c
        cos[CTX_LEN:, base + nf:base + 2 * nf] = c
        sin[CTX_LEN:, base:base + nf] = -s
        sin[CTX_LEN:, base + nf:base + 2 * nf] = s
    return jnp.asarray(cos), jnp.asarray(sin)


def _mla_weights(w_q_up, w_kv_up):
    r = w_q_up.shape[0]
    wq = w_q_up.reshape(r, MLA_HEADS, MLA_NOPE + MLA_ROPE)
    nope, rope = wq[..., :MLA_NOPE], wq[..., MLA_NOPE:]
    q16 = MLA_ROPE // 4
    ropep = jnp.concatenate([rope[..., q16:2 * q16], rope[..., :q16], rope[..., 3 * q16:], rope[..., 2 * q16:3 * q16]],
                            axis=-1)
    zpad = jnp.zeros((r, MLA_HEADS, LANE - MLA_ROPE), w_q_up.dtype)
    wq1 = jnp.concatenate([nope, rope, zpad], axis=-1).reshape(r, MLA_HEADS * 2 * LANE).astype(BF16)
    wq2 = jnp.concatenate([ropep, zpad], axis=-1).reshape(r, MLA_HEADS * LANE).astype(BF16)
    rk = w_kv_up.shape[0]
    wkv = w_kv_up.reshape(rk, MLA_HEADS, 2, MLA_NOPE).transpose(0, 2, 1, 3).reshape(rk, 2 * MLA_HEADS * MLA_NOPE)
    return wq1, wq2, wkv.astype(BF16)


def _moe_plan(idx, rank, counts, n_tiles):
    cnt = counts[:, 0].astype(jnp.int32)
    padded = ((cnt + MOE_G - 1) // MOE_G) * MOE_G
    ends = jnp.cumsum(padded)
    starts = ends - padded
    dest = starts[idx] + rank
    tile_start = jnp.arange(n_tiles, dtype=jnp.int32) * MOE_G
    te = jnp.sum((tile_start[:, None] >= ends[None, :]).astype(jnp.int32), axis=1)
    active = te < N_EXPERTS
    te_c = jnp.minimum(te, N_EXPERTS - 1)
    valid = jnp.clip(cnt[te_c] - (tile_start - starts[te_c]), 0, MOE_G)
    valid = jnp.where(active, valid, 0)
    last_e = jnp.max(jnp.where(cnt > 0, jnp.arange(N_EXPERTS, dtype=jnp.int32), 0))
    te_f = jnp.where(active, te_c, last_e)
    tail_blocks = (n_tiles * MOE_G - ends[-1]) // (MOE_G // 2)
    pad_start = jnp.concatenate([starts + cnt, ends[-1:]]).astype(jnp.int32)
    pad_len = jnp.concatenate([padded - cnt, tail_blocks[None]]).astype(jnp.int32)
    return dest, te_f, valid, pad_start, pad_len


def kernel(x, c, ctx, c_ctx, norm1_g, norm2_g, ada_w, ada_b, w_in, b_gate, gla_wa, gla_ba, gla_norm_g,
           na_rpb, mla_q_norm_g, mla_w_q_up, mla_kv_norm_g, mla_w_kv_up, w_branch_gla, w_branch_na,
           w_branch_mla, w_out, router_w, router_b, moe_w1, moe_b1, moe_w2, moe_b2, final_norm_g):
    b, n, d = x.shape
    rb = CTX_LEN + n
    assert b <= 8 and d == D_MODEL and ctx.shape[1] == CTX_LEN

    cc = jnp.zeros((16, d), F32).at[:b].set(c).at[8].set(c_ctx)
    mod_all = _ada_mod(cc, ada_w, ada_b)
    cos_t, sin_t = _rope_tables(rb)
    xs = jnp.concatenate([ctx, x], axis=1)

    for l in range(DEPTH):
        last = l == DEPTH - 1
        mod = mod_all[l].reshape(16 * 6, 1, d)
        h = _norm_mod(xs, norm1_g[l], mod)
        p = _matmul(h.reshape(b * rb, d), _proj_weight(w_in[l]), 1024 if (b * rb) % 1024 == 0 else ROW_TILE,
                    PROJ_TN, BF16)
        p3 = p.reshape(b, rb, PROJ_W)

        zpad = jnp.zeros((LANE - 2 * GLA_GATE_RANK, GLA_QK_W), F32)
        zr = jnp.zeros((GLA_GATE_RANK, GLA_QK_W), F32)
        waf = jnp.concatenate([gla_wa[l, 0], zr, zpad], axis=0).astype(BF16)
        wab = jnp.concatenate([zr, gla_wa[l, 1], zpad], axis=0).astype(BF16)
        og = _gla(p3, waf, wab, gla_ba[l], gla_norm_g[l].reshape(1, GLA_V_W))

        on = _na(p3, _na_bias_table(na_rpb[l]))

        wq1, wq2, wkv = _mla_weights(mla_w_q_up[l], mla_w_kv_up[l])
        q_m, k_m, v_m = _mla_proj(p3, cos_t, sin_t, mla_q_norm_g[l].reshape(1, -1), mla_kv_norm_g[l].reshape(1, -1),
                                  wq1, wq2, wkv)
        om = _mla_attn(q_m, k_m, v_m)

        xs, hp, logits_t = _merge(
            xs, p3, og, on, om, b_gate[l].reshape(1, -1),
            w_branch_gla[l].astype(BF16), w_branch_na[l].astype(BF16), w_branch_mla[l].astype(BF16),
            w_out[l].astype(BF16), mod, norm2_g[l].reshape(1, d), router_w[l].T,
            jnp.broadcast_to(router_b[l][:, None], (N_EXPERTS, LANE)), skip_ctx=last)

        t_tok = hp.shape[0]
        idx, wts, rank, counts = _route(logits_t)
        n_tiles = (TOP_K * t_tok) // MOE_G + N_EXPERTS
        dest, te, tv, pad_start, pad_len = _moe_plan(idx, rank, counts, n_tiles)
        xg = _dispatch(hp, dest, pad_start, pad_len, n_tiles * MOE_G)
        yg = _ffn(te, tv, xg, moe_w1[l], moe_b1[l], moe_w2[l], moe_b2[l])
        xs = _combine(xs, yg, dest, wts, mod, final_norm_g, ctx_first=not last, final_norm=last)
    return xs
```

```python
import functools

import numpy as np
import jax
import jax.numpy as jnp
from jax import lax
from jax.experimental import pallas as pl
from jax.experimental.pallas import tpu as pltpu

F32 = jnp.float32
BF16 = jnp.bfloat16

D_MODEL = 2048
DEPTH = 2
GRID_W = 64
CTX_LEN = 256
EPS = 1e-6
ROPE_BASE = 10000.0
NEG_INF = -1e30

GLA_HEADS = 4
GLA_DK = 64
GLA_DV = 128
GLA_GATE_RANK = 16
GLA_TAU = 16.0
GLA_CHUNK = 64
NA_HEADS = 8
NA_HEAD_DIM = 64
NA_KH = 8
NA_KW = 16
NA_SCALE = NA_HEAD_DIM ** -0.5
MLA_HEADS = 8
MLA_Q_RANK = 512
MLA_KV_RANK = 512
MLA_NOPE = 128
MLA_ROPE = 64
MLA_V = 128
MLA_SCALE = (MLA_NOPE + MLA_ROPE) ** -0.5
N_BRANCH = 3
N_EXPERTS = 32
TOP_K = 4
EXPERT_FF = D_MODEL
SWIGLU_LIMIT = 7.0
SWIGLU_ALPHA = 1.702

GLA_QK_W = GLA_HEADS * GLA_DK
GLA_V_W = GLA_HEADS * GLA_DV
NA_W = NA_HEADS * NA_HEAD_DIM
MLA_V_W = MLA_HEADS * MLA_V

LANE = 128
SUBLANE = 8
ROW_TILE = 256
VMEM_LIMIT = 56 * 1024 * 1024

COL_GATE = 0
COL_GQ = COL_GATE + N_BRANCH * D_MODEL
COL_GK = COL_GQ + GLA_QK_W
COL_GV = COL_GK + GLA_QK_W
COL_GR = COL_GV + GLA_V_W
COL_NQ = COL_GR + GLA_V_W
COL_NK = COL_NQ + NA_W
COL_NV = COL_NK + NA_W
COL_MCQ = COL_NV + NA_W
COL_MCKV = COL_MCQ + MLA_Q_RANK
COL_KR = COL_MCKV + MLA_KV_RANK
COL_KRP = COL_KR + LANE
COL_GA = COL_KRP + LANE
PROJ_TN = 1536
PROJ_W = 7 * PROJ_TN
assert COL_GA + LANE <= PROJ_W

MOE_G = 1024
MOE_SB = 256
MOE_FC = 256
DISPATCH_TT = 256
COMBINE_TT = 128
ROUTE_TT = 256
MLA_TQ = 512


def _cparams(sem):
    return pltpu.CompilerParams(dimension_semantics=sem, vmem_limit_bytes=VMEM_LIMIT)


def _nt(a, b):
    return lax.dot_general(a, b, (((1,), (1,)), ((), ())), preferred_element_type=F32)


def _tn(a, b):
    return lax.dot_general(a, b, (((0,), (0,)), ((), ())), preferred_element_type=F32)


def _mm(a, b):
    return jnp.dot(a, b, preferred_element_type=F32)


def _sigmoid(x):
    return 1.0 / (1.0 + jnp.exp(-x))


def _split_hi_lo(x):
    hi = x.astype(BF16)
    lo = (x - hi.astype(F32)).astype(BF16)
    return hi, lo


def _pack_bf16_pair(x):
    n = x.shape[1] // 2
    r = x.astype(BF16).astype(F32)
    lo = pltpu.bitcast(r[:, :n], jnp.uint32) >> 16
    hi = pltpu.bitcast(r[:, n:], jnp.uint32) & jnp.uint32(0xFFFF0000)
    return hi | lo


def _unpack_bf16_pair(w):
    lo = pltpu.bitcast(w << 16, F32)
    hi = pltpu.bitcast(w & jnp.uint32(0xFFFF0000), F32)
    return lo, hi


def _ada_kernel(c_ref, w_ref, b_ref, o_ref):
    c = c_ref[...]
    s = (c * _sigmoid(c)).astype(BF16)
    o_ref[0] = _mm(s, w_ref[0].astype(BF16)) + b_ref[0]


def _ada_mod(cc, ada_w, ada_b):
    nl, d, n6 = ada_w.shape
    tn = 1024
    return pl.pallas_call(
        _ada_kernel,
        grid=(nl, n6 // tn),
        in_specs=[pl.BlockSpec((16, d), lambda l, j: (0, 0)),
                  pl.BlockSpec((1, d, tn), lambda l, j: (l, 0, j)),
                  pl.BlockSpec((1, 1, tn), lambda l, j: (l, 0, j))],
        out_specs=pl.BlockSpec((1, 16, tn), lambda l, j: (l, 0, j)),
        out_shape=jax.ShapeDtypeStruct((nl, 16, n6), F32),
        compiler_params=_cparams(("arbitrary", "arbitrary")),
        name="ada_mod",
    )(cc, ada_w, ada_b.reshape(nl, 1, n6))


def _mod_spec(part, ctx_first):
    def imap(b, j):
        row = jnp.where(j == 0, 8, b) if ctx_first else b
        return (row * 6 + part, 0, 0)
    return pl.BlockSpec((1, 1, D_MODEL), imap)


def _norm_mod_kernel(x_ref, g_ref, sc_ref, sh_ref, o_ref):
    x = x_ref[0]
    ms = jnp.mean(x * x, axis=-1, keepdims=True)
    y = x * lax.rsqrt(ms + EPS) * g_ref[...]
    o_ref[0] = (y * (1.0 + sc_ref[0]) + sh_ref[0]).astype(o_ref.dtype)


def _norm_mod(x, g, mod):
    b, rb, d = x.shape
    return pl.pallas_call(
        _norm_mod_kernel,
        grid=(b, rb // ROW_TILE),
        in_specs=[pl.BlockSpec((1, ROW_TILE, d), lambda i, j: (i, j, 0)),
                  pl.BlockSpec((1, d), lambda i, j: (0, 0)),
                  _mod_spec(1, True), _mod_spec(0, True)],
        out_specs=pl.BlockSpec((1, ROW_TILE, d), lambda i, j: (i, j, 0)),
        out_shape=jax.ShapeDtypeStruct((b, rb, d), BF16),
        compiler_params=_cparams(("arbitrary", "arbitrary")),
        name="norm_mod",
    )(x, g.reshape(1, d), mod, mod)


def _matmul_kernel(x_ref, w_ref, o_ref):
    o_ref[...] = _mm(x_ref[...], w_ref[...]).astype(o_ref.dtype)


def _matmul(x, w, tm, tn, out_dtype):
    m, k = x.shape
    n = w.shape[1]
    return pl.pallas_call(
        _matmul_kernel,
        grid=(n // tn, m // tm),
        in_specs=[pl.BlockSpec((tm, k), lambda j, i: (i, 0)),
                  pl.BlockSpec((k, tn), lambda j, i: (0, j))],
        out_specs=pl.BlockSpec((tm, tn), lambda j, i: (i, j)),
        out_shape=jax.ShapeDtypeStruct((m, n), out_dtype),
        compiler_params=_cparams(("arbitrary", "arbitrary")),
        name="in_proj",
    )(x, w)


def _gla_kernel(q_ref, k_ref, v_ref, r_ref, ab_ref, waf_ref, wab_ref, ba_ref, g_ref, o_ref,
                laf_ref, lab_ref, of_ref, ob_ref, st_ref, *, n_ctx_chunks, n_chunks):
    c = GLA_CHUNK
    ab = ab_ref[0]
    zf = _mm(ab, waf_ref[...]) + ba_ref[0:1, :]
    zb = _mm(ab, wab_ref[...]) + ba_ref[1:2, :]
    laf_ref[...] = (jnp.minimum(zf, 0.0) - jnp.log(1.0 + jnp.exp(-jnp.abs(zf)))) * (1.0 / GLA_TAU)
    lab_ref[...] = (jnp.minimum(zb, 0.0) - jnp.log(1.0 + jnp.exp(-jnp.abs(zb)))) * (1.0 / GLA_TAU)
    st_ref[...] = jnp.zeros_like(st_ref)

    ri = lax.broadcasted_iota(jnp.int32, (c, c), 0)
    ci = lax.broadcasted_iota(jnp.int32, (c, c), 1)
    tri = [(ri >= ci), (ri <= ci)]
    tri_bf = [t.astype(F32).astype(BF16) for t in tri]
    tri4 = [jnp.concatenate([t] * GLA_HEADS, axis=0) for t in tri]
    lane_q = lax.broadcasted_iota(jnp.int32, (c, GLA_QK_W), 1) // GLA_DK
    lane_v = lax.broadcasted_iota(jnp.int32, (c, GLA_V_W), 1) // GLA_DV
    st_row_h = lax.broadcasted_iota(jnp.int32, (GLA_V_W, GLA_QK_W), 0) // GLA_DV
    st_col_h = lax.broadcasted_iota(jnp.int32, (GLA_V_W, GLA_QK_W), 1) // GLA_DK
    st_mask = st_row_h == st_col_h

    def one_dir(d, chunk, la_ref, out_ref):
        r0 = pl.multiple_of(chunk * c, c)
        la = la_ref[pl.ds(r0, c), :]
        hi, lo = _split_hi_lo(la)
        bc = _mm(tri_bf[d], hi) + _mm(tri_bf[d], lo)
        btot = bc[c - 1:c, :] if d == 0 else bc[0:1, :]
        q = q_ref[0, pl.ds(r0, c), :].astype(F32) * (GLA_DK ** -0.5)
        k = k_ref[0, pl.ds(r0, c), :].astype(F32)
        v = v_ref[0, pl.ds(r0, c), :]
        qd = (q * jnp.exp(bc)).astype(BF16)
        ki = (k * jnp.exp(-bc)).astype(BF16)
        kd = (k * jnp.exp(btot - bc)).astype(BF16)
        zero = jnp.zeros_like(qd)
        qs = jnp.concatenate([jnp.where(lane_q == h, qd, zero) for h in range(GLA_HEADS)], axis=0)
        att = _nt(qs, ki)
        att = jnp.where(tri4[d], att, 0.0).astype(BF16)
        rr = _mm(att, v)
        o = _nt(qd, st_ref[d].astype(BF16))
        for h in range(GLA_HEADS):
            o = o + jnp.where(lane_v == h, rr[h * c:(h + 1) * c, :], 0.0)
        out_ref[pl.ds(r0, c), :] = o
        upd = _tn(v, kd)
        st_ref[d] = jnp.where(st_mask, st_ref[d] * jnp.exp(btot) + upd, 0.0)

    def step(i, carry):
        one_dir(0, i, laf_ref, of_ref)
        cb = jnp.where(i < n_ctx_chunks, n_ctx_chunks - 1 - i, n_chunks + n_ctx_chunks - 1 - i)
        one_dir(1, cb, lab_ref, ob_ref)
        return carry

    lax.fori_loop(0, n_chunks, step, 0)

    def epilogue(j, carry):
        r0 = pl.multiple_of(j * ROW_TILE, ROW_TILE)
        o = of_ref[pl.ds(r0, ROW_TILE), :] + ob_ref[pl.ds(r0, ROW_TILE), :]
        r = r_ref[0, pl.ds(r0, ROW_TILE), :].astype(F32)
        gate = r * _sigmoid(r)
        for h in range(GLA_HEADS):
            sl = slice(h * GLA_DV, (h + 1) * GLA_DV)
            oh = o[:, sl]
            ms = jnp.mean(oh * oh, axis=-1, keepdims=True)
            y = oh * lax.rsqrt(ms + EPS) * g_ref[:, sl]
            o_ref[0, pl.ds(r0, ROW_TILE), sl] = (y * gate[:, sl]).astype(o_ref.dtype)
        return carry

    lax.fori_loop(0, (n_chunks * c) // ROW_TILE, epilogue, 0)


def _gla(p3, waf, wab, ba, g):
    b, rb, _ = p3.shape
    n_chunks = rb // GLA_CHUNK
    kern = functools.partial(_gla_kernel, n_ctx_chunks=CTX_LEN // GLA_CHUNK, n_chunks=n_chunks)

    def col(width, off):
        return pl.BlockSpec((1, rb, width), lambda i: (i, 0, off // width))

    return pl.pallas_call(
        kern,
        grid=(b,),
        in_specs=[col(GLA_QK_W, COL_GQ), col(GLA_QK_W, COL_GK), col(GLA_V_W, COL_GV), col(GLA_V_W, COL_GR),
                  col(LANE, COL_GA),
                  pl.BlockSpec((LANE, GLA_QK_W), lambda i: (0, 0)),
                  pl.BlockSpec((LANE, GLA_QK_W), lambda i: (0, 0)),
                  pl.BlockSpec((2, GLA_QK_W), lambda i: (0, 0)),
                  pl.BlockSpec((1, GLA_V_W), lambda i: (0, 0))],
        out_specs=pl.BlockSpec((1, rb, GLA_V_W), lambda i: (i, 0, 0)),
        out_shape=jax.ShapeDtypeStruct((b, rb, GLA_V_W), BF16),
        scratch_shapes=[pltpu.VMEM((rb, GLA_QK_W), F32), pltpu.VMEM((rb, GLA_QK_W), F32),
                        pltpu.VMEM((rb, GLA_V_W), F32), pltpu.VMEM((rb, GLA_V_W), F32),
                        pltpu.VMEM((2, GLA_V_W, GLA_QK_W), F32)],
        compiler_params=_cparams(("arbitrary",)),
        name="gla",
    )(p3, p3, p3, p3, p3, waf, wab, ba, g)


def _softmax_pv(parts):
    m = parts[0][0].max(axis=-1, keepdims=True)
    for s, _ in parts[1:]:
        m = jnp.maximum(m, s.max(axis=-1, keepdims=True))
    acc = None
    den = None
    for s, v in parts:
        e = jnp.exp(s - m)
        l = e.sum(axis=-1, keepdims=True)
        pv = _mm(e.astype(BF16), v)
        acc = pv if acc is None else acc + pv
        den = l if den is None else den + l
    return acc / den


def _na_kernel(q_ref, k_ref, v_ref, bias_ref, o_ref, *, n_rows):
    kh = NA_KH
    lane = lax.broadcasted_iota(jnp.int32, (GRID_W, LANE), 1)
    head_mask = [lane < NA_HEAD_DIM, lane >= NA_HEAD_DIM]
    kc = k_ref[0, 0:CTX_LEN, :]
    vc = v_ref[0, 0:CTX_LEN, :]

    def row(r, carry):
        rs = jnp.clip(r - kh // 2, 0, n_rows - kh)
        off = rs - r + (NA_KH - 1)
        q0 = pl.multiple_of(CTX_LEN + r * GRID_W, GRID_W)
        k0 = pl.multiple_of(CTX_LEN + rs * GRID_W, GRID_W)
        q = q_ref[0, pl.ds(q0, GRID_W), :]
        kl = k_ref[0, pl.ds(k0, kh * GRID_W), :]
        vl = v_ref[0, pl.ds(k0, kh * GRID_W), :]
        outs = []
        for hh in range(2):
            qm = jnp.where(head_mask[hh], q, jnp.zeros_like(q))
            s_loc = _nt(qm, kl) * NA_SCALE + bias_ref[hh, off]
            s_ctx = _nt(qm, kc) * NA_SCALE
            outs.append(_softmax_pv([(s_loc, vl), (s_ctx, vc)]))
        o_ref[0, pl.ds(q0, GRID_W), :] = jnp.where(head_mask[0], outs[0], outs[1]).astype(o_ref.dtype)
        return carry

    lax.fori_loop(0, n_rows, row, 0, unroll=4)

    qc = q_ref[0, 0:CTX_LEN, :]
    lane_c = lax.broadcasted_iota(jnp.int32, (CTX_LEN, LANE), 1)
    outs = []
    for hh in range(2):
        msk = (lane_c < NA_HEAD_DIM) if hh == 0 else (lane_c >= NA_HEAD_DIM)
        qm = jnp.where(msk, qc, jnp.zeros_like(qc))
        outs.append(_softmax_pv([(_nt(qm, kc) * NA_SCALE, vc)]))
    o_ref[0, 0:CTX_LEN, :] = jnp.where(lane_c < NA_HEAD_DIM, outs[0], outs[1]).astype(o_ref.dtype)


def _na_bias_table(rpb):
    o = np.arange(NA_KH)[:, None] + np.arange(NA_KH)[None, :]
    qcol = np.arange(GRID_W)[:, None]
    kcol = np.arange(GRID_W)[None, :]
    wstart = np.clip(qcol - NA_KW // 2, 0, GRID_W - NA_KW)
    in_win = (kcol >= wstart) & (kcol < wstart + NA_KW)
    dc = np.clip(kcol - qcol + NA_KW - 1, 0, 2 * NA_KW - 2)
    del o
    t = jnp.stack([rpb[:, s:s + NA_KH, :] for s in range(NA_KH)], axis=1)
    onehot = (dc[None] == np.arange(2 * NA_KW - 1)[:, None, None]).astype(np.float32)
    t = jnp.einsum('hoic,cqk->hoqik', t.astype(F32), jnp.asarray(onehot), precision=lax.Precision.HIGHEST)
    t = jnp.where(in_win[None, None, :, None, :], t, NEG_INF)
    return t.reshape(NA_HEADS, NA_KH, GRID_W, NA_KH * GRID_W)


def _na(p3, bias):
    b, rb, _ = p3.shape
    n_rows = (rb - CTX_LEN) // GRID_W
    kern = functools.partial(_na_kernel, n_rows=n_rows)

    def col(off):
        return pl.BlockSpec((1, rb, LANE), lambda p, i: (i, 0, off // LANE + p))

    return pl.pallas_call(
        kern,
        grid=(NA_HEADS // 2, b),
        in_specs=[col(COL_NQ), col(COL_NK), col(COL_NV),
                  pl.BlockSpec((2, NA_KH, GRID_W, NA_KH * GRID_W), lambda p, i: (p, 0, 0, 0))],
        out_specs=pl.BlockSpec((1, rb, LANE), lambda p, i: (i, 0, p)),
        out_shape=jax.ShapeDtypeStruct((b, rb, NA_W), BF16),
        compiler_params=_cparams(("arbitrary", "arbitrary")),
        name="na",
    )(p3, p3, p3, bias)


def _mla_proj_kernel(cq_ref, ckv_ref, kr_ref, krp_ref, cos_ref, sin_ref, gq_ref, gkv_ref,
                     wq_ref, wq2_ref, wkv_ref, q_ref, k_ref, v_ref):
    def norm(x, g):
        x = x.astype(F32)
        ms = jnp.mean(x * x, axis=-1, keepdims=True)
        return (x * lax.rsqrt(ms + EPS) * g).astype(BF16)

    cos = cos_ref[...]
    sin = sin_ref[...]
    nq = norm(cq_ref[0], gq_ref[...])
    yq = _mm(nq, wq_ref[...])
    yq2 = _mm(nq, wq2_ref[...])
    nkv = norm(ckv_ref[0], gkv_ref[...])
    ykv = _mm(nkv, wkv_ref[...])
    k_rot = (kr_ref[0].astype(F32) * cos + krp_ref[0].astype(F32) * sin).astype(BF16)
    for h in range(MLA_HEADS):
        a = 2 * h * LANE
        q_ref[0, :, a:a + LANE] = (yq[:, a:a + LANE] * MLA_SCALE).astype(BF16)
        rot = yq[:, a + LANE:a + 2 * LANE] * cos + yq2[:, h * LANE:(h + 1) * LANE] * sin
        q_ref[0, :, a + LANE:a + 2 * LANE] = (rot * MLA_SCALE).astype(BF16)
        k_ref[0, :, a:a + LANE] = ykv[:, h * LANE:(h + 1) * LANE].astype(BF16)
        k_ref[0, :, a + LANE:a + 2 * LANE] = k_rot
    v_ref[0] = ykv[:, MLA_HEADS * MLA_NOPE:].astype(BF16)


def _mla_proj(p3, cos_t, sin_t, gq, gkv, wq, wq2, wkv):
    b, rb, _ = p3.shape
    tm = ROW_TILE

    def col(width, off):
        return pl.BlockSpec((1, tm, width), lambda i, j: (i, j, off // width))

    def full(a):
        return pl.BlockSpec(a.shape, lambda i, j: (0, 0))

    hw = MLA_HEADS * 2 * LANE
    nj = rb // tm
    return pl.pallas_call(
        _mla_proj_kernel,
        grid=(b, nj),
        in_specs=[col(MLA_Q_RANK, COL_MCQ), col(MLA_KV_RANK, COL_MCKV), col(LANE, COL_KR), col(LANE, COL_KRP),
                  pl.BlockSpec((tm, LANE), lambda i, j: (j, 0)), pl.BlockSpec((tm, LANE), lambda i, j: (j, 0)),
                  full(gq), full(gkv), full(wq), full(wq2), full(wkv)],
        out_specs=[pl.BlockSpec((1, tm, hw), lambda i, j: (i, (j + nj - 1) % nj, 0)),
                   pl.BlockSpec((1, tm, hw), lambda i, j: (i, (j + nj - 1) % nj, 0)),
                   pl.BlockSpec((1, tm, MLA_V_W), lambda i, j: (i, (j + nj - 1) % nj, 0))],
        out_shape=[jax.ShapeDtypeStruct((b, rb, hw), BF16), jax.ShapeDtypeStruct((b, rb, hw), BF16),
                   jax.ShapeDtypeStruct((b, rb, MLA_V_W), BF16)],
        compiler_params=_cparams(("arbitrary", "arbitrary")),
        name="mla_proj",
    )(p3, p3, p3, p3, cos_t, sin_t, gq, gkv, wq, wq2, wkv)


def _mla_attn_kernel(q_ref, k_ref, v_ref, o_ref):
    k = k_ref[0]
    v = v_ref[0]
    for r0 in range(0, q_ref.shape[1], ROW_TILE):
        q = q_ref[0, r0:r0 + ROW_TILE, :]
        o_ref[0, r0:r0 + ROW_TILE, :] = _softmax_pv([(_nt(q, k), v)]).astype(o_ref.dtype)


def _mla_attn(q, k, v, ctx_queries):
    b, rb, _ = q.shape
    n_lat = rb - CTX_LEN
    if ctx_queries:
        tq, nq, n_keys = CTX_LEN, 1, CTX_LEN
        q_off = kv_blk = n_lat // CTX_LEN
    else:
        tq = MLA_TQ if n_lat % MLA_TQ == 0 else ROW_TILE
        nq, n_keys, q_off, kv_blk = n_lat // tq, rb, 0, 0
    return pl.pallas_call(
        _mla_attn_kernel,
        grid=(b, MLA_HEADS, nq),
        in_specs=[pl.BlockSpec((1, tq, 2 * LANE), lambda i, h, j: (i, j + q_off, h)),
                  pl.BlockSpec((1, n_keys, 2 * LANE), lambda i, h, j: (i, kv_blk, h)),
                  pl.BlockSpec((1, n_keys, MLA_V), lambda i, h, j: (i, kv_blk, h))],
        out_specs=pl.BlockSpec((1, tq, MLA_V), lambda i, h, j: (i, j, h)),
        out_shape=jax.ShapeDtypeStruct((b, nq * tq, MLA_V_W), BF16),
        compiler_params=_cparams(("arbitrary", "arbitrary", "arbitrary")),
        name="mla_attn_ctx" if ctx_queries else "mla_attn",
    )(q, k, v)


def _merge_kernel(x_ref, oa_ref, ob_ref, oc_ref, occ_ref, ga_ref, gb_ref, gc_ref, bg_ref,
                  wa_ref, wb_ref, wc_ref, wo_ref, gt1_ref, sc2_ref, sh2_ref, g2_ref, rw_ref, rb_ref,
                  xo_ref, h_ref, lg_ref, *, ctx_tile):
    d = D_MODEL

    def gate(g_ref, k):
        return _sigmoid(g_ref[0].astype(F32) + bg_ref[:, k * d:(k + 1) * d])

    oc = oc_ref[0]
    if ctx_tile:
        oc = jnp.where(pl.program_id(1) == 0, occ_ref[0], oc)
    m = gate(ga_ref, 0) * _mm(oa_ref[0], wa_ref[...])
    m = m + gate(gb_ref, 1) * _mm(ob_ref[0], wb_ref[...])
    m = m + gate(gc_ref, 2) * _mm(oc, wc_ref[...])
    y = _mm(m.astype(BF16), wo_ref[...])
    x = x_ref[0] + gt1_ref[0] * y
    xo_ref[0] = x
    ms = jnp.mean(x * x, axis=-1, keepdims=True)
    h = x * lax.rsqrt(ms + EPS) * g2_ref[...]
    h = h * (1.0 + sc2_ref[0]) + sh2_ref[0]
    h_ref[...] = _pack_bf16_pair(h)
    hh, hl = _split_hi_lo(h)
    wh, wl = _split_hi_lo(rw_ref[...])
    lg_ref[...] = _nt(wh, hh) + _nt(wh, hl) + _nt(wl, hh) + rb_ref[:, 0:1]


def _merge(x, p3, og, on, om, om_ctx, bg, wa, wb, wc, wo, mod, g2, rwt, rbias, skip_ctx):
    b, rb, d = x.shape
    tm = ROW_TILE
    jo = 1 if skip_ctx else 0
    nj = rb // tm - jo
    rows_out = nj * tm

    def rows(width, cblk=0):
        return pl.BlockSpec((1, tm, width), lambda i, j: (i, j + jo, cblk))

    om_spec = pl.BlockSpec((1, tm, MLA_V_W), lambda i, j: (i, jnp.maximum(j + jo - 1, 0), 0))
    omc_spec = pl.BlockSpec((1, tm, MLA_V_W), lambda i, j: (i, 0, 0))
    kern = functools.partial(_merge_kernel, ctx_tile=not skip_ctx)

    def full(a):
        return pl.BlockSpec(a.shape, lambda i, j: (0,) * a.ndim, pipeline_mode=pl.Buffered(1))

    def mod_spec(part):
        def imap(i, j):
            row = i if skip_ctx else jnp.where(j == 0, 8, i)
            return (row * 6 + part, 0, 0)
        return pl.BlockSpec((1, 1, d), imap)

    return pl.pallas_call(
        kern,
        grid=(b, nj),
        in_specs=[rows(d), rows(GLA_V_W), rows(NA_W), om_spec, omc_spec,
                  rows(d, 0), rows(d, 1), rows(d, 2), full(bg),
                  full(wa), full(wb), full(wc), full(wo),
                  mod_spec(2), mod_spec(4), mod_spec(3), full(g2), full(rwt), full(rbias)],
        out_specs=[pl.BlockSpec((1, tm, d), lambda i, j: (i, j, 0)),
                   pl.BlockSpec((tm, d // 2), lambda i, j: (i * nj + j, 0)),
                   pl.BlockSpec((N_EXPERTS, tm), lambda i, j: (0, i * nj + j))],
        out_shape=[jax.ShapeDtypeStruct((b, rows_out, d), F32),
                   jax.ShapeDtypeStruct((b * rows_out, d // 2), jnp.uint32),
                   jax.ShapeDtypeStruct((N_EXPERTS, b * rows_out), F32)],
        compiler_params=_cparams(("arbitrary", "arbitrary")),
        name="merge",
    )(x, og, on, om, om if skip_ctx else om_ctx, p3, p3, p3, bg, wa, wb, wc, wo, mod, mod, mod, g2, rwt, rbias)


def _route_kernel(l_ref, idx_ref, w_ref, rank_ref, cnt_ref, carry_ref):
    i = pl.program_id(0)
    tt = l_ref.shape[1]

    @pl.when(i == 0)
    def _():
        carry_ref[...] = jnp.zeros_like(carry_ref)

    l = l_ref[...]
    eio = lax.broadcasted_iota(jnp.int32, (N_EXPERTS, tt), 0)
    vals, idxs = [], []
    for _ in range(TOP_K):
        m = l.max(axis=0, keepdims=True)
        ik = jnp.min(jnp.where(l == m, eio, N_EXPERTS), axis=0, keepdims=True)
        vals.append(m)
        idxs.append(ik)
        l = jnp.where(eio == ik, -jnp.inf, l)
    es = [jnp.exp(v - vals[0]) for v in vals]
    den = es[0] + es[1] + es[2] + es[3]
    sel = jnp.zeros((N_EXPERTS, tt), F32)
    for ik in idxs:
        sel = sel + (eio == ik).astype(F32)
    si = lax.broadcasted_iota(jnp.int32, (tt, tt), 0)
    ti = lax.broadcasted_iota(jnp.int32, (tt, tt), 1)
    before = (si < ti).astype(F32).astype(BF16)
    rank_full = _mm(sel.astype(BF16), before) + carry_ref[:, 0:1]
    for k in range(TOP_K):
        idx_ref[k:k + 1, :] = idxs[k]
        w_ref[k:k + 1, :] = es[k] / den
        rk = jnp.sum(jnp.where(eio == idxs[k], rank_full, 0.0), axis=0, keepdims=True)
        rank_ref[k:k + 1, :] = rk.astype(jnp.int32)
    carry_ref[...] = carry_ref[...] + jnp.sum(sel, axis=1, keepdims=True)
    cnt_ref[...] = carry_ref[...]


def _route(logits_t):
    ne, t = logits_t.shape
    tt = ROUTE_TT
    spec4 = pl.BlockSpec((TOP_K, tt), lambda i: (0, i))
    return pl.pallas_call(
        _route_kernel,
        grid=(t // tt,),
        in_specs=[pl.BlockSpec((ne, tt), lambda i: (0, i))],
        out_specs=[spec4, spec4, spec4, pl.BlockSpec((ne, LANE), lambda i: (0, 0))],
        out_shape=[jax.ShapeDtypeStruct((TOP_K, t), jnp.int32), jax.ShapeDtypeStruct((TOP_K, t), F32),
                   jax.ShapeDtypeStruct((TOP_K, t), jnp.int32), jax.ShapeDtypeStruct((ne, LANE), F32)],
        scratch_shapes=[pltpu.VMEM((ne, LANE), F32)],
        compiler_params=_cparams(("arbitrary",)),
        name="route",
    )(logits_t)


def _pad_fill(ps_ref, pn_ref, zero_ref, xg_ref, sem, wait):
    def copy(pos, rows):
        cp = pltpu.make_async_copy(zero_ref.at[pl.ds(0, rows), :], xg_ref.at[pl.ds(pos, rows), :], sem)
        cp.wait() if wait else cp.start()

    def per_expert(e, carry):
        pos = ps_ref[e]
        head = (-pos) & (SUBLANE - 1)
        for r in range(SUBLANE - 1):
            @pl.when(r < head)
            def _(r=r):
                copy(pos + r, 1)

        pos = pos + head
        n = pn_ref[e] - head
        bit = MOE_G // 2
        while bit >= SUBLANE:
            on = (n & bit) != 0

            @pl.when(on)
            def _(pos=pos, bit=bit):
                copy(pl.multiple_of(pos, SUBLANE), bit)

            pos = pos + jnp.where(on, bit, 0)
            bit //= 2
        return carry

    lax.fori_loop(0, N_EXPERTS, per_expert, 0)

    zr = zero_ref.shape[0]

    def tail(i, carry):
        pos = pl.multiple_of(ps_ref[N_EXPERTS] + i * zr, zr)
        cp = pltpu.make_async_copy(zero_ref, xg_ref.at[pl.ds(pos, zr), :], sem)
        cp.wait() if wait else cp.start()
        return carry

    lax.fori_loop(0, pn_ref[N_EXPERTS], tail, 0)


def _dispatch_kernel(ps_ref, pn_ref, dest_ref, h_ref, xg_ref, zero_ref, sem, zsem):
    tt = h_ref.shape[0]

    @pl.when(pl.program_id(0) == 0)
    def _():
        zero_ref[...] = jnp.zeros_like(zero_ref)
        _pad_fill(ps_ref, pn_ref, zero_ref, xg_ref, zsem, wait=False)

    def issue(t, carry):
        for k in range(TOP_K):
            pltpu.make_async_copy(h_ref.at[pl.ds(t, 1), :], xg_ref.at[pl.ds(dest_ref[0, k, t], 1), :], sem).start()
        return carry

    lax.fori_loop(0, tt, issue, 0)
    for k in range(TOP_K):
        pltpu.make_async_copy(h_ref, xg_ref.at[pl.ds(0, tt), :], sem).wait()

    @pl.when(pl.program_id(0) == 0)
    def _():
        _pad_fill(ps_ref, pn_ref, zero_ref, xg_ref, zsem, wait=True)


def _dispatch(hp, dest, pad_start, pad_len, n_slots):
    t, w = hp.shape
    tt = DISPATCH_TT
    dest3 = dest.reshape(TOP_K, t // tt, tt).transpose(1, 0, 2)
    grid_spec = pltpu.PrefetchScalarGridSpec(
        num_scalar_prefetch=2,
        grid=(t // tt,),
        in_specs=[pl.BlockSpec((1, TOP_K, tt), lambda i, ps, pn: (i, 0, 0), memory_space=pltpu.SMEM),
                  pl.BlockSpec((tt, w), lambda i, ps, pn: (i, 0))],
        out_specs=pl.BlockSpec(memory_space=pl.ANY),
        scratch_shapes=[pltpu.VMEM((MOE_G // 2, w), jnp.uint32), pltpu.SemaphoreType.DMA(()),
                        pltpu.SemaphoreType.DMA(())],
    )
    return pl.pallas_call(
        _dispatch_kernel,
        grid_spec=grid_spec,
        out_shape=jax.ShapeDtypeStruct((n_slots, w), jnp.uint32),
        compiler_params=_cparams(("arbitrary",)),
        name="dispatch",
    )(pad_start, pad_len, dest3, hp)


def _ffn_kernel(te_ref, tv_ref, x_ref, w1g_ref, w1l_ref, b1g_ref, b1l_ref, w2_ref, b2_ref, y_ref,
                xb_ref, acc_ref, *, n_fc):
    i = pl.program_id(0)
    j = pl.program_id(1)
    valid = tv_ref[i]
    g, half = x_ref.shape

    @pl.when((valid > 0) & (j == 0))
    def _():
        lo, hi = _unpack_bf16_pair(x_ref[...])
        ok = lax.broadcasted_iota(jnp.int32, (g, half), 0) < valid
        xb_ref[:, 0:half] = jnp.where(ok, lo, 0.0).astype(BF16)
        xb_ref[:, half:2 * half] = jnp.where(ok, hi, 0.0).astype(BF16)
        acc_ref[...] = jnp.broadcast_to(b2_ref[0, 0], acc_ref.shape)

    for nr in range(MOE_SB, g + 1, MOE_SB):
        @pl.when((valid > nr - MOE_SB) & (valid <= nr))
        def _(nr=nr):
            wg = w1g_ref[0, 0].astype(BF16)
            wl = w1l_ref[0, 0].astype(BF16)
            w2 = w2_ref[0, 0].astype(BF16)
            for r0 in range(0, nr, MOE_SB):
                x = xb_ref[r0:r0 + MOE_SB, :]
                ug = _mm(x, wg) + b1g_ref[0, 0]
                ul = _mm(x, wl) + b1l_ref[0, 0]
                xg = jnp.minimum(ug, SWIGLU_LIMIT)
                xl = jnp.clip(ul, -SWIGLU_LIMIT, SWIGLU_LIMIT)
                act = xg * _sigmoid(SWIGLU_ALPHA * xg) * (xl + 1.0)
                acc_ref[r0:r0 + MOE_SB, :] = acc_ref[r0:r0 + MOE_SB, :] + _mm(act.astype(BF16), w2)

    @pl.when((j == n_fc - 1) & (valid > 0))
    def _():
        y_ref[...] = _pack_bf16_pair(acc_ref[...])

    @pl.when((j == n_fc - 1) & (valid == 0))
    def _():
        y_ref[...] = jnp.zeros_like(y_ref)


def _ffn(layer, tile_expert, tile_valid, xg, w1, b1, w2, b2):
    n_slots, half = xg.shape
    nl, ne, d, ff2 = w1.shape
    ff = ff2 // 2
    n_fc = ff // MOE_FC
    n_tiles = n_slots // MOE_G
    kern = functools.partial(_ffn_kernel, n_fc=n_fc)

    def jj(j, tv, i):
        return jnp.where(tv[i] > 0, j, n_fc - 1)

    grid_spec = pltpu.PrefetchScalarGridSpec(
        num_scalar_prefetch=2,
        grid=(n_tiles, n_fc),
        in_specs=[pl.BlockSpec((MOE_G, half), lambda i, j, te, tv: (i, 0)),
                  pl.BlockSpec((1, 1, d, MOE_FC), lambda i, j, te, tv: (layer, te[i], 0, jj(j, tv, i))),
                  pl.BlockSpec((1, 1, d, MOE_FC), lambda i, j, te, tv: (layer, te[i], 0, jj(j, tv, i) + n_fc)),
                  pl.BlockSpec((1, 1, 1, MOE_FC), lambda i, j, te, tv: (layer, te[i], 0, jj(j, tv, i))),
                  pl.BlockSpec((1, 1, 1, MOE_FC), lambda i, j, te, tv: (layer, te[i], 0, jj(j, tv, i) + n_fc)),
                  pl.BlockSpec((1, 1, MOE_FC, d), lambda i, j, te, tv: (layer, te[i], jj(j, tv, i), 0)),
                  pl.BlockSpec((1, 1, 1, d), lambda i, j, te, tv: (layer, te[i], 0, 0))],
        out_specs=pl.BlockSpec((MOE_G, half), lambda i, j, te, tv: (i, 0)),
        scratch_shapes=[pltpu.VMEM((MOE_G, d), BF16), pltpu.VMEM((MOE_G, d), F32)],
    )
    return pl.pallas_call(
        kern,
        grid_spec=grid_spec,
        out_shape=jax.ShapeDtypeStruct((n_slots, half), jnp.uint32),
        compiler_params=_cparams(("arbitrary", "arbitrary")),
        name="moe_ffn",
    )(tile_expert, tile_valid, xg, w1, w1, b1.reshape(nl, ne, 1, ff2), b1.reshape(nl, ne, 1, ff2), w2,
      b2.reshape(nl, ne, 1, d))


def _combine_kernel(dest_ref, destn_ref, x_ref, wt_ref, gt2_ref, gf_ref, yg_ref, o_ref, buf_ref, sem,
                    *, final_norm):
    i = pl.program_id(0)
    n = pl.num_programs(0)
    tt = x_ref.shape[1]
    slot = i % 2

    def issue(d_ref, s):
        def body(t, carry):
            for k in range(TOP_K):
                pltpu.make_async_copy(yg_ref.at[pl.ds(d_ref[0, k, t], 1), :],
                                      buf_ref.at[s, k, pl.ds(t, 1), :], sem.at[s]).start()
            return carry
        lax.fori_loop(0, tt, body, 0)

    @pl.when(i == 0)
    def _():
        issue(dest_ref, 0)

    @pl.when(i + 1 < n)
    def _():
        issue(destn_ref, 1 - slot)

    for k in range(TOP_K):
        pltpu.make_async_copy(yg_ref.at[pl.ds(0, tt), :], buf_ref.at[slot, k], sem.at[slot]).wait()

    wt = wt_ref[...]
    acc_lo = None
    acc_hi = None
    for k in range(TOP_K):
        lo, hi = _unpack_bf16_pair(buf_ref[slot, k])
        wk = wt[:, k:k + 1]
        acc_lo = lo * wk if acc_lo is None else acc_lo + lo * wk
        acc_hi = hi * wk if acc_hi is None else acc_hi + hi * wk
    y = jnp.concatenate([acc_lo, acc_hi], axis=1)
    x = x_ref[0] + gt2_ref[0] * y
    if final_norm:
        ms = jnp.mean(x * x, axis=-1, keepdims=True)
        x = x * lax.rsqrt(ms + EPS) * gf_ref[...]
    o_ref[0] = x


def _combine(x, yg, dest, wts, mod, gf, ctx_first, final_norm):
    b, rows, d = x.shape
    tt = COMBINE_TT
    nj = rows // tt
    t = b * rows
    nt = t // tt
    dest3 = dest.reshape(TOP_K, nt, tt).transpose(1, 0, 2)
    wt = wts.T
    kern = functools.partial(_combine_kernel, final_norm=final_norm)

    def mod_imap(i):
        bi = i // nj
        row = jnp.where((i % nj) * tt < CTX_LEN, 8, bi) if ctx_first else bi
        return (row * 6 + 5, 0, 0)

    return pl.pallas_call(
        kern,
        grid=(nt,),
        in_specs=[pl.BlockSpec((1, TOP_K, tt), lambda i: (i, 0, 0), memory_space=pltpu.SMEM),
                  pl.BlockSpec((1, TOP_K, tt), lambda i: (jnp.minimum(i + 1, nt - 1), 0, 0),
                               memory_space=pltpu.SMEM),
                  pl.BlockSpec((1, tt, d), lambda i: (i // nj, i % nj, 0)),
                  pl.BlockSpec((tt, TOP_K), lambda i: (i, 0)),
                  pl.BlockSpec((1, 1, d), mod_imap),
                  pl.BlockSpec((1, d), lambda i: (0, 0)),
                  pl.BlockSpec(memory_space=pl.ANY)],
        out_specs=pl.BlockSpec((1, tt, d), lambda i: (i // nj, i % nj, 0)),
        out_shape=jax.ShapeDtypeStruct((b, rows, d), F32),
        scratch_shapes=[pltpu.VMEM((2, TOP_K, tt, d // 2), jnp.uint32), pltpu.SemaphoreType.DMA((2,))],
        compiler_params=_cparams(("arbitrary",)),
        name="combine",
    )(dest3, dest3, x, wt, mod, gf.reshape(1, d), yg)


def _proj_weight(w_in):
    d = w_in.shape[0]
    splits = (GLA_QK_W, GLA_QK_W, GLA_V_W, GLA_V_W, GLA_GATE_RANK, GLA_GATE_RANK,
              NA_W, NA_W, NA_W, MLA_Q_RANK, MLA_KV_RANK, MLA_ROPE, N_BRANCH * D_MODEL)
    pts = np.cumsum((0,) + splits)
    (gq, gk, gv, gr, gaf, gab, nq, nk, nv, mcq, mckv, mkr, gate) = [w_in[:, pts[i]:pts[i + 1]] for i in range(13)]
    q16 = MLA_ROPE // 4
    mkrp = jnp.concatenate([mkr[:, q16:2 * q16], mkr[:, :q16], mkr[:, 3 * q16:], mkr[:, 2 * q16:3 * q16]], axis=1)
    z = lambda n: jnp.zeros((d, n), w_in.dtype)
    cols = [gate, gq, gk, gv, gr, nq, nk, nv, mcq, mckv,
            mkr, z(LANE - MLA_ROPE), mkrp, z(LANE - MLA_ROPE),
            gaf, gab, z(LANE - 2 * GLA_GATE_RANK)]
    w = jnp.concatenate(cols, axis=1)
    w = jnp.concatenate([w, z(PROJ_W - w.shape[1])], axis=1)
    return w.astype(BF16)


def _rope_tables(rb):
    n = rb - CTX_LEN
    t = np.arange(n)
    nf = MLA_ROPE // 4
    freqs = ROPE_BASE ** (-np.arange(nf, dtype=np.float64) / nf)
    cos = np.zeros((rb, LANE), np.float32)
    sin = np.zeros((rb, LANE), np.float32)
    cos[:CTX_LEN, :MLA_ROPE] = 1.0
    for a, pos in enumerate((t // GRID_W, t % GRID_W)):
        ang = (pos.astype(np.float32)[:, None] * freqs.astype(np.float32)[None, :]).astype(np.float32)
        c, s = np.cos(ang), np.sin(ang)
        base = a * 2 * nf
        cos[CTX_LEN:, base:base + nf] = c
        cos[CTX_LEN:, base + nf:base + 2 * nf] = c
        sin[CTX_LEN:, base:base + nf] = -s
        sin[CTX_LEN:, base + nf:base + 2 * nf] = s
    return jnp.asarray(cos), jnp.asarray(sin)


def _mla_weights(w_q_up, w_kv_up):
    r = w_q_up.shape[0]
    wq = w_q_up.reshape(r, MLA_HEADS, MLA_NOPE + MLA_ROPE)
    nope, rope = wq[..., :MLA_NOPE], wq[..., MLA_NOPE:]
    q16 = MLA_ROPE // 4
    ropep = jnp.concatenate([rope[..., q16:2 * q16], rope[..., :q16], rope[..., 3 * q16:], rope[..., 2 * q16:3 * q16]],
                            axis=-1)
    zpad = jnp.zeros((r, MLA_HEADS, LANE - MLA_ROPE), w_q_up.dtype)
    wq1 = jnp.concatenate([nope, rope, zpad], axis=-1).reshape(r, MLA_HEADS * 2 * LANE).astype(BF16)
    wq2 = jnp.concatenate([ropep, zpad], axis=-1).reshape(r, MLA_HEADS * LANE).astype(BF16)
    rk = w_kv_up.shape[0]
    wkv = w_kv_up.reshape(rk, MLA_HEADS, 2, MLA_NOPE).transpose(0, 2, 1, 3).reshape(rk, 2 * MLA_HEADS * MLA_NOPE)
    return wq1, wq2, wkv.astype(BF16)


def _moe_plan(idx, rank, counts, n_tiles):
    cnt = counts[:, 0].astype(jnp.int32)
    padded = ((cnt + MOE_G - 1) // MOE_G) * MOE_G
    ends = jnp.cumsum(padded)
    starts = ends - padded
    e_ids = jnp.arange(N_EXPERTS, dtype=jnp.int32)
    dest = jnp.sum(jnp.where(idx[..., None] == e_ids, starts, 0), axis=-1) + rank
    tile_start = jnp.arange(n_tiles, dtype=jnp.int32) * MOE_G
    te = jnp.sum((tile_start[:, None] >= ends[None, :]).astype(jnp.int32), axis=1)
    active = te < N_EXPERTS
    te_c = jnp.minimum(te, N_EXPERTS - 1)
    tile_is = te_c[:, None] == e_ids[None, :]
    cnt_t = jnp.sum(jnp.where(tile_is, cnt, 0), axis=1)
    start_t = jnp.sum(jnp.where(tile_is, starts, 0), axis=1)
    valid = jnp.clip(cnt_t - (tile_start - start_t), 0, MOE_G)
    valid = jnp.where(active, valid, 0)
    last_e = jnp.max(jnp.where(cnt > 0, jnp.arange(N_EXPERTS, dtype=jnp.int32), 0))
    te_f = jnp.where(active, te_c, last_e)
    tail_blocks = (n_tiles * MOE_G - ends[-1]) // (MOE_G // 2)
    pad_start = jnp.concatenate([starts + cnt, ends[-1:]]).astype(jnp.int32)
    pad_len = jnp.concatenate([padded - cnt, tail_blocks[None]]).astype(jnp.int32)
    return dest, te_f, valid, pad_start, pad_len


def kernel(x, c, ctx, c_ctx, norm1_g, norm2_g, ada_w, ada_b, w_in, b_gate, gla_wa, gla_ba, gla_norm_g,
           na_rpb, mla_q_norm_g, mla_w_q_up, mla_kv_norm_g, mla_w_kv_up, w_branch_gla, w_branch_na,
           w_branch_mla, w_out, router_w, router_b, moe_w1, moe_b1, moe_w2, moe_b2, final_norm_g):
    b, n, d = x.shape
    rb = CTX_LEN + n
    assert b <= 8 and d == D_MODEL and ctx.shape[1] == CTX_LEN

    cc = jnp.zeros((16, d), F32).at[:b].set(c).at[8].set(c_ctx)
    mod_all = _ada_mod(cc, ada_w, ada_b)
    cos_t, sin_t = _rope_tables(rb)
    xs = jnp.concatenate([ctx, x], axis=1)

    for l in range(DEPTH):
        last = l == DEPTH - 1
        mod = mod_all[l].reshape(16 * 6, 1, d)
        h = _norm_mod(xs, norm1_g[l], mod)
        p = _matmul(h.reshape(b * rb, d), _proj_weight(w_in[l]), 1024 if (b * rb) % 1024 == 0 else ROW_TILE,
                    PROJ_TN, BF16)
        p3 = p.reshape(b, rb, PROJ_W)

        zpad = jnp.zeros((LANE - 2 * GLA_GATE_RANK, GLA_QK_W), F32)
        zr = jnp.zeros((GLA_GATE_RANK, GLA_QK_W), F32)
        waf = jnp.concatenate([gla_wa[l, 0], zr, zpad], axis=0).astype(BF16)
        wab = jnp.concatenate([zr, gla_wa[l, 1], zpad], axis=0).astype(BF16)
        og = _gla(p3, waf, wab, gla_ba[l], gla_norm_g[l].reshape(1, GLA_V_W))

        on = _na(p3, _na_bias_table(na_rpb[l]))

        wq1, wq2, wkv = _mla_weights(mla_w_q_up[l], mla_w_kv_up[l])
        q_m, k_m, v_m = _mla_proj(p3, cos_t, sin_t, mla_q_norm_g[l].reshape(1, -1), mla_kv_norm_g[l].reshape(1, -1),
                                  wq1, wq2, wkv)
        om = _mla_attn(q_m, k_m, v_m, ctx_queries=False)
        om_ctx = None if last else _mla_attn(q_m, k_m, v_m, ctx_queries=True)

        xs, hp, logits_t = _merge(
            xs, p3, og, on, om, om_ctx, b_gate[l].reshape(1, -1),
            w_branch_gla[l].astype(BF16), w_branch_na[l].astype(BF16), w_branch_mla[l].astype(BF16),
            w_out[l].astype(BF16), mod, norm2_g[l].reshape(1, d), router_w[l].T,
            jnp.broadcast_to(router_b[l][:, None], (N_EXPERTS, LANE)), skip_ctx=last)

        t_tok = hp.shape[0]
        idx, wts, rank, counts = _route(logits_t)
        n_tiles = (TOP_K * t_tok) // MOE_G + N_EXPERTS
        dest, te, tv, pad_start, pad_len = _moe_plan(idx, rank, counts, n_tiles)
        xg = _dispatch(hp, dest, pad_start, pad_len, n_tiles * MOE_G)
        yg = _ffn(l, te, tv, xg, moe_w1, moe_b1, moe_w2, moe_b2)
        xs = _combine(xs, yg, dest, wts, mod, final_norm_g, ctx_first=not last, final_norm=last)
    return xs
```

```python
import functools

import numpy as np
import jax
import jax.numpy as jnp
from jax import lax
from jax.experimental import pallas as pl
from jax.experimental.pallas import tpu as pltpu

F32 = jnp.float32
BF16 = jnp.bfloat16

D_MODEL = 2048
DEPTH = 2
GRID_W = 64
CTX_LEN = 256
EPS = 1e-6
ROPE_BASE = 10000.0
NEG_INF = -1e30

GLA_HEADS = 4
GLA_DK = 64
GLA_DV = 128
GLA_GATE_RANK = 16
GLA_TAU = 16.0
GLA_CHUNK = 64
NA_HEADS = 8
NA_HEAD_DIM = 64
NA_KH = 8
NA_KW = 16
NA_SCALE = NA_HEAD_DIM ** -0.5
MLA_HEADS = 8
MLA_Q_RANK = 512
MLA_KV_RANK = 512
MLA_NOPE = 128
MLA_ROPE = 64
MLA_V = 128
MLA_SCALE = (MLA_NOPE + MLA_ROPE) ** -0.5
N_BRANCH = 3
N_EXPERTS = 32
TOP_K = 4
EXPERT_FF = D_MODEL
SWIGLU_LIMIT = 7.0
SWIGLU_ALPHA = 1.702

GLA_QK_W = GLA_HEADS * GLA_DK
GLA_V_W = GLA_HEADS * GLA_DV
NA_W = NA_HEADS * NA_HEAD_DIM
MLA_V_W = MLA_HEADS * MLA_V

LANE = 128
SUBLANE = 8
ROW_TILE = 256
VMEM_LIMIT = 56 * 1024 * 1024

COL_GATE = 0
COL_GQ = COL_GATE + N_BRANCH * D_MODEL
COL_GK = COL_GQ + GLA_QK_W
COL_GV = COL_GK + GLA_QK_W
COL_GR = COL_GV + GLA_V_W
COL_NQ = COL_GR + GLA_V_W
COL_NK = COL_NQ + NA_W
COL_NV = COL_NK + NA_W
COL_MCQ = COL_NV + NA_W
COL_MCKV = COL_MCQ + MLA_Q_RANK
COL_KR = COL_MCKV + MLA_KV_RANK
COL_KRP = COL_KR + LANE
COL_GA = COL_KRP + LANE
PROJ_TN = 1536
PROJ_W = 7 * PROJ_TN
assert COL_GA + LANE <= PROJ_W

MOE_G = 1024
MOE_SB = 256
MOE_FC = 256
DISPATCH_TT = 256
COMBINE_TT = 128
ROUTE_TT = 256
MLA_TQ = 1024


def _cparams(sem):
    return pltpu.CompilerParams(dimension_semantics=sem, vmem_limit_bytes=VMEM_LIMIT)


def _nt(a, b):
    return lax.dot_general(a, b, (((1,), (1,)), ((), ())), preferred_element_type=F32)


def _tn(a, b):
    return lax.dot_general(a, b, (((0,), (0,)), ((), ())), preferred_element_type=F32)


def _mm(a, b):
    return jnp.dot(a, b, preferred_element_type=F32)


def _sigmoid(x):
    return 1.0 / (1.0 + jnp.exp(-x))


def _split_hi_lo(x):
    hi = x.astype(BF16)
    lo = (x - hi.astype(F32)).astype(BF16)
    return hi, lo


def _ada_kernel(c_ref, w_ref, b_ref, o_ref):
    c = c_ref[...]
    s = (c * _sigmoid(c)).astype(BF16)
    o_ref[0] = _mm(s, w_ref[0].astype(BF16)) + b_ref[0]


def _ada_mod(cc, ada_w, ada_b):
    nl, d, n6 = ada_w.shape
    tn = 1024
    return pl.pallas_call(
        _ada_kernel,
        grid=(nl, n6 // tn),
        in_specs=[pl.BlockSpec((16, d), lambda l, j: (0, 0)),
                  pl.BlockSpec((1, d, tn), lambda l, j: (l, 0, j)),
                  pl.BlockSpec((1, 1, tn), lambda l, j: (l, 0, j))],
        out_specs=pl.BlockSpec((1, 16, tn), lambda l, j: (l, 0, j)),
        out_shape=jax.ShapeDtypeStruct((nl, 16, n6), F32),
        compiler_params=_cparams(("arbitrary", "arbitrary")),
        name="ada_mod",
    )(cc, ada_w, ada_b.reshape(nl, 1, n6))


def _mod_spec(part, ctx_first):
    def imap(b, j):
        row = jnp.where(j == 0, 8, b) if ctx_first else b
        return (row * 6 + part, 0, 0)
    return pl.BlockSpec((1, 1, D_MODEL), imap)


def _norm_mod_kernel(x_ref, g_ref, sc_ref, sh_ref, o_ref):
    x = x_ref[0]
    ms = jnp.mean(x * x, axis=-1, keepdims=True)
    y = x * lax.rsqrt(ms + EPS) * g_ref[...]
    o_ref[0] = (y * (1.0 + sc_ref[0]) + sh_ref[0]).astype(o_ref.dtype)


def _norm_mod(x, g, mod):
    b, rb, d = x.shape
    return pl.pallas_call(
        _norm_mod_kernel,
        grid=(b, rb // ROW_TILE),
        in_specs=[pl.BlockSpec((1, ROW_TILE, d), lambda i, j: (i, j, 0)),
                  pl.BlockSpec((1, d), lambda i, j: (0, 0)),
                  _mod_spec(1, True), _mod_spec(0, True)],
        out_specs=pl.BlockSpec((1, ROW_TILE, d), lambda i, j: (i, j, 0)),
        out_shape=jax.ShapeDtypeStruct((b, rb, d), BF16),
        compiler_params=_cparams(("arbitrary", "arbitrary")),
        name="norm_mod",
    )(x, g.reshape(1, d), mod, mod)


def _matmul_kernel(x_ref, w_ref, o_ref):
    o_ref[...] = _mm(x_ref[...], w_ref[...]).astype(o_ref.dtype)


def _matmul(x, w, tm, tn, out_dtype):
    m, k = x.shape
    n = w.shape[1]
    return pl.pallas_call(
        _matmul_kernel,
        grid=(n // tn, m // tm),
        in_specs=[pl.BlockSpec((tm, k), lambda j, i: (i, 0)),
                  pl.BlockSpec((k, tn), lambda j, i: (0, j))],
        out_specs=pl.BlockSpec((tm, tn), lambda j, i: (i, j)),
        out_shape=jax.ShapeDtypeStruct((m, n), out_dtype),
        compiler_params=_cparams(("arbitrary", "arbitrary")),
        name="in_proj",
    )(x, w)


def _gla_kernel(q_ref, k_ref, v_ref, r_ref, ab_ref, waf_ref, wab_ref, ba_ref, g_ref, o_ref,
                laf_ref, lab_ref, of_ref, ob_ref, st_ref, *, n_ctx_chunks, n_chunks):
    c = GLA_CHUNK
    ab = ab_ref[0]
    zf = _mm(ab, waf_ref[...]) + ba_ref[0:1, :]
    zb = _mm(ab, wab_ref[...]) + ba_ref[1:2, :]
    laf_ref[...] = (jnp.minimum(zf, 0.0) - jnp.log(1.0 + jnp.exp(-jnp.abs(zf)))) * (1.0 / GLA_TAU)
    lab_ref[...] = (jnp.minimum(zb, 0.0) - jnp.log(1.0 + jnp.exp(-jnp.abs(zb)))) * (1.0 / GLA_TAU)
    st_ref[...] = jnp.zeros_like(st_ref)

    ri = lax.broadcasted_iota(jnp.int32, (c, c), 0)
    ci = lax.broadcasted_iota(jnp.int32, (c, c), 1)
    tri = [(ri >= ci), (ri <= ci)]
    tri_bf = [t.astype(F32).astype(BF16) for t in tri]
    tri4 = [jnp.concatenate([t] * GLA_HEADS, axis=0) for t in tri]
    lane_q = lax.broadcasted_iota(jnp.int32, (c, GLA_QK_W), 1) // GLA_DK
    lane_v = lax.broadcasted_iota(jnp.int32, (c, GLA_V_W), 1) // GLA_DV
    st_row_h = lax.broadcasted_iota(jnp.int32, (GLA_V_W, GLA_QK_W), 0) // GLA_DV
    st_col_h = lax.broadcasted_iota(jnp.int32, (GLA_V_W, GLA_QK_W), 1) // GLA_DK
    st_mask = st_row_h == st_col_h

    def one_dir(d, chunk, la_ref, out_ref):
        r0 = pl.multiple_of(chunk * c, c)
        la = la_ref[pl.ds(r0, c), :]
        hi, lo = _split_hi_lo(la)
        bc = _mm(tri_bf[d], hi) + _mm(tri_bf[d], lo)
        btot = bc[c - 1:c, :] if d == 0 else bc[0:1, :]
        q = q_ref[0, pl.ds(r0, c), :].astype(F32) * (GLA_DK ** -0.5)
        k = k_ref[0, pl.ds(r0, c), :].astype(F32)
        v = v_ref[0, pl.ds(r0, c), :]
        qd = (q * jnp.exp(bc)).astype(BF16)
        ki = (k * jnp.exp(-bc)).astype(BF16)
        kd = (k * jnp.exp(btot - bc)).astype(BF16)
        zero = jnp.zeros_like(qd)
        qs = jnp.concatenate([jnp.where(lane_q == h, qd, zero) for h in range(GLA_HEADS)], axis=0)
        att = _nt(qs, ki)
        att = jnp.where(tri4[d], att, 0.0).astype(BF16)
        rr = _mm(att, v)
        o = _nt(qd, st_ref[d].astype(BF16))
        for h in range(GLA_HEADS):
            o = o + jnp.where(lane_v == h, rr[h * c:(h + 1) * c, :], 0.0)
        out_ref[pl.ds(r0, c), :] = o
        upd = _tn(v, kd)
        st_ref[d] = jnp.where(st_mask, st_ref[d] * jnp.exp(btot) + upd, 0.0)

    def step(i, carry):
        one_dir(0, i, laf_ref, of_ref)
        cb = jnp.where(i < n_ctx_chunks, n_ctx_chunks - 1 - i, n_chunks + n_ctx_chunks - 1 - i)
        one_dir(1, cb, lab_ref, ob_ref)
        return carry

    lax.fori_loop(0, n_chunks, step, 0)

    def epilogue(j, carry):
        r0 = pl.multiple_of(j * ROW_TILE, ROW_TILE)
        o = of_ref[pl.ds(r0, ROW_TILE), :] + ob_ref[pl.ds(r0, ROW_TILE), :]
        r = r_ref[0, pl.ds(r0, ROW_TILE), :].astype(F32)
        gate = r * _sigmoid(r)
        for h in range(GLA_HEADS):
            sl = slice(h * GLA_DV, (h + 1) * GLA_DV)
            oh = o[:, sl]
            ms = jnp.mean(oh * oh, axis=-1, keepdims=True)
            y = oh * lax.rsqrt(ms + EPS) * g_ref[:, sl]
            o_ref[0, pl.ds(r0, ROW_TILE), sl] = (y * gate[:, sl]).astype(o_ref.dtype)
        return carry

    lax.fori_loop(0, (n_chunks * c) // ROW_TILE, epilogue, 0)


def _gla(p3, waf, wab, ba, g):
    b, rb, _ = p3.shape
    n_chunks = rb // GLA_CHUNK
    kern = functools.partial(_gla_kernel, n_ctx_chunks=CTX_LEN // GLA_CHUNK, n_chunks=n_chunks)

    def col(width, off):
        return pl.BlockSpec((1, rb, width), lambda i: (i, 0, off // width))

    return pl.pallas_call(
        kern,
        grid=(b,),
        in_specs=[col(GLA_QK_W, COL_GQ), col(GLA_QK_W, COL_GK), col(GLA_V_W, COL_GV), col(GLA_V_W, COL_GR),
                  col(LANE, COL_GA),
                  pl.BlockSpec((LANE, GLA_QK_W), lambda i: (0, 0)),
                  pl.BlockSpec((LANE, GLA_QK_W), lambda i: (0, 0)),
                  pl.BlockSpec((2, GLA_QK_W), lambda i: (0, 0)),
                  pl.BlockSpec((1, GLA_V_W), lambda i: (0, 0))],
        out_specs=pl.BlockSpec((1, rb, GLA_V_W), lambda i: (i, 0, 0)),
        out_shape=jax.ShapeDtypeStruct((b, rb, GLA_V_W), BF16),
        scratch_shapes=[pltpu.VMEM((rb, GLA_QK_W), F32), pltpu.VMEM((rb, GLA_QK_W), F32),
                        pltpu.VMEM((rb, GLA_V_W), F32), pltpu.VMEM((rb, GLA_V_W), F32),
                        pltpu.VMEM((2, GLA_V_W, GLA_QK_W), F32)],
        compiler_params=_cparams(("arbitrary",)),
        name="gla",
    )(p3, p3, p3, p3, p3, waf, wab, ba, g)


def _softmax_pv(parts):
    m = parts[0][0].max(axis=-1, keepdims=True)
    for s, _ in parts[1:]:
        m = jnp.maximum(m, s.max(axis=-1, keepdims=True))
    acc = None
    den = None
    for s, v in parts:
        e = jnp.exp(s - m)
        l = e.sum(axis=-1, keepdims=True)
        pv = _mm(e.astype(BF16), v)
        acc = pv if acc is None else acc + pv
        den = l if den is None else den + l
    return acc / den


def _na_kernel(q_ref, k_ref, v_ref, bias_ref, o_ref, *, n_rows):
    kh = NA_KH
    lane = lax.broadcasted_iota(jnp.int32, (GRID_W, LANE), 1)
    head_mask = [lane < NA_HEAD_DIM, lane >= NA_HEAD_DIM]
    kc = k_ref[0, 0:CTX_LEN, :]
    vc = v_ref[0, 0:CTX_LEN, :]

    def row(r, carry):
        rs = jnp.clip(r - kh // 2, 0, n_rows - kh)
        off = rs - r + (NA_KH - 1)
        q0 = pl.multiple_of(CTX_LEN + r * GRID_W, GRID_W)
        k0 = pl.multiple_of(CTX_LEN + rs * GRID_W, GRID_W)
        q = q_ref[0, pl.ds(q0, GRID_W), :]
        kl = k_ref[0, pl.ds(k0, kh * GRID_W), :]
        vl = v_ref[0, pl.ds(k0, kh * GRID_W), :]
        outs = []
        for hh in range(2):
            qm = jnp.where(head_mask[hh], q, jnp.zeros_like(q))
            s_loc = _nt(qm, kl) * NA_SCALE + bias_ref[hh, off]
            s_ctx = _nt(qm, kc) * NA_SCALE
            outs.append(_softmax_pv([(s_loc, vl), (s_ctx, vc)]))
        o_ref[0, pl.ds(q0, GRID_W), :] = jnp.where(head_mask[0], outs[0], outs[1]).astype(o_ref.dtype)
        return carry

    lax.fori_loop(0, n_rows, row, 0, unroll=4)

    qc = q_ref[0, 0:CTX_LEN, :]
    lane_c = lax.broadcasted_iota(jnp.int32, (CTX_LEN, LANE), 1)
    outs = []
    for hh in range(2):
        msk = (lane_c < NA_HEAD_DIM) if hh == 0 else (lane_c >= NA_HEAD_DIM)
        qm = jnp.where(msk, qc, jnp.zeros_like(qc))
        outs.append(_softmax_pv([(_nt(qm, kc) * NA_SCALE, vc)]))
    o_ref[0, 0:CTX_LEN, :] = jnp.where(lane_c < NA_HEAD_DIM, outs[0], outs[1]).astype(o_ref.dtype)


def _na_bias_table(rpb):
    o = np.arange(NA_KH)[:, None] + np.arange(NA_KH)[None, :]
    qcol = np.arange(GRID_W)[:, None]
    kcol = np.arange(GRID_W)[None, :]
    wstart = np.clip(qcol - NA_KW // 2, 0, GRID_W - NA_KW)
    in_win = (kcol >= wstart) & (kcol < wstart + NA_KW)
    dc = np.clip(kcol - qcol + NA_KW - 1, 0, 2 * NA_KW - 2)
    del o
    t = jnp.stack([rpb[:, s:s + NA_KH, :] for s in range(NA_KH)], axis=1)
    onehot = (dc[None] == np.arange(2 * NA_KW - 1)[:, None, None]).astype(np.float32)
    t = jnp.einsum('hoic,cqk->hoqik', t.astype(F32), jnp.asarray(onehot), precision=lax.Precision.HIGHEST)
    t = jnp.where(in_win[None, None, :, None, :], t, NEG_INF)
    return t.reshape(NA_HEADS, NA_KH, GRID_W, NA_KH * GRID_W)


def _na(p3, bias):
    b, rb, _ = p3.shape
    n_rows = (rb - CTX_LEN) // GRID_W
    kern = functools.partial(_na_kernel, n_rows=n_rows)

    def col(off):
        return pl.BlockSpec((1, rb, LANE), lambda p, i: (i, 0, off // LANE + p))

    return pl.pallas_call(
        kern,
        grid=(NA_HEADS // 2, b),
        in_specs=[col(COL_NQ), col(COL_NK), col(COL_NV),
                  pl.BlockSpec((2, NA_KH, GRID_W, NA_KH * GRID_W), lambda p, i: (p, 0, 0, 0))],
        out_specs=pl.BlockSpec((1, rb, LANE), lambda p, i: (i, 0, p)),
        out_shape=jax.ShapeDtypeStruct((b, rb, NA_W), BF16),
        compiler_params=_cparams(("arbitrary", "arbitrary")),
        name="na",
    )(p3, p3, p3, bias)


def _mla_proj_kernel(cq_ref, ckv_ref, kr_ref, krp_ref, cos_ref, sin_ref, gq_ref, gkv_ref,
                     wq_ref, wq2_ref, wkv_ref, q_ref, k_ref, v_ref):
    def norm(x, g):
        x = x.astype(F32)
        ms = jnp.mean(x * x, axis=-1, keepdims=True)
        return (x * lax.rsqrt(ms + EPS) * g).astype(BF16)

    cos = cos_ref[...]
    sin = sin_ref[...]
    nq = norm(cq_ref[0], gq_ref[...])
    yq = _mm(nq, wq_ref[...])
    yq2 = _mm(nq, wq2_ref[...])
    nkv = norm(ckv_ref[0], gkv_ref[...])
    ykv = _mm(nkv, wkv_ref[...])
    k_rot = (kr_ref[0].astype(F32) * cos + krp_ref[0].astype(F32) * sin).astype(BF16)
    for h in range(MLA_HEADS):
        a = 2 * h * LANE
        q_ref[0, :, a:a + LANE] = (yq[:, a:a + LANE] * MLA_SCALE).astype(BF16)
        rot = yq[:, a + LANE:a + 2 * LANE] * cos + yq2[:, h * LANE:(h + 1) * LANE] * sin
        q_ref[0, :, a + LANE:a + 2 * LANE] = (rot * MLA_SCALE).astype(BF16)
        k_ref[0, :, a:a + LANE] = ykv[:, h * LANE:(h + 1) * LANE].astype(BF16)
        k_ref[0, :, a + LANE:a + 2 * LANE] = k_rot
        v0 = MLA_HEADS * MLA_NOPE + h * MLA_V
        v_ref[0, :, a:a + LANE] = ykv[:, v0:v0 + MLA_V].astype(BF16)
        v_ref[0, :, a + LANE:a + 2 * LANE] = jnp.ones((ykv.shape[0], LANE), BF16)


def _mla_proj(p3, cos_t, sin_t, gq, gkv, wq, wq2, wkv):
    b, rb, _ = p3.shape
    tm = ROW_TILE

    def col(width, off):
        return pl.BlockSpec((1, tm, width), lambda i, j: (i, j, off // width))

    def full(a):
        return pl.BlockSpec(a.shape, lambda i, j: (0, 0))

    hw = MLA_HEADS * 2 * LANE
    nj = rb // tm
    return pl.pallas_call(
        _mla_proj_kernel,
        grid=(b, nj),
        in_specs=[col(MLA_Q_RANK, COL_MCQ), col(MLA_KV_RANK, COL_MCKV), col(LANE, COL_KR), col(LANE, COL_KRP),
                  pl.BlockSpec((tm, LANE), lambda i, j: (j, 0)), pl.BlockSpec((tm, LANE), lambda i, j: (j, 0)),
                  full(gq), full(gkv), full(wq), full(wq2), full(wkv)],
        out_specs=[pl.BlockSpec((1, tm, hw), lambda i, j: (i, (j + nj - 1) % nj, 0)),
                   pl.BlockSpec((1, tm, hw), lambda i, j: (i, (j + nj - 1) % nj, 0)),
                   pl.BlockSpec((1, tm, hw), lambda i, j: (i, (j + nj - 1) % nj, 0))],
        out_shape=[jax.ShapeDtypeStruct((b, rb, hw), BF16), jax.ShapeDtypeStruct((b, rb, hw), BF16),
                   jax.ShapeDtypeStruct((b, rb, hw), BF16)],
        compiler_params=_cparams(("arbitrary", "arbitrary")),
        name="mla_proj",
    )(p3, p3, p3, p3, cos_t, sin_t, gq, gkv, wq, wq2, wkv)


def _mla_attn_kernel(q_ref, k_ref, v_ref, o_ref):
    k = k_ref[0]
    v = v_ref[0]
    for r0 in range(0, q_ref.shape[1], ROW_TILE):
        s = _nt(q_ref[0, r0:r0 + ROW_TILE, :], k)
        e = jnp.exp((s - s.max(axis=-1, keepdims=True)).astype(BF16))
        pv = _mm(e, v)
        o_ref[0, r0:r0 + ROW_TILE, :] = (pv[:, :MLA_V] / pv[:, MLA_V:]).astype(o_ref.dtype)


def _mla_attn(q, k, v, ctx_queries):
    b, rb, _ = q.shape
    n_lat = rb - CTX_LEN
    if ctx_queries:
        tq, nq, n_keys = CTX_LEN, 1, CTX_LEN
        q_off = kv_blk = n_lat // CTX_LEN
    else:
        tq = MLA_TQ if n_lat % MLA_TQ == 0 else ROW_TILE
        nq, n_keys, q_off, kv_blk = n_lat // tq, rb, 0, 0
    return pl.pallas_call(
        _mla_attn_kernel,
        grid=(b, MLA_HEADS, nq),
        in_specs=[pl.BlockSpec((1, tq, 2 * LANE), lambda i, h, j: (i, j + q_off, h)),
                  pl.BlockSpec((1, n_keys, 2 * LANE), lambda i, h, j: (i, kv_blk, h)),
                  pl.BlockSpec((1, n_keys, 2 * LANE), lambda i, h, j: (i, kv_blk, h))],
        out_specs=pl.BlockSpec((1, tq, MLA_V), lambda i, h, j: (i, j, h)),
        out_shape=jax.ShapeDtypeStruct((b, nq * tq, MLA_V_W), BF16),
        compiler_params=_cparams(("arbitrary", "arbitrary", "arbitrary")),
        name="mla_attn_ctx" if ctx_queries else "mla_attn",
    )(q, k, v)


def _merge_kernel(x_ref, oa_ref, ob_ref, oc_ref, occ_ref, ga_ref, gb_ref, gc_ref, bg_ref,
                  wa_ref, wb_ref, wc_ref, wo_ref, gt1_ref, sc2_ref, sh2_ref, g2_ref, rw_ref, rb_ref,
                  xo_ref, h_ref, lg_ref, *, ctx_tile):
    d = D_MODEL

    def gate(g_ref, k):
        return _sigmoid(g_ref[0].astype(F32) + bg_ref[:, k * d:(k + 1) * d])

    oc = oc_ref[0]
    if ctx_tile:
        oc = jnp.where(pl.program_id(1) == 0, occ_ref[0], oc)
    m = gate(ga_ref, 0) * _mm(oa_ref[0], wa_ref[...])
    m = m + gate(gb_ref, 1) * _mm(ob_ref[0], wb_ref[...])
    m = m + gate(gc_ref, 2) * _mm(oc, wc_ref[...])
    y = _mm(m.astype(BF16), wo_ref[...])
    x = x_ref[0] + gt1_ref[0] * y
    xo_ref[0] = x
    ms = jnp.mean(x * x, axis=-1, keepdims=True)
    h = x * lax.rsqrt(ms + EPS) * g2_ref[...]
    h = h * (1.0 + sc2_ref[0]) + sh2_ref[0]
    h_ref[...] = h
    hh, hl = _split_hi_lo(h)
    wh, wl = _split_hi_lo(rw_ref[...])
    lg_ref[...] = _nt(wh, hh) + _nt(wh, hl) + _nt(wl, hh) + rb_ref[:, 0:1]


def _merge(x, p3, og, on, om, om_ctx, bg, wa, wb, wc, wo, mod, g2, rwt, rbias, skip_ctx):
    b, rb, d = x.shape
    tm = ROW_TILE
    jo = 1 if skip_ctx else 0
    nj = rb // tm - jo
    rows_out = nj * tm

    def rows(width, cblk=0):
        return pl.BlockSpec((1, tm, width), lambda i, j: (i, j + jo, cblk))

    om_spec = pl.BlockSpec((1, tm, MLA_V_W), lambda i, j: (i, jnp.maximum(j + jo - 1, 0), 0))
    omc_spec = pl.BlockSpec((1, tm, MLA_V_W), lambda i, j: (i, 0, 0))
    kern = functools.partial(_merge_kernel, ctx_tile=not skip_ctx)

    def full(a):
        return pl.BlockSpec(a.shape, lambda i, j: (0,) * a.ndim, pipeline_mode=pl.Buffered(1))

    def mod_spec(part):
        def imap(i, j):
            row = i if skip_ctx else jnp.where(j == 0, 8, i)
            return (row * 6 + part, 0, 0)
        return pl.BlockSpec((1, 1, d), imap)

    return pl.pallas_call(
        kern,
        grid=(b, nj),
        in_specs=[rows(d), rows(GLA_V_W), rows(NA_W), om_spec, omc_spec,
                  rows(d, 0), rows(d, 1), rows(d, 2), full(bg),
                  full(wa), full(wb), full(wc), full(wo),
                  mod_spec(2), mod_spec(4), mod_spec(3), full(g2), full(rwt), full(rbias)],
        out_specs=[pl.BlockSpec((1, tm, d), lambda i, j: (i, j, 0)),
                   pl.BlockSpec((tm, d), lambda i, j: (i * nj + j, 0)),
                   pl.BlockSpec((N_EXPERTS, tm), lambda i, j: (0, i * nj + j))],
        out_shape=[jax.ShapeDtypeStruct((b, rows_out, d), F32),
                   jax.ShapeDtypeStruct((b * rows_out, d), F32),
                   jax.ShapeDtypeStruct((N_EXPERTS, b * rows_out), F32)],
        compiler_params=_cparams(("arbitrary", "arbitrary")),
        name="merge",
    )(x, og, on, om, om if skip_ctx else om_ctx, p3, p3, p3, bg, wa, wb, wc, wo, mod, mod, mod, g2, rwt, rbias)


def _route_kernel(l_ref, idx_ref, w_ref, rank_ref, cnt_ref, carry_ref):
    i = pl.program_id(0)
    tt = l_ref.shape[1]

    @pl.when(i == 0)
    def _():
        carry_ref[...] = jnp.zeros_like(carry_ref)

    l = l_ref[...]
    eio = lax.broadcasted_iota(jnp.int32, (N_EXPERTS, tt), 0)
    vals, idxs = [], []
    for _ in range(TOP_K):
        m = l.max(axis=0, keepdims=True)
        ik = jnp.min(jnp.where(l == m, eio, N_EXPERTS), axis=0, keepdims=True)
        vals.append(m)
        idxs.append(ik)
        l = jnp.where(eio == ik, -jnp.inf, l)
    es = [jnp.exp(v - vals[0]) for v in vals]
    den = es[0] + es[1] + es[2] + es[3]
    sel = jnp.zeros((N_EXPERTS, tt), F32)
    for ik in idxs:
        sel = sel + (eio == ik).astype(F32)
    si = lax.broadcasted_iota(jnp.int32, (tt, tt), 0)
    ti = lax.broadcasted_iota(jnp.int32, (tt, tt), 1)
    before = (si < ti).astype(F32).astype(BF16)
    rank_full = _mm(sel.astype(BF16), before) + carry_ref[:, 0:1]
    for k in range(TOP_K):
        idx_ref[k:k + 1, :] = idxs[k]
        w_ref[k:k + 1, :] = es[k] / den
        rk = jnp.sum(jnp.where(eio == idxs[k], rank_full, 0.0), axis=0, keepdims=True)
        rank_ref[k:k + 1, :] = rk.astype(jnp.int32)
    carry_ref[...] = carry_ref[...] + jnp.sum(sel, axis=1, keepdims=True)
    cnt_ref[...] = carry_ref[...]


def _route(logits_t):
    ne, t = logits_t.shape
    tt = ROUTE_TT
    spec4 = pl.BlockSpec((TOP_K, tt), lambda i: (0, i))
    return pl.pallas_call(
        _route_kernel,
        grid=(t // tt,),
        in_specs=[pl.BlockSpec((ne, tt), lambda i: (0, i))],
        out_specs=[spec4, spec4, spec4, pl.BlockSpec((ne, LANE), lambda i: (0, 0))],
        out_shape=[jax.ShapeDtypeStruct((TOP_K, t), jnp.int32), jax.ShapeDtypeStruct((TOP_K, t), F32),
                   jax.ShapeDtypeStruct((TOP_K, t), jnp.int32), jax.ShapeDtypeStruct((ne, LANE), F32)],
        scratch_shapes=[pltpu.VMEM((ne, LANE), F32)],
        compiler_params=_cparams(("arbitrary",)),
        name="route",
    )(logits_t)


def _pad_fill(ps_ref, pn_ref, zero_ref, xg_ref, sem, wait):
    def copy(pos, rows):
        cp = pltpu.make_async_copy(zero_ref.at[pl.ds(0, rows), :], xg_ref.at[pl.ds(pos, rows), :], sem)
        cp.wait() if wait else cp.start()

    def per_expert(e, carry):
        pos = ps_ref[e]
        head = (-pos) & (SUBLANE - 1)
        for r in range(SUBLANE - 1):
            @pl.when(r < head)
            def _(r=r):
                copy(pos + r, 1)

        pos = pos + head
        n = pn_ref[e] - head
        bit = MOE_G // 2
        while bit >= SUBLANE:
            on = (n & bit) != 0

            @pl.when(on)
            def _(pos=pos, bit=bit):
                copy(pl.multiple_of(pos, SUBLANE), bit)

            pos = pos + jnp.where(on, bit, 0)
            bit //= 2
        return carry

    lax.fori_loop(0, N_EXPERTS, per_expert, 0)

    zr = zero_ref.shape[0]

    def tail(i, carry):
        pos = pl.multiple_of(ps_ref[N_EXPERTS] + i * zr, zr)
        cp = pltpu.make_async_copy(zero_ref, xg_ref.at[pl.ds(pos, zr), :], sem)
        cp.wait() if wait else cp.start()
        return carry

    lax.fori_loop(0, pn_ref[N_EXPERTS], tail, 0)


def _dispatch_kernel(ps_ref, pn_ref, dest_ref, h_ref, xg_ref, zero_ref, sem, zsem):
    tt = h_ref.shape[0]

    @pl.when(pl.program_id(0) == 0)
    def _():
        zero_ref[...] = jnp.zeros_like(zero_ref)
        _pad_fill(ps_ref, pn_ref, zero_ref, xg_ref, zsem, wait=False)

    def issue(t, carry):
        for k in range(TOP_K):
            pltpu.make_async_copy(h_ref.at[pl.ds(t, 1), :], xg_ref.at[pl.ds(dest_ref[0, k, t], 1), :], sem).start()
        return carry

    lax.fori_loop(0, tt, issue, 0)
    for k in range(TOP_K):
        pltpu.make_async_copy(h_ref, xg_ref.at[pl.ds(0, tt), :], sem).wait()

    @pl.when(pl.program_id(0) == 0)
    def _():
        _pad_fill(ps_ref, pn_ref, zero_ref, xg_ref, zsem, wait=True)


def _dispatch(hp, dest, pad_start, pad_len, n_slots):
    t, w = hp.shape
    tt = DISPATCH_TT
    dest3 = dest.reshape(TOP_K, t // tt, tt).transpose(1, 0, 2)
    grid_spec = pltpu.PrefetchScalarGridSpec(
        num_scalar_prefetch=2,
        grid=(t // tt,),
        in_specs=[pl.BlockSpec((1, TOP_K, tt), lambda i, ps, pn: (i, 0, 0), memory_space=pltpu.SMEM),
                  pl.BlockSpec((tt, w), lambda i, ps, pn: (i, 0))],
        out_specs=pl.BlockSpec(memory_space=pl.ANY),
        scratch_shapes=[pltpu.VMEM((MOE_G // 2, w), hp.dtype), pltpu.SemaphoreType.DMA(()),
                        pltpu.SemaphoreType.DMA(())],
    )
    return pl.pallas_call(
        _dispatch_kernel,
        grid_spec=grid_spec,
        out_shape=jax.ShapeDtypeStruct((n_slots, w), hp.dtype),
        compiler_params=_cparams(("arbitrary",)),
        name="dispatch",
    )(pad_start, pad_len, dest3, hp)


def _ffn_kernel(te_ref, tv_ref, x_ref, w1g_ref, w1l_ref, b1g_ref, b1l_ref, w2_ref, b2_ref, y_ref,
                xb_ref, *, n_fc, n_tiles):
    i = pl.program_id(0)
    j = pl.program_id(1)
    valid = tv_ref[i]
    g = y_ref.shape[0]
    chunk = x_ref.shape[0]

    @pl.when(i < n_tiles)
    def _():
        xb_ref[i % 2, pl.ds(pl.multiple_of(j * chunk, chunk), chunk), :] = x_ref[...].astype(BF16)

    cur = (i + 1) % 2

    @pl.when((valid > 0) & (j == 0))
    def _():
        y_ref[...] = jnp.broadcast_to(b2_ref[0, 0], y_ref.shape)

    for nr in range(MOE_SB, g + 1, MOE_SB):
        @pl.when((valid > nr - MOE_SB) & (valid <= nr))
        def _(nr=nr):
            wg = w1g_ref[0, 0].astype(BF16)
            wl = w1l_ref[0, 0].astype(BF16)
            w2 = w2_ref[0, 0].astype(BF16)
            for r0 in range(0, nr, MOE_SB):
                x = xb_ref[cur, r0:r0 + MOE_SB, :]
                ug = _mm(x, wg) + b1g_ref[0, 0]
                ul = _mm(x, wl) + b1l_ref[0, 0]
                xg = jnp.minimum(ug, SWIGLU_LIMIT)
                xl = jnp.clip(ul, -SWIGLU_LIMIT, SWIGLU_LIMIT)
                act = xg * _sigmoid(SWIGLU_ALPHA * xg) * (xl + 1.0)
                y_ref[r0:r0 + MOE_SB, :] = y_ref[r0:r0 + MOE_SB, :] + _mm(act.astype(BF16), w2)

    @pl.when((j == n_fc - 1) & (valid == 0))
    def _():
        y_ref[...] = jnp.zeros_like(y_ref)


def _ffn(layer, tile_expert, tile_valid, xg, w1, b1, w2, b2):
    n_slots, d = xg.shape
    nl, ne, _, ff2 = w1.shape
    ff = ff2 // 2
    n_fc = ff // MOE_FC
    n_tiles = n_slots // MOE_G
    chunk = MOE_G // n_fc
    kern = functools.partial(_ffn_kernel, n_fc=n_fc, n_tiles=n_tiles)
    tile_expert = jnp.concatenate([tile_expert[:1], tile_expert])
    tile_valid = jnp.concatenate([jnp.zeros((1,), tile_valid.dtype), tile_valid])

    def jj(j, tv, i):
        return jnp.where(tv[i] > 0, j, n_fc - 1)

    grid_spec = pltpu.PrefetchScalarGridSpec(
        num_scalar_prefetch=2,
        grid=(n_tiles + 1, n_fc),
        in_specs=[pl.BlockSpec((chunk, d), lambda i, j, te, tv: (jnp.minimum(i, n_tiles - 1) * n_fc + j, 0)),
                  pl.BlockSpec((1, 1, d, MOE_FC), lambda i, j, te, tv: (layer, te[i], 0, jj(j, tv, i))),
                  pl.BlockSpec((1, 1, d, MOE_FC), lambda i, j, te, tv: (layer, te[i], 0, jj(j, tv, i) + n_fc)),
                  pl.BlockSpec((1, 1, 1, MOE_FC), lambda i, j, te, tv: (layer, te[i], 0, jj(j, tv, i))),
                  pl.BlockSpec((1, 1, 1, MOE_FC), lambda i, j, te, tv: (layer, te[i], 0, jj(j, tv, i) + n_fc)),
                  pl.BlockSpec((1, 1, MOE_FC, d), lambda i, j, te, tv: (layer, te[i], jj(j, tv, i), 0)),
                  pl.BlockSpec((1, 1, 1, d), lambda i, j, te, tv: (layer, te[i], 0, 0))],
        out_specs=pl.BlockSpec((MOE_G, d), lambda i, j, te, tv: (jnp.maximum(i - 1, 0), 0)),
        scratch_shapes=[pltpu.VMEM((2, MOE_G, d), BF16)],
    )
    return pl.pallas_call(
        kern,
        grid_spec=grid_spec,
        out_shape=jax.ShapeDtypeStruct((n_slots, d), F32),
        compiler_params=_cparams(("arbitrary", "arbitrary")),
        name="moe_ffn",
    )(tile_expert, tile_valid, xg, w1, w1, b1.reshape(nl, ne, 1, ff2), b1.reshape(nl, ne, 1, ff2), w2,
      b2.reshape(nl, ne, 1, d))


def _combine_kernel(dest_ref, destn_ref, x_ref, wt_ref, gt2_ref, gf_ref, yg_ref, o_ref, buf_ref, sem,
                    *, final_norm):
    i = pl.program_id(0)
    n = pl.num_programs(0)
    tt = x_ref.shape[1]
    slot = i % 2

    def issue(d_ref, s):
        def body(t, carry):
            for k in range(TOP_K):
                pltpu.make_async_copy(yg_ref.at[pl.ds(d_ref[0, k, t], 1), :],
                                      buf_ref.at[s, k, pl.ds(t, 1), :], sem.at[s]).start()
            return carry
        lax.fori_loop(0, tt, body, 0)

    @pl.when(i == 0)
    def _():
        issue(dest_ref, 0)

    @pl.when(i + 1 < n)
    def _():
        issue(destn_ref, 1 - slot)

    for k in range(TOP_K):
        pltpu.make_async_copy(yg_ref.at[pl.ds(0, tt), :], buf_ref.at[slot, k], sem.at[slot]).wait()

    wt = wt_ref[...]
    y = buf_ref[slot, 0] * wt[:, 0:1]
    for k in range(1, TOP_K):
        y = y + buf_ref[slot, k] * wt[:, k:k + 1]
    x = x_ref[0] + gt2_ref[0] * y
    if final_norm:
        ms = jnp.mean(x * x, axis=-1, keepdims=True)
        x = x * lax.rsqrt(ms + EPS) * gf_ref[...]
    o_ref[0] = x


def _combine(x, yg, dest, wts, mod, gf, ctx_first, final_norm):
    b, rows, d = x.shape
    tt = COMBINE_TT
    nj = rows // tt
    t = b * rows
    nt = t // tt
    dest3 = dest.reshape(TOP_K, nt, tt).transpose(1, 0, 2)
    wt = wts.T
    kern = functools.partial(_combine_kernel, final_norm=final_norm)

    def mod_imap(i):
        bi = i // nj
        row = jnp.where((i % nj) * tt < CTX_LEN, 8, bi) if ctx_first else bi
        return (row * 6 + 5, 0, 0)

    return pl.pallas_call(
        kern,
        grid=(nt,),
        in_specs=[pl.BlockSpec((1, TOP_K, tt), lambda i: (i, 0, 0), memory_space=pltpu.SMEM),
                  pl.BlockSpec((1, TOP_K, tt), lambda i: (jnp.minimum(i + 1, nt - 1), 0, 0),
                               memory_space=pltpu.SMEM),
                  pl.BlockSpec((1, tt, d), lambda i: (i // nj, i % nj, 0)),
                  pl.BlockSpec((tt, TOP_K), lambda i: (i, 0)),
                  pl.BlockSpec((1, 1, d), mod_imap),
                  pl.BlockSpec((1, d), lambda i: (0, 0)),
                  pl.BlockSpec(memory_space=pl.ANY)],
        out_specs=pl.BlockSpec((1, tt, d), lambda i: (i // nj, i % nj, 0)),
        out_shape=jax.ShapeDtypeStruct((b, rows, d), F32),
        scratch_shapes=[pltpu.VMEM((2, TOP_K, tt, d), yg.dtype), pltpu.SemaphoreType.DMA((2,))],
        compiler_params=_cparams(("arbitrary",)),
        name="combine",
    )(dest3, dest3, x, wt, mod, gf.reshape(1, d), yg)


def _proj_weight(w_in):
    d = w_in.shape[0]
    splits = (GLA_QK_W, GLA_QK_W, GLA_V_W, GLA_V_W, GLA_GATE_RANK, GLA_GATE_RANK,
              NA_W, NA_W, NA_W, MLA_Q_RANK, MLA_KV_RANK, MLA_ROPE, N_BRANCH * D_MODEL)
    pts = np.cumsum((0,) + splits)
    (gq, gk, gv, gr, gaf, gab, nq, nk, nv, mcq, mckv, mkr, gate) = [w_in[:, pts[i]:pts[i + 1]] for i in range(13)]
    q16 = MLA_ROPE // 4
    mkrp = jnp.concatenate([mkr[:, q16:2 * q16], mkr[:, :q16], mkr[:, 3 * q16:], mkr[:, 2 * q16:3 * q16]], axis=1)
    z = lambda n: jnp.zeros((d, n), w_in.dtype)
    cols = [gate, gq, gk, gv, gr, nq, nk, nv, mcq, mckv,
            mkr, z(LANE - MLA_ROPE), mkrp, z(LANE - MLA_ROPE),
            gaf, gab, z(LANE - 2 * GLA_GATE_RANK)]
    w = jnp.concatenate(cols, axis=1)
    w = jnp.concatenate([w, z(PROJ_W - w.shape[1])], axis=1)
    return w.astype(BF16)


def _rope_tables(rb):
    n = rb - CTX_LEN
    t = np.arange(n)
    nf = MLA_ROPE // 4
    freqs = ROPE_BASE ** (-np.arange(nf, dtype=np.float64) / nf)
    cos = np.zeros((rb, LANE), np.float32)
    sin = np.zeros((rb, LANE), np.float32)
    cos[:CTX_LEN, :MLA_ROPE] = 1.0
    for a, pos in enumerate((t // GRID_W, t % GRID_W)):
        ang = (pos.astype(np.float32)[:, None] * freqs.astype(np.float32)[None, :]).astype(np.float32)
        c, s = np.cos(ang), np.sin(ang)
        base = a * 2 * nf
        cos[CTX_LEN:, base:base + nf] = c
        cos[CTX_LEN:, base + nf:base + 2 * nf] = c
        sin[CTX_LEN:, base:base + nf] = -s
        sin[CTX_LEN:, base + nf:base + 2 * nf] = s
    return jnp.asarray(cos), jnp.asarray(sin)


def _mla_weights(w_q_up, w_kv_up):
    r = w_q_up.shape[0]
    wq = w_q_up.reshape(r, MLA_HEADS, MLA_NOPE + MLA_ROPE)
    nope, rope = wq[..., :MLA_NOPE], wq[..., MLA_NOPE:]
    q16 = MLA_ROPE // 4
    ropep = jnp.concatenate([rope[..., q16:2 * q16], rope[..., :q16], rope[..., 3 * q16:], rope[..., 2 * q16:3 * q16]],
                            axis=-1)
    zpad = jnp.zeros((r, MLA_HEADS, LANE - MLA_ROPE), w_q_up.dtype)
    wq1 = jnp.concatenate([nope, rope, zpad], axis=-1).reshape(r, MLA_HEADS * 2 * LANE).astype(BF16)
    wq2 = jnp.concatenate([ropep, zpad], axis=-1).reshape(r, MLA_HEADS * LANE).astype(BF16)
    rk = w_kv_up.shape[0]
    wkv = w_kv_up.reshape(rk, MLA_HEADS, 2, MLA_NOPE).transpose(0, 2, 1, 3).reshape(rk, 2 * MLA_HEADS * MLA_NOPE)
    return wq1, wq2, wkv.astype(BF16)


def _moe_plan(idx, rank, counts, n_tiles):
    cnt = counts[:, 0].astype(jnp.int32)
    padded = ((cnt + MOE_G - 1) // MOE_G) * MOE_G
    ends = jnp.cumsum(padded)
    starts = ends - padded
    e_ids = jnp.arange(N_EXPERTS, dtype=jnp.int32)
    dest = jnp.sum(jnp.where(idx[..., None] == e_ids, starts, 0), axis=-1) + rank
    tile_start = jnp.arange(n_tiles, dtype=jnp.int32) * MOE_G
    te = jnp.sum((tile_start[:, None] >= ends[None, :]).astype(jnp.int32), axis=1)
    active = te < N_EXPERTS
    te_c = jnp.minimum(te, N_EXPERTS - 1)
    tile_is = te_c[:, None] == e_ids[None, :]
    cnt_t = jnp.sum(jnp.where(tile_is, cnt, 0), axis=1)
    start_t = jnp.sum(jnp.where(tile_is, starts, 0), axis=1)
    valid = jnp.clip(cnt_t - (tile_start - start_t), 0, MOE_G)
    valid = jnp.where(active, valid, 0)
    last_e = jnp.max(jnp.where(cnt > 0, jnp.arange(N_EXPERTS, dtype=jnp.int32), 0))
    te_f = jnp.where(active, te_c, last_e)
    tail_blocks = (n_tiles * MOE_G - ends[-1]) // (MOE_G // 2)
    pad_start = jnp.concatenate([starts + cnt, ends[-1:]]).astype(jnp.int32)
    pad_len = jnp.concatenate([padded - cnt, tail_blocks[None]]).astype(jnp.int32)
    return dest, te_f, valid, pad_start, pad_len


def kernel(x, c, ctx, c_ctx, norm1_g, norm2_g, ada_w, ada_b, w_in, b_gate, gla_wa, gla_ba, gla_norm_g,
           na_rpb, mla_q_norm_g, mla_w_q_up, mla_kv_norm_g, mla_w_kv_up, w_branch_gla, w_branch_na,
           w_branch_mla, w_out, router_w, router_b, moe_w1, moe_b1, moe_w2, moe_b2, final_norm_g):
    b, n, d = x.shape
    rb = CTX_LEN + n
    assert b <= 8 and d == D_MODEL and ctx.shape[1] == CTX_LEN

    cc = jnp.zeros((16, d), F32).at[:b].set(c).at[8].set(c_ctx)
    mod_all = _ada_mod(cc, ada_w, ada_b)
    cos_t, sin_t = _rope_tables(rb)
    xs = jnp.concatenate([ctx, x], axis=1)

    for l in range(DEPTH):
        last = l == DEPTH - 1
        mod = mod_all[l].reshape(16 * 6, 1, d)
        h = _norm_mod(xs, norm1_g[l], mod)
        p = _matmul(h.reshape(b * rb, d), _proj_weight(w_in[l]), 1024 if (b * rb) % 1024 == 0 else ROW_TILE,
                    PROJ_TN, BF16)
        p3 = p.reshape(b, rb, PROJ_W)

        zpad = jnp.zeros((LANE - 2 * GLA_GATE_RANK, GLA_QK_W), F32)
        zr = jnp.zeros((GLA_GATE_RANK, GLA_QK_W), F32)
        waf = jnp.concatenate([gla_wa[l, 0], zr, zpad], axis=0).astype(BF16)
        wab = jnp.concatenate([zr, gla_wa[l, 1], zpad], axis=0).astype(BF16)
        og = _gla(p3, waf, wab, gla_ba[l], gla_norm_g[l].reshape(1, GLA_V_W))

        on = _na(p3, _na_bias_table(na_rpb[l]))

        wq1, wq2, wkv = _mla_weights(mla_w_q_up[l], mla_w_kv_up[l])
        q_m, k_m, v_m = _mla_proj(p3, cos_t, sin_t, mla_q_norm_g[l].reshape(1, -1), mla_kv_norm_g[l].reshape(1, -1),
                                  wq1, wq2, wkv)
        om = _mla_attn(q_m, k_m, v_m, ctx_queries=False)
        om_ctx = None if last else _mla_attn(q_m, k_m, v_m, ctx_queries=True)

        xs, hp, logits_t = _merge(
            xs, p3, og, on, om, om_ctx, b_gate[l].reshape(1, -1),
            w_branch_gla[l].astype(BF16), w_branch_na[l].astype(BF16), w_branch_mla[l].astype(BF16),
            w_out[l].astype(BF16), mod, norm2_g[l].reshape(1, d), router_w[l].T,
            jnp.broadcast_to(router_b[l][:, None], (N_EXPERTS, LANE)), skip_ctx=last)

        t_tok = hp.shape[0]
        idx, wts, rank, counts = _route(logits_t)
        n_tiles = (TOP_K * t_tok) // MOE_G + N_EXPERTS
        dest, te, tv, pad_start, pad_len = _moe_plan(idx, rank, counts, n_tiles)
        xg = _dispatch(hp, dest, pad_start, pad_len, n_tiles * MOE_G)
        yg = _ffn(l, te, tv, xg, moe_w1, moe_b1, moe_w2, moe_b2)
        xs = _combine(xs, yg, dest, wts, mod, final_norm_g, ctx_first=not last, final_norm=last)
    return xs
```

```python
import functools

import numpy as np
import jax
import jax.numpy as jnp
from jax import lax
from jax.experimental import pallas as pl
from jax.experimental.pallas import tpu as pltpu

F32 = jnp.float32
BF16 = jnp.bfloat16

D_MODEL = 2048
DEPTH = 2
GRID_W = 64
CTX_LEN = 256
EPS = 1e-6
ROPE_BASE = 10000.0
NEG_INF = -1e30

GLA_HEADS = 4
GLA_DK = 64
GLA_DV = 128
GLA_GATE_RANK = 16
GLA_TAU = 16.0
GLA_CHUNK = 64
NA_HEADS = 8
NA_HEAD_DIM = 64
NA_KH = 8
NA_KW = 16
NA_SCALE = NA_HEAD_DIM ** -0.5
MLA_HEADS = 8
MLA_Q_RANK = 512
MLA_KV_RANK = 512
MLA_NOPE = 128
MLA_ROPE = 64
MLA_V = 128
MLA_SCALE = (MLA_NOPE + MLA_ROPE) ** -0.5
N_BRANCH = 3
N_EXPERTS = 32
TOP_K = 4
EXPERT_FF = D_MODEL
SWIGLU_LIMIT = 7.0
SWIGLU_ALPHA = 1.702

GLA_QK_W = GLA_HEADS * GLA_DK
GLA_V_W = GLA_HEADS * GLA_DV
NA_W = NA_HEADS * NA_HEAD_DIM
MLA_V_W = MLA_HEADS * MLA_V

LANE = 128
SUBLANE = 8
ROW_TILE = 256
VMEM_LIMIT = 56 * 1024 * 1024

COL_GATE = 0
COL_GQ = COL_GATE + N_BRANCH * D_MODEL
COL_GK = COL_GQ + GLA_QK_W
COL_GV = COL_GK + GLA_QK_W
COL_GR = COL_GV + GLA_V_W
COL_NQ = COL_GR + GLA_V_W
COL_NK = COL_NQ + NA_W
COL_NV = COL_NK + NA_W
COL_MCQ = COL_NV + NA_W
COL_MCKV = COL_MCQ + MLA_Q_RANK
COL_KR = COL_MCKV + MLA_KV_RANK
COL_KRP = COL_KR + LANE
COL_GA = COL_KRP + LANE
PROJ_TN = 1536
PROJ_W = 7 * PROJ_TN
assert COL_GA + LANE <= PROJ_W

MOE_G = 1024
MOE_SB = 256
MOE_FC = 256
DISPATCH_TT = 256
COMBINE_TT = 128
ROUTE_TT = 256
MLA_TQ = 1024


def _cparams(sem):
    return pltpu.CompilerParams(dimension_semantics=sem, vmem_limit_bytes=VMEM_LIMIT)


def _nt(a, b):
    return lax.dot_general(a, b, (((1,), (1,)), ((), ())), preferred_element_type=F32)


def _tn(a, b):
    return lax.dot_general(a, b, (((0,), (0,)), ((), ())), preferred_element_type=F32)


def _mm(a, b):
    return jnp.dot(a, b, preferred_element_type=F32)


def _sigmoid(x):
    return 1.0 / (1.0 + jnp.exp(-x))


def _split_hi_lo(x):
    hi = x.astype(BF16)
    lo = (x - hi.astype(F32)).astype(BF16)
    return hi, lo


def _ada_kernel(c_ref, w_ref, b_ref, o_ref):
    c = c_ref[...]
    s = (c * _sigmoid(c)).astype(BF16)
    o_ref[0] = _mm(s, w_ref[0].astype(BF16)) + b_ref[0]


def _ada_mod(cc, ada_w, ada_b):
    nl, d, n6 = ada_w.shape
    tn = 1024
    return pl.pallas_call(
        _ada_kernel,
        grid=(nl, n6 // tn),
        in_specs=[pl.BlockSpec((16, d), lambda l, j: (0, 0)),
                  pl.BlockSpec((1, d, tn), lambda l, j: (l, 0, j)),
                  pl.BlockSpec((1, 1, tn), lambda l, j: (l, 0, j))],
        out_specs=pl.BlockSpec((1, 16, tn), lambda l, j: (l, 0, j)),
        out_shape=jax.ShapeDtypeStruct((nl, 16, n6), F32),
        compiler_params=_cparams(("arbitrary", "arbitrary")),
        name="ada_mod",
    )(cc, ada_w, ada_b.reshape(nl, 1, n6))


def _mod_spec(part, ctx_first):
    def imap(b, j):
        row = jnp.where(j == 0, 8, b) if ctx_first else b
        return (row * 6 + part, 0, 0)
    return pl.BlockSpec((1, 1, D_MODEL), imap)


def _norm_mod_kernel(x_ref, g_ref, sc_ref, sh_ref, o_ref):
    x = x_ref[0]
    ms = jnp.mean(x * x, axis=-1, keepdims=True)
    y = x * lax.rsqrt(ms + EPS) * g_ref[...]
    o_ref[0] = (y * (1.0 + sc_ref[0]) + sh_ref[0]).astype(o_ref.dtype)


def _norm_mod(x, g, mod):
    b, rb, d = x.shape
    return pl.pallas_call(
        _norm_mod_kernel,
        grid=(b, rb // ROW_TILE),
        in_specs=[pl.BlockSpec((1, ROW_TILE, d), lambda i, j: (i, j, 0)),
                  pl.BlockSpec((1, d), lambda i, j: (0, 0)),
                  _mod_spec(1, True), _mod_spec(0, True)],
        out_specs=pl.BlockSpec((1, ROW_TILE, d), lambda i, j: (i, j, 0)),
        out_shape=jax.ShapeDtypeStruct((b, rb, d), BF16),
        compiler_params=_cparams(("arbitrary", "arbitrary")),
        name="norm_mod",
    )(x, g.reshape(1, d), mod, mod)


def _matmul_kernel(x_ref, w_ref, o_ref):
    o_ref[...] = _mm(x_ref[...], w_ref[...]).astype(o_ref.dtype)


def _matmul(x, w, tm, tn, out_dtype):
    m, k = x.shape
    n = w.shape[1]
    return pl.pallas_call(
        _matmul_kernel,
        grid=(n // tn, m // tm),
        in_specs=[pl.BlockSpec((tm, k), lambda j, i: (i, 0)),
                  pl.BlockSpec((k, tn), lambda j, i: (0, j))],
        out_specs=pl.BlockSpec((tm, tn), lambda j, i: (i, j)),
        out_shape=jax.ShapeDtypeStruct((m, n), out_dtype),
        compiler_params=_cparams(("arbitrary", "arbitrary")),
        name="in_proj",
    )(x, w)


def _gla_kernel(q_ref, k_ref, v_ref, r_ref, ab_ref, waf_ref, wab_ref, ba_ref, g_ref, o_ref,
                laf_ref, lab_ref, of_ref, ob_ref, st_ref, *, n_ctx_chunks, n_chunks):
    c = GLA_CHUNK
    ab = ab_ref[0]
    zf = _mm(ab, waf_ref[...]) + ba_ref[0:1, :]
    zb = _mm(ab, wab_ref[...]) + ba_ref[1:2, :]
    laf_ref[...] = (jnp.minimum(zf, 0.0) - jnp.log(1.0 + jnp.exp(-jnp.abs(zf)))) * (1.0 / GLA_TAU)
    lab_ref[...] = (jnp.minimum(zb, 0.0) - jnp.log(1.0 + jnp.exp(-jnp.abs(zb)))) * (1.0 / GLA_TAU)
    st_ref[...] = jnp.zeros_like(st_ref)

    ri = lax.broadcasted_iota(jnp.int32, (c, c), 0)
    ci = lax.broadcasted_iota(jnp.int32, (c, c), 1)
    tri = [(ri >= ci), (ri <= ci)]
    tri_bf = [t.astype(F32).astype(BF16) for t in tri]
    tri4 = [jnp.concatenate([t] * GLA_HEADS, axis=0) for t in tri]
    lane_q = lax.broadcasted_iota(jnp.int32, (c, GLA_QK_W), 1) // GLA_DK
    lane_v = lax.broadcasted_iota(jnp.int32, (c, GLA_V_W), 1) // GLA_DV
    st_row_h = lax.broadcasted_iota(jnp.int32, (GLA_V_W, GLA_QK_W), 0) // GLA_DV
    st_col_h = lax.broadcasted_iota(jnp.int32, (GLA_V_W, GLA_QK_W), 1) // GLA_DK
    st_mask = st_row_h == st_col_h

    def one_dir(d, chunk, la_ref, out_ref):
        r0 = pl.multiple_of(chunk * c, c)
        la = la_ref[pl.ds(r0, c), :]
        hi, lo = _split_hi_lo(la)
        bc = _mm(tri_bf[d], hi) + _mm(tri_bf[d], lo)
        btot = bc[c - 1:c, :] if d == 0 else bc[0:1, :]
        q = q_ref[0, pl.ds(r0, c), :].astype(F32) * (GLA_DK ** -0.5)
        k = k_ref[0, pl.ds(r0, c), :].astype(F32)
        v = v_ref[0, pl.ds(r0, c), :]
        qd = (q * jnp.exp(bc)).astype(BF16)
        ki = (k * jnp.exp(-bc)).astype(BF16)
        kd = (k * jnp.exp(btot - bc)).astype(BF16)
        zero = jnp.zeros_like(qd)
        qs = jnp.concatenate([jnp.where(lane_q == h, qd, zero) for h in range(GLA_HEADS)], axis=0)
        att = _nt(qs, ki)
        att = jnp.where(tri4[d], att, 0.0).astype(BF16)
        rr = _mm(att, v)
        o = _nt(qd, st_ref[d].astype(BF16))
        for h in range(GLA_HEADS):
            o = o + jnp.where(lane_v == h, rr[h * c:(h + 1) * c, :], 0.0)
        out_ref[pl.ds(r0, c), :] = o
        upd = _tn(v, kd)
        st_ref[d] = jnp.where(st_mask, st_ref[d] * jnp.exp(btot) + upd, 0.0)

    def step(i, carry):
        one_dir(0, i, laf_ref, of_ref)
        cb = jnp.where(i < n_ctx_chunks, n_ctx_chunks - 1 - i, n_chunks + n_ctx_chunks - 1 - i)
        one_dir(1, cb, lab_ref, ob_ref)
        return carry

    lax.fori_loop(0, n_chunks, step, 0, unroll=2)

    def epilogue(j, carry):
        r0 = pl.multiple_of(j * ROW_TILE, ROW_TILE)
        o = of_ref[pl.ds(r0, ROW_TILE), :] + ob_ref[pl.ds(r0, ROW_TILE), :]
        r = r_ref[0, pl.ds(r0, ROW_TILE), :].astype(F32)
        gate = r * _sigmoid(r)
        for h in range(GLA_HEADS):
            sl = slice(h * GLA_DV, (h + 1) * GLA_DV)
            oh = o[:, sl]
            ms = jnp.mean(oh * oh, axis=-1, keepdims=True)
            y = oh * lax.rsqrt(ms + EPS) * g_ref[:, sl]
            o_ref[0, pl.ds(r0, ROW_TILE), sl] = (y * gate[:, sl]).astype(o_ref.dtype)
        return carry

    lax.fori_loop(0, (n_chunks * c) // ROW_TILE, epilogue, 0)


def _gla(p3, waf, wab, ba, g):
    b, rb, _ = p3.shape
    n_chunks = rb // GLA_CHUNK
    kern = functools.partial(_gla_kernel, n_ctx_chunks=CTX_LEN // GLA_CHUNK, n_chunks=n_chunks)

    def col(width, off):
        return pl.BlockSpec((1, rb, width), lambda i: (i, 0, off // width))

    return pl.pallas_call(
        kern,
        grid=(b,),
        in_specs=[col(GLA_QK_W, COL_GQ), col(GLA_QK_W, COL_GK), col(GLA_V_W, COL_GV), col(GLA_V_W, COL_GR),
                  col(LANE, COL_GA),
                  pl.BlockSpec((LANE, GLA_QK_W), lambda i: (0, 0)),
                  pl.BlockSpec((LANE, GLA_QK_W), lambda i: (0, 0)),
                  pl.BlockSpec((2, GLA_QK_W), lambda i: (0, 0)),
                  pl.BlockSpec((1, GLA_V_W), lambda i: (0, 0))],
        out_specs=pl.BlockSpec((1, rb, GLA_V_W), lambda i: (i, 0, 0)),
        out_shape=jax.ShapeDtypeStruct((b, rb, GLA_V_W), BF16),
        scratch_shapes=[pltpu.VMEM((rb, GLA_QK_W), F32), pltpu.VMEM((rb, GLA_QK_W), F32),
                        pltpu.VMEM((rb, GLA_V_W), F32), pltpu.VMEM((rb, GLA_V_W), F32),
                        pltpu.VMEM((2, GLA_V_W, GLA_QK_W), F32)],
        compiler_params=_cparams(("arbitrary",)),
        name="gla",
    )(p3, p3, p3, p3, p3, waf, wab, ba, g)


def _softmax_pv(parts):
    m = parts[0][0].max(axis=-1, keepdims=True)
    for s, _ in parts[1:]:
        m = jnp.maximum(m, s.max(axis=-1, keepdims=True))
    acc = None
    den = None
    for s, v in parts:
        e = jnp.exp(s - m)
        l = e.sum(axis=-1, keepdims=True)
        pv = _mm(e.astype(BF16), v)
        acc = pv if acc is None else acc + pv
        den = l if den is None else den + l
    return acc / den


NA_QR = 4
NA_KR = NA_QR + NA_KH


def _na_block_rows(n_rows):
    r0s = (0, 2 * NA_QR, n_rows - NA_QR)
    return [(r0, int(np.clip(r0 - NA_KH // 2, 0, n_rows - NA_KR))) for r0 in r0s]


def _na_kernel(q_ref, k_ref, v_ref, bias_ref, o_ref, *, n_rows):
    nq = NA_QR * GRID_W
    nk = NA_KR * GRID_W
    n_blocks = n_rows // NA_QR
    lane = lax.broadcasted_iota(jnp.int32, (nq, LANE), 1)
    head_mask = [lane < NA_HEAD_DIM, lane >= NA_HEAD_DIM]
    kc = k_ref[0, 0:CTX_LEN, :]
    vc = v_ref[0, 0:CTX_LEN, :]

    def block(bi, carry):
        r0 = bi * NA_QR
        ks = jnp.clip(r0 - NA_KH // 2, 0, n_rows - NA_KR)
        var = jnp.where(bi == 0, 0, jnp.where(bi == n_blocks - 1, 2, 1))
        q0 = pl.multiple_of(CTX_LEN + r0 * GRID_W, GRID_W)
        k0 = pl.multiple_of(CTX_LEN + ks * GRID_W, GRID_W)
        q = q_ref[0, pl.ds(q0, nq), :]
        kl = k_ref[0, pl.ds(k0, nk), :]
        vl = v_ref[0, pl.ds(k0, nk), :]
        outs = []
        for hh in range(2):
            qm = jnp.where(head_mask[hh], q, jnp.zeros_like(q))
            s_loc = _nt(qm, kl) * NA_SCALE + bias_ref[hh, var]
            s_ctx = _nt(qm, kc) * NA_SCALE
            outs.append(_softmax_pv([(s_loc, vl), (s_ctx, vc)]))
        o_ref[0, pl.ds(q0, nq), :] = jnp.where(head_mask[0], outs[0], outs[1]).astype(o_ref.dtype)
        return carry

    lax.fori_loop(0, n_blocks, block, 0, unroll=2)

    qc = q_ref[0, 0:CTX_LEN, :]
    lane_c = lax.broadcasted_iota(jnp.int32, (CTX_LEN, LANE), 1)
    outs = []
    for hh in range(2):
        msk = (lane_c < NA_HEAD_DIM) if hh == 0 else (lane_c >= NA_HEAD_DIM)
        qm = jnp.where(msk, qc, jnp.zeros_like(qc))
        outs.append(_softmax_pv([(_nt(qm, kc) * NA_SCALE, vc)]))
    o_ref[0, 0:CTX_LEN, :] = jnp.where(lane_c < NA_HEAD_DIM, outs[0], outs[1]).astype(o_ref.dtype)


def _na_bias_table(rpb, n_rows):
    qcol = np.arange(GRID_W)[:, None]
    kcol = np.arange(GRID_W)[None, :]
    wstart = np.clip(qcol - NA_KW // 2, 0, GRID_W - NA_KW)
    in_win = (kcol >= wstart) & (kcol < wstart + NA_KW)
    dc = kcol - qcol + NA_KW - 1
    col_hot = ((dc[:, :, None] == np.arange(2 * NA_KW - 1)) & in_win[:, :, None]).astype(np.float32)
    geo = _na_block_rows(n_rows)
    row_hot = np.zeros((len(geo), NA_QR, NA_KR, 2 * NA_KH - 1), np.float32)
    for v, (r0, ks) in enumerate(geo):
        for a in range(NA_QR):
            rs = int(np.clip(r0 + a - NA_KH // 2, 0, n_rows - NA_KH))
            for i in range(NA_KR):
                if rs <= ks + i < rs + NA_KH:
                    row_hot[v, a, i, ks + i - (r0 + a) + NA_KH - 1] = 1.0
    t = jnp.einsum('hrc,vair,qkc->hvaqik', rpb.astype(F32), jnp.asarray(row_hot), jnp.asarray(col_hot),
                   precision=lax.Precision.HIGHEST)
    inside = (row_hot.sum(-1) > 0)[:, :, None, :, None] & in_win[None, None, :, None, :]
    t = jnp.where(inside[None], t, NEG_INF)
    return t.reshape(NA_HEADS, len(geo), NA_QR * GRID_W, NA_KR * GRID_W)


def _na(p3, bias):
    b, rb, _ = p3.shape
    n_rows = (rb - CTX_LEN) // GRID_W
    kern = functools.partial(_na_kernel, n_rows=n_rows)

    def col(off):
        return pl.BlockSpec((1, rb, LANE), lambda p, i: (i, 0, off // LANE + p))

    return pl.pallas_call(
        kern,
        grid=(NA_HEADS // 2, b),
        in_specs=[col(COL_NQ), col(COL_NK), col(COL_NV),
                  pl.BlockSpec((2,) + bias.shape[1:], lambda p, i: (p, 0, 0, 0))],
        out_specs=pl.BlockSpec((1, rb, LANE), lambda p, i: (i, 0, p)),
        out_shape=jax.ShapeDtypeStruct((b, rb, NA_W), BF16),
        compiler_params=_cparams(("arbitrary", "arbitrary")),
        name="na",
    )(p3, p3, p3, bias)


def _mla_proj_kernel(cq_ref, ckv_ref, kr_ref, krp_ref, cos_ref, sin_ref, gq_ref, gkv_ref,
                     wq_ref, wq2_ref, wkv_ref, q_ref, k_ref, v_ref):
    def norm(x, g):
        x = x.astype(F32)
        ms = jnp.mean(x * x, axis=-1, keepdims=True)
        return (x * lax.rsqrt(ms + EPS) * g).astype(BF16)

    cos = cos_ref[...]
    sin = sin_ref[...]
    nq = norm(cq_ref[0], gq_ref[...])
    yq = _mm(nq, wq_ref[...])
    yq2 = _mm(nq, wq2_ref[...])
    nkv = norm(ckv_ref[0], gkv_ref[...])
    ykv = _mm(nkv, wkv_ref[...])
    k_rot = (kr_ref[0].astype(F32) * cos + krp_ref[0].astype(F32) * sin).astype(BF16)
    for h in range(MLA_HEADS):
        a = 2 * h * LANE
        q_ref[0, :, a:a + LANE] = (yq[:, a:a + LANE] * MLA_SCALE).astype(BF16)
        rot = yq[:, a + LANE:a + 2 * LANE] * cos + yq2[:, h * LANE:(h + 1) * LANE] * sin
        q_ref[0, :, a + LANE:a + 2 * LANE] = (rot * MLA_SCALE).astype(BF16)
        k_ref[0, :, a:a + LANE] = ykv[:, h * LANE:(h + 1) * LANE].astype(BF16)
        k_ref[0, :, a + LANE:a + 2 * LANE] = k_rot
        v0 = MLA_HEADS * MLA_NOPE + h * MLA_V
        v_ref[0, :, a:a + LANE] = ykv[:, v0:v0 + MLA_V].astype(BF16)
        v_ref[0, :, a + LANE:a + 2 * LANE] = jnp.ones((ykv.shape[0], LANE), BF16)


def _mla_proj(p3, cos_t, sin_t, gq, gkv, wq, wq2, wkv):
    b, rb, _ = p3.shape
    tm = ROW_TILE

    def col(width, off):
        return pl.BlockSpec((1, tm, width), lambda i, j: (i, j, off // width))

    def full(a):
        return pl.BlockSpec(a.shape, lambda i, j: (0, 0))

    hw = MLA_HEADS * 2 * LANE
    nj = rb // tm
    return pl.pallas_call(
        _mla_proj_kernel,
        grid=(b, nj),
        in_specs=[col(MLA_Q_RANK, COL_MCQ), col(MLA_KV_RANK, COL_MCKV), col(LANE, COL_KR), col(LANE, COL_KRP),
                  pl.BlockSpec((tm, LANE), lambda i, j: (j, 0)), pl.BlockSpec((tm, LANE), lambda i, j: (j, 0)),
                  full(gq), full(gkv), full(wq), full(wq2), full(wkv)],
        out_specs=[pl.BlockSpec((1, tm, hw), lambda i, j: (i, (j + nj - 1) % nj, 0)),
                   pl.BlockSpec((1, tm, hw), lambda i, j: (i, (j + nj - 1) % nj, 0)),
                   pl.BlockSpec((1, tm, hw), lambda i, j: (i, (j + nj - 1) % nj, 0))],
        out_shape=[jax.ShapeDtypeStruct((b, rb, hw), BF16), jax.ShapeDtypeStruct((b, rb, hw), BF16),
                   jax.ShapeDtypeStruct((b, rb, hw), BF16)],
        compiler_params=_cparams(("arbitrary", "arbitrary")),
        name="mla_proj",
    )(p3, p3, p3, p3, cos_t, sin_t, gq, gkv, wq, wq2, wkv)


def _mla_attn_kernel(q_ref, k_ref, v_ref, o_ref):
    k = k_ref[0]
    v = v_ref[0]
    for r0 in range(0, q_ref.shape[1], ROW_TILE):
        s = _nt(q_ref[0, r0:r0 + ROW_TILE, :], k)
        e = jnp.exp((s - s.max(axis=-1, keepdims=True)).astype(BF16))
        pv = _mm(e, v)
        o_ref[0, r0:r0 + ROW_TILE, :] = (pv[:, :MLA_V] / pv[:, MLA_V:]).astype(o_ref.dtype)


def _mla_attn(q, k, v, ctx_queries):
    b, rb, _ = q.shape
    n_lat = rb - CTX_LEN
    if ctx_queries:
        tq, nq, n_keys = CTX_LEN, 1, CTX_LEN
        q_off = kv_blk = n_lat // CTX_LEN
    else:
        tq = MLA_TQ if n_lat % MLA_TQ == 0 else ROW_TILE
        nq, n_keys, q_off, kv_blk = n_lat // tq, rb, 0, 0
    return pl.pallas_call(
        _mla_attn_kernel,
        grid=(b, MLA_HEADS, nq),
        in_specs=[pl.BlockSpec((1, tq, 2 * LANE), lambda i, h, j: (i, j + q_off, h)),
                  pl.BlockSpec((1, n_keys, 2 * LANE), lambda i, h, j: (i, kv_blk, h)),
                  pl.BlockSpec((1, n_keys, 2 * LANE), lambda i, h, j: (i, kv_blk, h))],
        out_specs=pl.BlockSpec((1, tq, MLA_V), lambda i, h, j: (i, j, h)),
        out_shape=jax.ShapeDtypeStruct((b, nq * tq, MLA_V_W), BF16),
        compiler_params=_cparams(("arbitrary", "arbitrary", "arbitrary")),
        name="mla_attn_ctx" if ctx_queries else "mla_attn",
    )(q, k, v)


def _merge_kernel(x_ref, oa_ref, ob_ref, oc_ref, occ_ref, ga_ref, gb_ref, gc_ref, bg_ref,
                  wa_ref, wb_ref, wc_ref, wo_ref, gt1_ref, sc2_ref, sh2_ref, g2_ref, rw_ref, rb_ref,
                  xo_ref, h_ref, lg_ref, *, ctx_tile):
    d = D_MODEL

    wh, wl = _split_hi_lo(rw_ref[...])
    tm = x_ref.shape[1]
    sub = tm // 2
    for r0 in range(0, tm, sub):
        rows = slice(r0, r0 + sub)

        def gate(g_ref, k):
            return _sigmoid(g_ref[0, rows, :].astype(F32) + bg_ref[:, k * d:(k + 1) * d])

        oc = oc_ref[0, rows, :]
        if ctx_tile:
            oc = jnp.where(pl.program_id(1) == 0, occ_ref[0, rows, :], oc)
        m = gate(ga_ref, 0) * _mm(oa_ref[0, rows, :], wa_ref[...])
        m = m + gate(gb_ref, 1) * _mm(ob_ref[0, rows, :], wb_ref[...])
        m = m + gate(gc_ref, 2) * _mm(oc, wc_ref[...])
        y = _mm(m.astype(BF16), wo_ref[...])
        x = x_ref[0, rows, :] + gt1_ref[0] * y
        xo_ref[0, rows, :] = x
        ms = jnp.mean(x * x, axis=-1, keepdims=True)
        h = x * lax.rsqrt(ms + EPS) * g2_ref[...]
        h = h * (1.0 + sc2_ref[0]) + sh2_ref[0]
        h_ref[rows, :] = h
        hh, hl = _split_hi_lo(h)
        lg_ref[:, rows] = _nt(wh, hh) + _nt(wh, hl) + _nt(wl, hh) + rb_ref[:, 0:1]


def _merge(x, p3, og, on, om, om_ctx, bg, wa, wb, wc, wo, mod, g2, rwt, rbias, skip_ctx):
    b, rb, d = x.shape
    tm = ROW_TILE
    jo = 1 if skip_ctx else 0
    nj = rb // tm - jo
    rows_out = nj * tm

    def rows(width, cblk=0):
        return pl.BlockSpec((1, tm, width), lambda i, j: (i, j + jo, cblk))

    om_spec = pl.BlockSpec((1, tm, MLA_V_W), lambda i, j: (i, jnp.maximum(j + jo - 1, 0), 0))
    omc_spec = pl.BlockSpec((1, tm, MLA_V_W), lambda i, j: (i, 0, 0))
    kern = functools.partial(_merge_kernel, ctx_tile=not skip_ctx)

    def full(a):
        return pl.BlockSpec(a.shape, lambda i, j: (0,) * a.ndim, pipeline_mode=pl.Buffered(1))

    def mod_spec(part):
        def imap(i, j):
            row = i if skip_ctx else jnp.where(j == 0, 8, i)
            return (row * 6 + part, 0, 0)
        return pl.BlockSpec((1, 1, d), imap)

    return pl.pallas_call(
        kern,
        grid=(b, nj),
        in_specs=[rows(d), rows(GLA_V_W), rows(NA_W), om_spec, omc_spec,
                  rows(d, 0), rows(d, 1), rows(d, 2), full(bg),
                  full(wa), full(wb), full(wc), full(wo),
                  mod_spec(2), mod_spec(4), mod_spec(3), full(g2), full(rwt), full(rbias)],
        out_specs=[pl.BlockSpec((1, tm, d), lambda i, j: (i, j, 0)),
                   pl.BlockSpec((tm, d), lambda i, j: (i * nj + j, 0)),
                   pl.BlockSpec((N_EXPERTS, tm), lambda i, j: (0, i * nj + j))],
        out_shape=[jax.ShapeDtypeStruct((b, rows_out, d), F32),
                   jax.ShapeDtypeStruct((b * rows_out, d), F32),
                   jax.ShapeDtypeStruct((N_EXPERTS, b * rows_out), F32)],
        compiler_params=_cparams(("arbitrary", "arbitrary")),
        name="merge",
    )(x, og, on, om, om if skip_ctx else om_ctx, p3, p3, p3, bg, wa, wb, wc, wo, mod, mod, mod, g2, rwt, rbias)


def _route_kernel(l_ref, idx_ref, w_ref, rank_ref, cnt_ref, carry_ref):
    i = pl.program_id(0)
    tt = l_ref.shape[1]

    @pl.when(i == 0)
    def _():
        carry_ref[...] = jnp.zeros_like(carry_ref)

    l = l_ref[...]
    eio = lax.broadcasted_iota(jnp.int32, (N_EXPERTS, tt), 0)
    vals, idxs = [], []
    for _ in range(TOP_K):
        m = l.max(axis=0, keepdims=True)
        ik = jnp.min(jnp.where(l == m, eio, N_EXPERTS), axis=0, keepdims=True)
        vals.append(m)
        idxs.append(ik)
        l = jnp.where(eio == ik, -jnp.inf, l)
    es = [jnp.exp(v - vals[0]) for v in vals]
    den = es[0] + es[1] + es[2] + es[3]
    sel = jnp.zeros((N_EXPERTS, tt), F32)
    for ik in idxs:
        sel = sel + (eio == ik).astype(F32)
    si = lax.broadcasted_iota(jnp.int32, (tt, tt), 0)
    ti = lax.broadcasted_iota(jnp.int32, (tt, tt), 1)
    before = (si < ti).astype(F32).astype(BF16)
    rank_full = _mm(sel.astype(BF16), before) + carry_ref[:, 0:1]
    for k in range(TOP_K):
        idx_ref[k:k + 1, :] = idxs[k]
        w_ref[k:k + 1, :] = es[k] / den
        rk = jnp.sum(jnp.where(eio == idxs[k], rank_full, 0.0), axis=0, keepdims=True)
        rank_ref[k:k + 1, :] = rk.astype(jnp.int32)
    carry_ref[...] = carry_ref[...] + jnp.sum(sel, axis=1, keepdims=True)
    cnt_ref[...] = carry_ref[...]


def _route(logits_t):
    ne, t = logits_t.shape
    tt = ROUTE_TT
    spec4 = pl.BlockSpec((TOP_K, tt), lambda i: (0, i))
    return pl.pallas_call(
        _route_kernel,
        grid=(t // tt,),
        in_specs=[pl.BlockSpec((ne, tt), lambda i: (0, i))],
        out_specs=[spec4, spec4, spec4, pl.BlockSpec((ne, LANE), lambda i: (0, 0))],
        out_shape=[jax.ShapeDtypeStruct((TOP_K, t), jnp.int32), jax.ShapeDtypeStruct((TOP_K, t), F32),
                   jax.ShapeDtypeStruct((TOP_K, t), jnp.int32), jax.ShapeDtypeStruct((ne, LANE), F32)],
        scratch_shapes=[pltpu.VMEM((ne, LANE), F32)],
        compiler_params=_cparams(("arbitrary",)),
        name="route",
    )(logits_t)


def _pad_fill(ps_ref, pn_ref, zero_ref, xg_ref, sem, wait):
    def copy(pos, rows):
        cp = pltpu.make_async_copy(zero_ref.at[pl.ds(0, rows), :], xg_ref.at[pl.ds(pos, rows), :], sem)
        cp.wait() if wait else cp.start()

    def per_expert(e, carry):
        pos = ps_ref[e]
        head = (-pos) & (SUBLANE - 1)
        for r in range(SUBLANE - 1):
            @pl.when(r < head)
            def _(r=r):
                copy(pos + r, 1)

        pos = pos + head
        n = pn_ref[e] - head
        bit = MOE_G // 2
        while bit >= SUBLANE:
            on = (n & bit) != 0

            @pl.when(on)
            def _(pos=pos, bit=bit):
                copy(pl.multiple_of(pos, SUBLANE), bit)

            pos = pos + jnp.where(on, bit, 0)
            bit //= 2
        return carry

    lax.fori_loop(0, N_EXPERTS, per_expert, 0)

    zr = zero_ref.shape[0]

    def tail(i, carry):
        pos = pl.multiple_of(ps_ref[N_EXPERTS] + i * zr, zr)
        cp = pltpu.make_async_copy(zero_ref, xg_ref.at[pl.ds(pos, zr), :], sem)
        cp.wait() if wait else cp.start()
        return carry

    lax.fori_loop(0, pn_ref[N_EXPERTS], tail, 0)


def _dispatch_kernel(ps_ref, pn_ref, dest_ref, h_ref, xg_ref, zero_ref, sem, zsem):
    tt = h_ref.shape[0]

    @pl.when(pl.program_id(0) == 0)
    def _():
        zero_ref[...] = jnp.zeros_like(zero_ref)
        _pad_fill(ps_ref, pn_ref, zero_ref, xg_ref, zsem, wait=False)

    def issue(t, carry):
        for k in range(TOP_K):
            pltpu.make_async_copy(h_ref.at[pl.ds(t, 1), :], xg_ref.at[pl.ds(dest_ref[0, k, t], 1), :], sem).start()
        return carry

    lax.fori_loop(0, tt, issue, 0)
    for k in range(TOP_K):
        pltpu.make_async_copy(h_ref, xg_ref.at[pl.ds(0, tt), :], sem).wait()

    @pl.when(pl.program_id(0) == 0)
    def _():
        _pad_fill(ps_ref, pn_ref, zero_ref, xg_ref, zsem, wait=True)


def _dispatch(hp, dest, pad_start, pad_len, n_slots):
    t, w = hp.shape
    tt = DISPATCH_TT
    dest3 = dest.reshape(TOP_K, t // tt, tt).transpose(1, 0, 2)
    grid_spec = pltpu.PrefetchScalarGridSpec(
        num_scalar_prefetch=2,
        grid=(t // tt,),
        in_specs=[pl.BlockSpec((1, TOP_K, tt), lambda i, ps, pn: (i, 0, 0), memory_space=pltpu.SMEM),
                  pl.BlockSpec((tt, w), lambda i, ps, pn: (i, 0))],
        out_specs=pl.BlockSpec(memory_space=pl.ANY),
        scratch_shapes=[pltpu.VMEM((MOE_G // 2, w), hp.dtype), pltpu.SemaphoreType.DMA(()),
                        pltpu.SemaphoreType.DMA(())],
    )
    return pl.pallas_call(
        _dispatch_kernel,
        grid_spec=grid_spec,
        out_shape=jax.ShapeDtypeStruct((n_slots, w), hp.dtype),
        compiler_params=_cparams(("arbitrary",)),
        name="dispatch",
    )(pad_start, pad_len, dest3, hp)


def _ffn_kernel(te_ref, tv_ref, x_ref, w1g_ref, w1l_ref, b1g_ref, b1l_ref, w2_ref, b2_ref, y_ref,
                xb_ref, *, n_fc, n_tiles):
    i = pl.program_id(0)
    j = pl.program_id(1)
    valid = tv_ref[i]
    g = y_ref.shape[0]
    chunk = x_ref.shape[0]

    @pl.when(i < n_tiles)
    def _():
        xb_ref[i % 2, pl.ds(pl.multiple_of(j * chunk, chunk), chunk), :] = x_ref[...].astype(BF16)

    cur = (i + 1) % 2

    @pl.when((valid > 0) & (j == 0))
    def _():
        y_ref[...] = jnp.broadcast_to(b2_ref[0, 0], y_ref.shape)

    for nr in range(MOE_SB, g + 1, MOE_SB):
        @pl.when((valid > nr - MOE_SB) & (valid <= nr))
        def _(nr=nr):
            wg = w1g_ref[0, 0].astype(BF16)
            wl = w1l_ref[0, 0].astype(BF16)
            w2 = w2_ref[0, 0].astype(BF16)
            for r0 in range(0, nr, MOE_SB):
                x = xb_ref[cur, r0:r0 + MOE_SB, :]
                ug = _mm(x, wg) + b1g_ref[0, 0]
                ul = _mm(x, wl) + b1l_ref[0, 0]
                xg = jnp.minimum(ug, SWIGLU_LIMIT)
                xl = jnp.clip(ul, -SWIGLU_LIMIT, SWIGLU_LIMIT)
                act = xg * _sigmoid(SWIGLU_ALPHA * xg) * (xl + 1.0)
                y_ref[r0:r0 + MOE_SB, :] = y_ref[r0:r0 + MOE_SB, :] + _mm(act.astype(BF16), w2)

    @pl.when((j == n_fc - 1) & (valid == 0))
    def _():
        y_ref[...] = jnp.zeros_like(y_ref)


def _ffn(layer, tile_expert, tile_valid, xg, w1, b1, w2, b2):
    n_slots, d = xg.shape
    nl, ne, _, ff2 = w1.shape
    ff = ff2 // 2
    n_fc = ff // MOE_FC
    n_tiles = n_slots // MOE_G
    chunk = MOE_G // n_fc
    kern = functools.partial(_ffn_kernel, n_fc=n_fc, n_tiles=n_tiles)
    tile_expert = jnp.concatenate([tile_expert[:1], tile_expert])
    tile_valid = jnp.concatenate([jnp.zeros((1,), tile_valid.dtype), tile_valid])

    def jj(j, tv, i):
        return jnp.where(tv[i] > 0, j, n_fc - 1)

    grid_spec = pltpu.PrefetchScalarGridSpec(
        num_scalar_prefetch=2,
        grid=(n_tiles + 1, n_fc),
        in_specs=[pl.BlockSpec((chunk, d), lambda i, j, te, tv: (jnp.minimum(i, n_tiles - 1) * n_fc + j, 0)),
                  pl.BlockSpec((1, 1, d, MOE_FC), lambda i, j, te, tv: (layer, te[i], 0, jj(j, tv, i))),
                  pl.BlockSpec((1, 1, d, MOE_FC), lambda i, j, te, tv: (layer, te[i], 0, jj(j, tv, i) + n_fc)),
                  pl.BlockSpec((1, 1, 1, MOE_FC), lambda i, j, te, tv: (layer, te[i], 0, jj(j, tv, i))),
                  pl.BlockSpec((1, 1, 1, MOE_FC), lambda i, j, te, tv: (layer, te[i], 0, jj(j, tv, i) + n_fc)),
                  pl.BlockSpec((1, 1, MOE_FC, d), lambda i, j, te, tv: (layer, te[i], jj(j, tv, i), 0)),
                  pl.BlockSpec((1, 1, 1, d), lambda i, j, te, tv: (layer, te[i], 0, 0))],
        out_specs=pl.BlockSpec((MOE_G, d), lambda i, j, te, tv: (jnp.maximum(i - 1, 0), 0)),
        scratch_shapes=[pltpu.VMEM((2, MOE_G, d), BF16)],
    )
    return pl.pallas_call(
        kern,
        grid_spec=grid_spec,
        out_shape=jax.ShapeDtypeStruct((n_slots, d), F32),
        compiler_params=_cparams(("arbitrary", "arbitrary")),
        name="moe_ffn",
    )(tile_expert, tile_valid, xg, w1, w1, b1.reshape(nl, ne, 1, ff2), b1.reshape(nl, ne, 1, ff2), w2,
      b2.reshape(nl, ne, 1, d))


def _combine_kernel(dest_ref, destn_ref, x_ref, wt_ref, gt2_ref, gn_ref, sc_ref, sh_ref, yg_ref, o_ref, *rest,
                    final_norm):
    h_ref = None if final_norm else rest[0]
    buf_ref, sem = rest[-2:]
    i = pl.program_id(0)
    n = pl.num_programs(0)
    tt = x_ref.shape[1]
    slot = i % 2

    def issue(d_ref, s):
        def body(t, carry):
            for k in range(TOP_K):
                pltpu.make_async_copy(yg_ref.at[pl.ds(d_ref[0, k, t], 1), :],
                                      buf_ref.at[s, k, pl.ds(t, 1), :], sem.at[s]).start()
            return carry
        lax.fori_loop(0, tt, body, 0)

    @pl.when(i == 0)
    def _():
        issue(dest_ref, 0)

    @pl.when(i + 1 < n)
    def _():
        issue(destn_ref, 1 - slot)

    for k in range(TOP_K):
        pltpu.make_async_copy(yg_ref.at[pl.ds(0, tt), :], buf_ref.at[slot, k], sem.at[slot]).wait()

    wt = wt_ref[...]
    y = buf_ref[slot, 0] * wt[:, 0:1]
    for k in range(1, TOP_K):
        y = y + buf_ref[slot, k] * wt[:, k:k + 1]
    x = x_ref[0] + gt2_ref[0] * y
    ms = jnp.mean(x * x, axis=-1, keepdims=True)
    xn = x * lax.rsqrt(ms + EPS) * gn_ref[...]
    if final_norm:
        o_ref[0] = xn
    else:
        o_ref[0] = x
        h_ref[0] = (xn * (1.0 + sc_ref[0]) + sh_ref[0]).astype(h_ref.dtype)


def _combine(x, yg, dest, wts, mod, gn, mod_next, ctx_first, final_norm):
    b, rows, d = x.shape
    tt = COMBINE_TT
    nj = rows // tt
    t = b * rows
    nt = t // tt
    dest3 = dest.reshape(TOP_K, nt, tt).transpose(1, 0, 2)
    wt = wts.T
    kern = functools.partial(_combine_kernel, final_norm=final_norm)

    def mod_spec(part):
        def imap(i):
            bi = i // nj
            row = jnp.where((i % nj) * tt < CTX_LEN, 8, bi) if ctx_first else bi
            return (row * 6 + part, 0, 0)
        return pl.BlockSpec((1, 1, d), imap)

    row_spec = pl.BlockSpec((1, tt, d), lambda i: (i // nj, i % nj, 0))
    out_specs = [row_spec]
    out_shape = [jax.ShapeDtypeStruct((b, rows, d), F32)]
    if not final_norm:
        out_specs.append(row_spec)
        out_shape.append(jax.ShapeDtypeStruct((b, rows, d), BF16))

    return pl.pallas_call(
        kern,
        grid=(nt,),
        in_specs=[pl.BlockSpec((1, TOP_K, tt), lambda i: (i, 0, 0), memory_space=pltpu.SMEM),
                  pl.BlockSpec((1, TOP_K, tt), lambda i: (jnp.minimum(i + 1, nt - 1), 0, 0),
                               memory_space=pltpu.SMEM),
                  row_spec,
                  pl.BlockSpec((tt, TOP_K), lambda i: (i, 0)),
                  mod_spec(5),
                  pl.BlockSpec((1, d), lambda i: (0, 0)),
                  mod_spec(1), mod_spec(0),
                  pl.BlockSpec(memory_space=pl.ANY)],
        out_specs=out_specs,
        out_shape=out_shape,
        scratch_shapes=[pltpu.VMEM((2, TOP_K, tt, d), yg.dtype), pltpu.SemaphoreType.DMA((2,))],
        compiler_params=_cparams(("arbitrary",)),
        name="combine",
    )(dest3, dest3, x, wt, mod, gn.reshape(1, d), mod_next, mod_next, yg)


def _proj_weight(w_in):
    d = w_in.shape[0]
    splits = (GLA_QK_W, GLA_QK_W, GLA_V_W, GLA_V_W, GLA_GATE_RANK, GLA_GATE_RANK,
              NA_W, NA_W, NA_W, MLA_Q_RANK, MLA_KV_RANK, MLA_ROPE, N_BRANCH * D_MODEL)
    pts = np.cumsum((0,) + splits)
    (gq, gk, gv, gr, gaf, gab, nq, nk, nv, mcq, mckv, mkr, gate) = [w_in[:, pts[i]:pts[i + 1]] for i in range(13)]
    q16 = MLA_ROPE // 4
    mkrp = jnp.concatenate([mkr[:, q16:2 * q16], mkr[:, :q16], mkr[:, 3 * q16:], mkr[:, 2 * q16:3 * q16]], axis=1)
    z = lambda n: jnp.zeros((d, n), w_in.dtype)
    cols = [gate, gq, gk, gv, gr, nq, nk, nv, mcq, mckv,
            mkr, z(LANE - MLA_ROPE), mkrp, z(LANE - MLA_ROPE),
            gaf, gab, z(LANE - 2 * GLA_GATE_RANK)]
    w = jnp.concatenate(cols, axis=1)
    w = jnp.concatenate([w, z(PROJ_W - w.shape[1])], axis=1)
    return w.astype(BF16)


def _rope_tables(rb):
    n = rb - CTX_LEN
    t = np.arange(n)
    nf = MLA_ROPE // 4
    freqs = ROPE_BASE ** (-np.arange(nf, dtype=np.float64) / nf)
    cos = np.zeros((rb, LANE), np.float32)
    sin = np.zeros((rb, LANE), np.float32)
    cos[:CTX_LEN, :MLA_ROPE] = 1.0
    for a, pos in enumerate((t // GRID_W, t % GRID_W)):
        ang = (pos.astype(np.float32)[:, None] * freqs.astype(np.float32)[None, :]).astype(np.float32)
        c, s = np.cos(ang), np.sin(ang)
        base = a * 2 * nf
        cos[CTX_LEN:, base:base + nf] = c
        cos[CTX_LEN:, base + nf:base + 2 * nf] = c
        sin[CTX_LEN:, base:base + nf] = -s
        sin[CTX_LEN:, base + nf:base + 2 * nf] = s
    return jnp.asarray(cos), jnp.asarray(sin)


def _mla_weights(w_q_up, w_kv_up):
    r = w_q_up.shape[0]
    wq = w_q_up.reshape(r, MLA_HEADS, MLA_NOPE + MLA_ROPE)
    nope, rope = wq[..., :MLA_NOPE], wq[..., MLA_NOPE:]
    q16 = MLA_ROPE // 4
    ropep = jnp.concatenate([rope[..., q16:2 * q16], rope[..., :q16], rope[..., 3 * q16:], rope[..., 2 * q16:3 * q16]],
                            axis=-1)
    zpad = jnp.zeros((r, MLA_HEADS, LANE - MLA_ROPE), w_q_up.dtype)
    wq1 = jnp.concatenate([nope, rope, zpad], axis=-1).reshape(r, MLA_HEADS * 2 * LANE).astype(BF16)
    wq2 = jnp.concatenate([ropep, zpad], axis=-1).reshape(r, MLA_HEADS * LANE).astype(BF16)
    rk = w_kv_up.shape[0]
    wkv = w_kv_up.reshape(rk, MLA_HEADS, 2, MLA_NOPE).transpose(0, 2, 1, 3).reshape(rk, 2 * MLA_HEADS * MLA_NOPE)
    return wq1, wq2, wkv.astype(BF16)


def _moe_plan(idx, rank, counts, n_tiles):
    cnt = counts[:, 0].astype(jnp.int32)
    padded = ((cnt + MOE_G - 1) // MOE_G) * MOE_G
    ends = jnp.cumsum(padded)
    starts = ends - padded
    e_ids = jnp.arange(N_EXPERTS, dtype=jnp.int32)
    dest = jnp.sum(jnp.where(idx[..., None] == e_ids, starts, 0), axis=-1) + rank
    tile_start = jnp.arange(n_tiles, dtype=jnp.int32) * MOE_G
    te = jnp.sum((tile_start[:, None] >= ends[None, :]).astype(jnp.int32), axis=1)
    active = te < N_EXPERTS
    te_c = jnp.minimum(te, N_EXPERTS - 1)
    tile_is = te_c[:, None] == e_ids[None, :]
    cnt_t = jnp.sum(jnp.where(tile_is, cnt, 0), axis=1)
    start_t = jnp.sum(jnp.where(tile_is, starts, 0), axis=1)
    valid = jnp.clip(cnt_t - (tile_start - start_t), 0, MOE_G)
    valid = jnp.where(active, valid, 0)
    last_e = jnp.max(jnp.where(cnt > 0, jnp.arange(N_EXPERTS, dtype=jnp.int32), 0))
    te_f = jnp.where(active, te_c, last_e)
    tail_blocks = (n_tiles * MOE_G - ends[-1]) // (MOE_G // 2)
    pad_start = jnp.concatenate([starts + cnt, ends[-1:]]).astype(jnp.int32)
    pad_len = jnp.concatenate([padded - cnt, tail_blocks[None]]).astype(jnp.int32)
    return dest, te_f, valid, pad_start, pad_len


def kernel(x, c, ctx, c_ctx, norm1_g, norm2_g, ada_w, ada_b, w_in, b_gate, gla_wa, gla_ba, gla_norm_g,
           na_rpb, mla_q_norm_g, mla_w_q_up, mla_kv_norm_g, mla_w_kv_up, w_branch_gla, w_branch_na,
           w_branch_mla, w_out, router_w, router_b, moe_w1, moe_b1, moe_w2, moe_b2, final_norm_g):
    b, n, d = x.shape
    rb = CTX_LEN + n
    assert b <= 8 and d == D_MODEL and ctx.shape[1] == CTX_LEN

    cc = jnp.zeros((16, d), F32).at[:b].set(c).at[8].set(c_ctx)
    mod_all = _ada_mod(cc, ada_w, ada_b)
    cos_t, sin_t = _rope_tables(rb)
    xs = jnp.concatenate([ctx, x], axis=1)

    for l in range(DEPTH):
        last = l == DEPTH - 1
        mod = mod_all[l].reshape(16 * 6, 1, d)
        if l == 0:
            h = _norm_mod(xs, norm1_g[l], mod)
        p = _matmul(h.reshape(b * rb, d), _proj_weight(w_in[l]), 1024 if (b * rb) % 1024 == 0 else ROW_TILE,
                    PROJ_TN, BF16)
        p3 = p.reshape(b, rb, PROJ_W)

        zpad = jnp.zeros((LANE - 2 * GLA_GATE_RANK, GLA_QK_W), F32)
        zr = jnp.zeros((GLA_GATE_RANK, GLA_QK_W), F32)
        waf = jnp.concatenate([gla_wa[l, 0], zr, zpad], axis=0).astype(BF16)
        wab = jnp.concatenate([zr, gla_wa[l, 1], zpad], axis=0).astype(BF16)
        og = _gla(p3, waf, wab, gla_ba[l], gla_norm_g[l].reshape(1, GLA_V_W))

        on = _na(p3, _na_bias_table(na_rpb[l], n // GRID_W))

        wq1, wq2, wkv = _mla_weights(mla_w_q_up[l], mla_w_kv_up[l])
        q_m, k_m, v_m = _mla_proj(p3, cos_t, sin_t, mla_q_norm_g[l].reshape(1, -1), mla_kv_norm_g[l].reshape(1, -1),
                                  wq1, wq2, wkv)
        om = _mla_attn(q_m, k_m, v_m, ctx_queries=False)
        om_ctx = None if last else _mla_attn(q_m, k_m, v_m, ctx_queries=True)

        xs, hp, logits_t = _merge(
            xs, p3, og, on, om, om_ctx, b_gate[l].reshape(1, -1),
            w_branch_gla[l].astype(BF16), w_branch_na[l].astype(BF16), w_branch_mla[l].astype(BF16),
            w_out[l].astype(BF16), mod, norm2_g[l].reshape(1, d), router_w[l].T,
            jnp.broadcast_to(router_b[l][:, None], (N_EXPERTS, LANE)), skip_ctx=last)

        t_tok = hp.shape[0]
        idx, wts, rank, counts = _route(logits_t)
        n_tiles = (TOP_K * t_tok) // MOE_G + N_EXPERTS
        dest, te, tv, pad_start, pad_len = _moe_plan(idx, rank, counts, n_tiles)
        xg = _dispatch(hp, dest, pad_start, pad_len, n_tiles * MOE_G)
        yg = _ffn(l, te, tv, xg, moe_w1, moe_b1, moe_w2, moe_b2)
        if last:
            (xs,) = _combine(xs, yg, dest, wts, mod, final_norm_g, mod, ctx_first=False, final_norm=True)
        else:
            xs, h = _combine(xs, yg, dest, wts, mod, norm1_g[l + 1], mod_all[l + 1].reshape(16 * 6, 1, d),
                             ctx_first=True, final_norm=False)
    return xs
```

```python
import functools

import numpy as np
import jax
import jax.numpy as jnp
from jax import lax
from jax.experimental import pallas as pl
from jax.experimental.pallas import tpu as pltpu

F32 = jnp.float32
BF16 = jnp.bfloat16

D_MODEL = 2048
DEPTH = 2
GRID_W = 64
CTX_LEN = 256
EPS = 1e-6
ROPE_BASE = 10000.0
NEG_INF = -1e30

GLA_HEADS = 4
GLA_DK = 64
GLA_DV = 128
GLA_GATE_RANK = 16
GLA_TAU = 16.0
GLA_CHUNK = 64
NA_HEADS = 8
NA_HEAD_DIM = 64
NA_KH = 8
NA_KW = 16
NA_SCALE = NA_HEAD_DIM ** -0.5
MLA_HEADS = 8
MLA_Q_RANK = 512
MLA_KV_RANK = 512
MLA_NOPE = 128
MLA_ROPE = 64
MLA_V = 128
MLA_SCALE = (MLA_NOPE + MLA_ROPE) ** -0.5
N_BRANCH = 3
N_EXPERTS = 32
TOP_K = 4
EXPERT_FF = D_MODEL
SWIGLU_LIMIT = 7.0
SWIGLU_ALPHA = 1.702

GLA_QK_W = GLA_HEADS * GLA_DK
GLA_V_W = GLA_HEADS * GLA_DV
NA_W = NA_HEADS * NA_HEAD_DIM
MLA_V_W = MLA_HEADS * MLA_V

LANE = 128
SUBLANE = 8
ROW_TILE = 256
VMEM_LIMIT = 56 * 1024 * 1024

COL_GATE = 0
COL_GQ = COL_GATE + N_BRANCH * D_MODEL
COL_GK = COL_GQ + GLA_QK_W
COL_GV = COL_GK + GLA_QK_W
COL_GR = COL_GV + GLA_V_W
COL_NQ = COL_GR + GLA_V_W
COL_NK = COL_NQ + NA_W
COL_NV = COL_NK + NA_W
COL_MCQ = COL_NV + NA_W
COL_MCKV = COL_MCQ + MLA_Q_RANK
COL_KR = COL_MCKV + MLA_KV_RANK
COL_KRP = COL_KR + LANE
COL_GA = COL_KRP + LANE
PROJ_TN = 1536
PROJ_W = 7 * PROJ_TN
assert COL_GA + LANE <= PROJ_W

MOE_G = 1024
MOE_SB = 256
MOE_CHAIN = 1024
MOE_FC = 256
DISPATCH_TT = 256
COMBINE_TT = 128
ROUTE_TT = 256
MLA_TQ = 2048
MLA_CHAIN = 256


def _cparams(sem):
    return pltpu.CompilerParams(dimension_semantics=sem, vmem_limit_bytes=VMEM_LIMIT)


def _nt(a, b):
    return lax.dot_general(a, b, (((1,), (1,)), ((), ())), preferred_element_type=F32)


def _tn(a, b):
    return lax.dot_general(a, b, (((0,), (0,)), ((), ())), preferred_element_type=F32)


def _mm(a, b):
    return jnp.dot(a, b, preferred_element_type=F32)


def _sigmoid(x):
    return 1.0 / (1.0 + jnp.exp(-x))


def _split_hi_lo(x):
    hi = x.astype(BF16)
    lo = (x - hi.astype(F32)).astype(BF16)
    return hi, lo


def _ada_kernel(c_ref, w_ref, b_ref, o_ref):
    c = c_ref[...]
    s = (c * _sigmoid(c)).astype(BF16)
    o_ref[0] = _mm(s, w_ref[0].astype(BF16)) + b_ref[0]


def _ada_mod(cc, ada_w, ada_b):
    nl, d, n6 = ada_w.shape
    tn = 1024
    return pl.pallas_call(
        _ada_kernel,
        grid=(nl, n6 // tn),
        in_specs=[pl.BlockSpec((16, d), lambda l, j: (0, 0)),
                  pl.BlockSpec((1, d, tn), lambda l, j: (l, 0, j)),
                  pl.BlockSpec((1, 1, tn), lambda l, j: (l, 0, j))],
        out_specs=pl.BlockSpec((1, 16, tn), lambda l, j: (l, 0, j)),
        out_shape=jax.ShapeDtypeStruct((nl, 16, n6), F32),
        compiler_params=_cparams(("arbitrary", "arbitrary")),
        name="ada_mod",
    )(cc, ada_w, ada_b.reshape(nl, 1, n6))


def _mod_spec(part, ctx_first):
    def imap(b, j):
        row = jnp.where(j == 0, 8, b) if ctx_first else b
        return (row * 6 + part, 0, 0)
    return pl.BlockSpec((1, 1, D_MODEL), imap)


def _norm_mod_kernel(x_ref, g_ref, sc_ref, sh_ref, o_ref):
    x = x_ref[0]
    ms = jnp.mean(x * x, axis=-1, keepdims=True)
    y = x * lax.rsqrt(ms + EPS) * g_ref[...]
    o_ref[0] = (y * (1.0 + sc_ref[0]) + sh_ref[0]).astype(o_ref.dtype)


def _norm_mod(x, g, mod):
    b, rb, d = x.shape
    return pl.pallas_call(
        _norm_mod_kernel,
        grid=(b, rb // ROW_TILE),
        in_specs=[pl.BlockSpec((1, ROW_TILE, d), lambda i, j: (i, j, 0)),
                  pl.BlockSpec((1, d), lambda i, j: (0, 0)),
                  _mod_spec(1, True), _mod_spec(0, True)],
        out_specs=pl.BlockSpec((1, ROW_TILE, d), lambda i, j: (i, j, 0)),
        out_shape=jax.ShapeDtypeStruct((b, rb, d), BF16),
        compiler_params=_cparams(("arbitrary", "arbitrary")),
        name="norm_mod",
    )(x, g.reshape(1, d), mod, mod)


def _matmul_kernel(x_ref, w_ref, o_ref):
    o_ref[...] = _mm(x_ref[...], w_ref[...]).astype(o_ref.dtype)


def _matmul(x, w, tm, tn, out_dtype):
    m, k = x.shape
    n = w.shape[1]
    return pl.pallas_call(
        _matmul_kernel,
        grid=(n // tn, m // tm),
        in_specs=[pl.BlockSpec((tm, k), lambda j, i: (i, 0)),
                  pl.BlockSpec((k, tn), lambda j, i: (0, j))],
        out_specs=pl.BlockSpec((tm, tn), lambda j, i: (i, j)),
        out_shape=jax.ShapeDtypeStruct((m, n), out_dtype),
        compiler_params=_cparams(("arbitrary", "arbitrary")),
        name="in_proj",
    )(x, w)


def _gla_kernel(q_ref, k_ref, v_ref, r_ref, ab_ref, waf_ref, wab_ref, ba_ref, g_ref, o_ref,
                laf_ref, lab_ref, of_ref, ob_ref, st_ref, *, n_ctx_chunks, n_chunks):
    c = GLA_CHUNK
    ab = ab_ref[0]
    zf = _mm(ab, waf_ref[...]) + ba_ref[0:1, :]
    zb = _mm(ab, wab_ref[...]) + ba_ref[1:2, :]
    laf_ref[...] = (jnp.minimum(zf, 0.0) - jnp.log(1.0 + jnp.exp(-jnp.abs(zf)))) * (1.0 / GLA_TAU)
    lab_ref[...] = (jnp.minimum(zb, 0.0) - jnp.log(1.0 + jnp.exp(-jnp.abs(zb)))) * (1.0 / GLA_TAU)
    st_ref[...] = jnp.zeros_like(st_ref)

    ri = lax.broadcasted_iota(jnp.int32, (c, c), 0)
    ci = lax.broadcasted_iota(jnp.int32, (c, c), 1)
    tri = [(ri >= ci), (ri <= ci)]
    tri_bf = [t.astype(F32).astype(BF16) for t in tri]
    tri4 = [jnp.concatenate([t] * GLA_HEADS, axis=0) for t in tri]
    lane_q = lax.broadcasted_iota(jnp.int32, (c, GLA_QK_W), 1) // GLA_DK
    lane_v = lax.broadcasted_iota(jnp.int32, (c, GLA_V_W), 1) // GLA_DV
    st_row_h = lax.broadcasted_iota(jnp.int32, (GLA_V_W, GLA_QK_W), 0) // GLA_DV
    st_col_h = lax.broadcasted_iota(jnp.int32, (GLA_V_W, GLA_QK_W), 1) // GLA_DK
    st_mask = st_row_h == st_col_h

    def one_dir(d, chunk, la_ref, out_ref):
        r0 = pl.multiple_of(chunk * c, c)
        la = la_ref[pl.ds(r0, c), :]
        hi, lo = _split_hi_lo(la)
        bc = _mm(tri_bf[d], hi) + _mm(tri_bf[d], lo)
        btot = bc[c - 1:c, :] if d == 0 else bc[0:1, :]
        q = q_ref[0, pl.ds(r0, c), :].astype(F32) * (GLA_DK ** -0.5)
        k = k_ref[0, pl.ds(r0, c), :].astype(F32)
        v = v_ref[0, pl.ds(r0, c), :]
        qd = (q * jnp.exp(bc)).astype(BF16)
        ki = (k * jnp.exp(-bc)).astype(BF16)
        kd = (k * jnp.exp(btot - bc)).astype(BF16)
        zero = jnp.zeros_like(qd)
        qs = jnp.concatenate([jnp.where(lane_q == h, qd, zero) for h in range(GLA_HEADS)], axis=0)
        att = _nt(qs, ki)
        att = jnp.where(tri4[d], att, 0.0).astype(BF16)
        rr = _mm(att, v)
        o = _nt(qd, st_ref[d].astype(BF16))
        for h in range(GLA_HEADS):
            o = o + jnp.where(lane_v == h, rr[h * c:(h + 1) * c, :], 0.0)
        out_ref[pl.ds(r0, c), :] = o
        upd = _tn(v, kd)
        st_ref[d] = jnp.where(st_mask, st_ref[d] * jnp.exp(btot) + upd, 0.0)

    def step(i, carry):
        one_dir(0, i, laf_ref, of_ref)
        cb = jnp.where(i < n_ctx_chunks, n_ctx_chunks - 1 - i, n_chunks + n_ctx_chunks - 1 - i)
        one_dir(1, cb, lab_ref, ob_ref)
        return carry

    lax.fori_loop(0, n_chunks, step, 0, unroll=4)

    def epilogue(j, carry):
        r0 = pl.multiple_of(j * ROW_TILE, ROW_TILE)
        o = of_ref[pl.ds(r0, ROW_TILE), :] + ob_ref[pl.ds(r0, ROW_TILE), :]
        r = r_ref[0, pl.ds(r0, ROW_TILE), :].astype(F32)
        gate = r * _sigmoid(r)
        for h in range(GLA_HEADS):
            sl = slice(h * GLA_DV, (h + 1) * GLA_DV)
            oh = o[:, sl]
            ms = jnp.mean(oh * oh, axis=-1, keepdims=True)
            y = oh * lax.rsqrt(ms + EPS) * g_ref[:, sl]
            o_ref[0, pl.ds(r0, ROW_TILE), sl] = (y * gate[:, sl]).astype(o_ref.dtype)
        return carry

    lax.fori_loop(0, (n_chunks * c) // ROW_TILE, epilogue, 0)


def _gla(p3, waf, wab, ba, g):
    b, rb, _ = p3.shape
    n_chunks = rb // GLA_CHUNK
    kern = functools.partial(_gla_kernel, n_ctx_chunks=CTX_LEN // GLA_CHUNK, n_chunks=n_chunks)

    def col(width, off):
        return pl.BlockSpec((1, rb, width), lambda i: (i, 0, off // width))

    return pl.pallas_call(
        kern,
        grid=(b,),
        in_specs=[col(GLA_QK_W, COL_GQ), col(GLA_QK_W, COL_GK), col(GLA_V_W, COL_GV), col(GLA_V_W, COL_GR),
                  col(LANE, COL_GA),
                  pl.BlockSpec((LANE, GLA_QK_W), lambda i: (0, 0)),
                  pl.BlockSpec((LANE, GLA_QK_W), lambda i: (0, 0)),
                  pl.BlockSpec((2, GLA_QK_W), lambda i: (0, 0)),
                  pl.BlockSpec((1, GLA_V_W), lambda i: (0, 0))],
        out_specs=pl.BlockSpec((1, rb, GLA_V_W), lambda i: (i, 0, 0)),
        out_shape=jax.ShapeDtypeStruct((b, rb, GLA_V_W), BF16),
        scratch_shapes=[pltpu.VMEM((rb, GLA_QK_W), F32), pltpu.VMEM((rb, GLA_QK_W), F32),
                        pltpu.VMEM((rb, GLA_V_W), F32), pltpu.VMEM((rb, GLA_V_W), F32),
                        pltpu.VMEM((2, GLA_V_W, GLA_QK_W), F32)],
        compiler_params=_cparams(("arbitrary",)),
        name="gla",
    )(p3, p3, p3, p3, p3, waf, wab, ba, g)


def _softmax_pv(parts):
    m = parts[0][0].max(axis=-1, keepdims=True)
    for s, _ in parts[1:]:
        m = jnp.maximum(m, s.max(axis=-1, keepdims=True))
    acc = None
    den = None
    for s, v in parts:
        e = jnp.exp(s - m)
        l = e.sum(axis=-1, keepdims=True)
        pv = _mm(e.astype(BF16), v)
        acc = pv if acc is None else acc + pv
        den = l if den is None else den + l
    return acc / den


NA_QR = 4
NA_KR = NA_QR + NA_KH


def _na_block_rows(n_rows):
    r0s = (0, 2 * NA_QR, n_rows - NA_QR)
    return [(r0, int(np.clip(r0 - NA_KH // 2, 0, n_rows - NA_KR))) for r0 in r0s]


def _na_kernel(q_ref, k_ref, v_ref, bias_ref, o_ref, *, n_rows):
    nq = NA_QR * GRID_W
    nk = NA_KR * GRID_W
    n_blocks = n_rows // NA_QR
    lane = lax.broadcasted_iota(jnp.int32, (nq, LANE), 1)
    head_mask = [lane < NA_HEAD_DIM, lane >= NA_HEAD_DIM]
    kc = k_ref[0, 0:CTX_LEN, :]
    vc = v_ref[0, 0:CTX_LEN, :]

    def block(bi, carry):
        r0 = bi * NA_QR
        ks = jnp.clip(r0 - NA_KH // 2, 0, n_rows - NA_KR)
        var = jnp.where(bi == 0, 0, jnp.where(bi == n_blocks - 1, 2, 1))
        q0 = pl.multiple_of(CTX_LEN + r0 * GRID_W, GRID_W)
        k0 = pl.multiple_of(CTX_LEN + ks * GRID_W, GRID_W)
        q = q_ref[0, pl.ds(q0, nq), :]
        kl = k_ref[0, pl.ds(k0, nk), :]
        vl = v_ref[0, pl.ds(k0, nk), :]
        outs = []
        for hh in range(2):
            qm = jnp.where(head_mask[hh], q, jnp.zeros_like(q))
            s_loc = _nt(qm, kl) * NA_SCALE + bias_ref[hh, var]
            s_ctx = _nt(qm, kc) * NA_SCALE
            outs.append(_softmax_pv([(s_loc, vl), (s_ctx, vc)]))
        o_ref[0, pl.ds(q0, nq), :] = jnp.where(head_mask[0], outs[0], outs[1]).astype(o_ref.dtype)
        return carry

    lax.fori_loop(0, n_blocks, block, 0, unroll=2)

    qc = q_ref[0, 0:CTX_LEN, :]
    lane_c = lax.broadcasted_iota(jnp.int32, (CTX_LEN, LANE), 1)
    outs = []
    for hh in range(2):
        msk = (lane_c < NA_HEAD_DIM) if hh == 0 else (lane_c >= NA_HEAD_DIM)
        qm = jnp.where(msk, qc, jnp.zeros_like(qc))
        outs.append(_softmax_pv([(_nt(qm, kc) * NA_SCALE, vc)]))
    o_ref[0, 0:CTX_LEN, :] = jnp.where(lane_c < NA_HEAD_DIM, outs[0], outs[1]).astype(o_ref.dtype)


def _na_bias_table(rpb, n_rows):
    qcol = np.arange(GRID_W)[:, None]
    kcol = np.arange(GRID_W)[None, :]
    wstart = np.clip(qcol - NA_KW // 2, 0, GRID_W - NA_KW)
    in_win = (kcol >= wstart) & (kcol < wstart + NA_KW)
    dc = kcol - qcol + NA_KW - 1
    col_hot = ((dc[:, :, None] == np.arange(2 * NA_KW - 1)) & in_win[:, :, None]).astype(np.float32)
    geo = _na_block_rows(n_rows)
    row_hot = np.zeros((len(geo), NA_QR, NA_KR, 2 * NA_KH - 1), np.float32)
    for v, (r0, ks) in enumerate(geo):
        for a in range(NA_QR):
            rs = int(np.clip(r0 + a - NA_KH // 2, 0, n_rows - NA_KH))
            for i in range(NA_KR):
                if rs <= ks + i < rs + NA_KH:
                    row_hot[v, a, i, ks + i - (r0 + a) + NA_KH - 1] = 1.0
    t = jnp.einsum('hrc,vair,qkc->hvaqik', rpb.astype(F32), jnp.asarray(row_hot), jnp.asarray(col_hot),
                   precision=lax.Precision.HIGHEST)
    inside = (row_hot.sum(-1) > 0)[:, :, None, :, None] & in_win[None, None, :, None, :]
    t = jnp.where(inside[None], t, NEG_INF)
    return t.reshape(NA_HEADS, len(geo), NA_QR * GRID_W, NA_KR * GRID_W)


def _na(p3, bias):
    b, rb, _ = p3.shape
    n_rows = (rb - CTX_LEN) // GRID_W
    kern = functools.partial(_na_kernel, n_rows=n_rows)

    def col(off):
        return pl.BlockSpec((1, rb, LANE), lambda p, i: (i, 0, off // LANE + p))

    return pl.pallas_call(
        kern,
        grid=(NA_HEADS // 2, b),
        in_specs=[col(COL_NQ), col(COL_NK), col(COL_NV),
                  pl.BlockSpec((2,) + bias.shape[1:], lambda p, i: (p, 0, 0, 0))],
        out_specs=pl.BlockSpec((1, rb, LANE), lambda p, i: (i, 0, p)),
        out_shape=jax.ShapeDtypeStruct((b, rb, NA_W), BF16),
        compiler_params=_cparams(("arbitrary", "arbitrary")),
        name="na",
    )(p3, p3, p3, bias)


def _mla_proj_kernel(cq_ref, ckv_ref, kr_ref, krp_ref, cos_ref, sin_ref, gq_ref, gkv_ref,
                     wq_ref, wq2_ref, wkv_ref, q_ref, k_ref, v_ref):
    def norm(x, g):
        x = x.astype(F32)
        ms = jnp.mean(x * x, axis=-1, keepdims=True)
        return (x * lax.rsqrt(ms + EPS) * g).astype(BF16)

    cos = cos_ref[...]
    sin = sin_ref[...]
    nq = norm(cq_ref[0], gq_ref[...])
    yq = _mm(nq, wq_ref[...])
    yq2 = _mm(nq, wq2_ref[...])
    nkv = norm(ckv_ref[0], gkv_ref[...])
    ykv = _mm(nkv, wkv_ref[...])
    k_rot = (kr_ref[0].astype(F32) * cos + krp_ref[0].astype(F32) * sin).astype(BF16)
    for h in range(MLA_HEADS):
        a = 2 * h * LANE
        q_ref[0, :, a:a + LANE] = (yq[:, a:a + LANE] * MLA_SCALE).astype(BF16)
        rot = yq[:, a + LANE:a + 2 * LANE] * cos + yq2[:, h * LANE:(h + 1) * LANE] * sin
        q_ref[0, :, a + LANE:a + 2 * LANE] = (rot * MLA_SCALE).astype(BF16)
        k_ref[0, :, a:a + LANE] = ykv[:, h * LANE:(h + 1) * LANE].astype(BF16)
        k_ref[0, :, a + LANE:a + 2 * LANE] = k_rot
        v0 = MLA_HEADS * MLA_NOPE + h * MLA_V
        v_ref[0, :, a:a + LANE] = ykv[:, v0:v0 + MLA_V].astype(BF16)
        v_ref[0, :, a + LANE:a + 2 * LANE] = jnp.ones((ykv.shape[0], LANE), BF16)


def _mla_proj(p3, cos_t, sin_t, gq, gkv, wq, wq2, wkv):
    b, rb, _ = p3.shape
    tm = ROW_TILE

    def col(width, off):
        return pl.BlockSpec((1, tm, width), lambda i, j: (i, j, off // width))

    def full(a):
        return pl.BlockSpec(a.shape, lambda i, j: (0, 0))

    hw = MLA_HEADS * 2 * LANE
    nj = rb // tm
    return pl.pallas_call(
        _mla_proj_kernel,
        grid=(b, nj),
        in_specs=[col(MLA_Q_RANK, COL_MCQ), col(MLA_KV_RANK, COL_MCKV), col(LANE, COL_KR), col(LANE, COL_KRP),
                  pl.BlockSpec((tm, LANE), lambda i, j: (j, 0)), pl.BlockSpec((tm, LANE), lambda i, j: (j, 0)),
                  full(gq), full(gkv), full(wq), full(wq2), full(wkv)],
        out_specs=[pl.BlockSpec((1, tm, hw), lambda i, j: (i, (j + nj - 1) % nj, 0)),
                   pl.BlockSpec((1, tm, hw), lambda i, j: (i, (j + nj - 1) % nj, 0)),
                   pl.BlockSpec((1, tm, hw), lambda i, j: (i, (j + nj - 1) % nj, 0))],
        out_shape=[jax.ShapeDtypeStruct((b, rb, hw), BF16), jax.ShapeDtypeStruct((b, rb, hw), BF16),
                   jax.ShapeDtypeStruct((b, rb, hw), BF16)],
        compiler_params=_cparams(("arbitrary", "arbitrary")),
        name="mla_proj",
    )(p3, p3, p3, p3, cos_t, sin_t, gq, gkv, wq, wq2, wkv)


def _mla_attn_kernel(q_ref, k_ref, v_ref, o_ref):
    k = k_ref[0]
    v = v_ref[0]
    tq = q_ref.shape[1]
    sub = min(tq, MLA_CHAIN)
    for r0 in range(0, tq, sub):
        s = _nt(q_ref[0, r0:r0 + sub, :], k)
        e = jnp.exp((s - s.max(axis=-1, keepdims=True)).astype(BF16))
        pv = _mm(e, v)
        o_ref[0, r0:r0 + sub, :] = (pv[:, :MLA_V] / pv[:, MLA_V:]).astype(o_ref.dtype)


def _mla_attn(q, k, v, ctx_queries):
    b, rb, _ = q.shape
    n_lat = rb - CTX_LEN
    if ctx_queries:
        tq, nq, n_keys = CTX_LEN, 1, CTX_LEN
        q_off = kv_blk = n_lat // CTX_LEN
    else:
        tq = MLA_TQ if n_lat % MLA_TQ == 0 else ROW_TILE
        nq, n_keys, q_off, kv_blk = n_lat // tq, rb, 0, 0
    return pl.pallas_call(
        _mla_attn_kernel,
        grid=(b, MLA_HEADS, nq),
        in_specs=[pl.BlockSpec((1, tq, 2 * LANE), lambda i, h, j: (i, j + q_off, h)),
                  pl.BlockSpec((1, n_keys, 2 * LANE), lambda i, h, j: (i, kv_blk, h)),
                  pl.BlockSpec((1, n_keys, 2 * LANE), lambda i, h, j: (i, kv_blk, h))],
        out_specs=pl.BlockSpec((1, tq, MLA_V), lambda i, h, j: (i, j, h)),
        out_shape=jax.ShapeDtypeStruct((b, nq * tq, MLA_V_W), BF16),
        compiler_params=_cparams(("arbitrary", "arbitrary", "arbitrary")),
        name="mla_attn_ctx" if ctx_queries else "mla_attn",
    )(q, k, v)


def _merge_kernel(x_ref, oa_ref, ob_ref, oc_ref, occ_ref, ga_ref, gb_ref, gc_ref, bg_ref,
                  wa_ref, wb_ref, wc_ref, wo_ref, gt1_ref, sc2_ref, sh2_ref, g2_ref, rw_ref, rb_ref,
                  xo_ref, h_ref, lg_ref, *, ctx_tile):
    d = D_MODEL

    wh, wl = _split_hi_lo(rw_ref[...])
    tm = x_ref.shape[1]
    sub = tm
    for r0 in range(0, tm, sub):
        rows = slice(r0, r0 + sub)

        def gate(g_ref, k):
            return _sigmoid(g_ref[0, rows, :].astype(F32) + bg_ref[:, k * d:(k + 1) * d])

        oc = oc_ref[0, rows, :]
        if ctx_tile:
            oc = jnp.where(pl.program_id(1) == 0, occ_ref[0, rows, :], oc)
        m = gate(ga_ref, 0) * _mm(oa_ref[0, rows, :], wa_ref[...])
        m = m + gate(gb_ref, 1) * _mm(ob_ref[0, rows, :], wb_ref[...])
        m = m + gate(gc_ref, 2) * _mm(oc, wc_ref[...])
        y = _mm(m.astype(BF16), wo_ref[...])
        x = x_ref[0, rows, :] + gt1_ref[0] * y
        xo_ref[0, rows, :] = x
        ms = jnp.mean(x * x, axis=-1, keepdims=True)
        h = x * lax.rsqrt(ms + EPS) * g2_ref[...]
        h = h * (1.0 + sc2_ref[0]) + sh2_ref[0]
        h_ref[rows, :] = h
        hh, hl = _split_hi_lo(h)
        lg_ref[:, rows] = _nt(wh, hh) + _nt(wh, hl) + _nt(wl, hh) + rb_ref[:, 0:1]


def _merge(x, p3, og, on, om, om_ctx, bg, wa, wb, wc, wo, mod, g2, rwt, rbias, skip_ctx):
    b, rb, d = x.shape
    tm = ROW_TILE
    jo = 1 if skip_ctx else 0
    nj = rb // tm - jo
    rows_out = nj * tm

    def rows(width, cblk=0):
        return pl.BlockSpec((1, tm, width), lambda i, j: (i, j + jo, cblk))

    om_spec = pl.BlockSpec((1, tm, MLA_V_W), lambda i, j: (i, jnp.maximum(j + jo - 1, 0), 0))
    omc_spec = pl.BlockSpec((1, tm, MLA_V_W), lambda i, j: (i, 0, 0))
    kern = functools.partial(_merge_kernel, ctx_tile=not skip_ctx)

    def full(a):
        return pl.BlockSpec(a.shape, lambda i, j: (0,) * a.ndim, pipeline_mode=pl.Buffered(1))

    def mod_spec(part):
        def imap(i, j):
            row = i if skip_ctx else jnp.where(j == 0, 8, i)
            return (row * 6 + part, 0, 0)
        return pl.BlockSpec((1, 1, d), imap)

    return pl.pallas_call(
        kern,
        grid=(b, nj),
        in_specs=[rows(d), rows(GLA_V_W), rows(NA_W), om_spec, omc_spec,
                  rows(d, 0), rows(d, 1), rows(d, 2), full(bg),
                  full(wa), full(wb), full(wc), full(wo),
                  mod_spec(2), mod_spec(4), mod_spec(3), full(g2), full(rwt), full(rbias)],
        out_specs=[pl.BlockSpec((1, tm, d), lambda i, j: (i, j, 0)),
                   pl.BlockSpec((tm, d), lambda i, j: (i * nj + j, 0)),
                   pl.BlockSpec((N_EXPERTS, tm), lambda i, j: (0, i * nj + j))],
        out_shape=[jax.ShapeDtypeStruct((b, rows_out, d), F32),
                   jax.ShapeDtypeStruct((b * rows_out, d), F32),
                   jax.ShapeDtypeStruct((N_EXPERTS, b * rows_out), F32)],
        compiler_params=_cparams(("arbitrary", "arbitrary")),
        name="merge",
    )(x, og, on, om, om if skip_ctx else om_ctx, p3, p3, p3, bg, wa, wb, wc, wo, mod, mod, mod, g2, rwt, rbias)


def _route_kernel(l_ref, idx_ref, w_ref, rank_ref, cnt_ref, carry_ref):
    i = pl.program_id(0)
    tt = l_ref.shape[1]

    @pl.when(i == 0)
    def _():
        carry_ref[...] = jnp.zeros_like(carry_ref)

    l = l_ref[...]
    eio = lax.broadcasted_iota(jnp.int32, (N_EXPERTS, tt), 0)
    vals, idxs = [], []
    for _ in range(TOP_K):
        m = l.max(axis=0, keepdims=True)
        ik = jnp.min(jnp.where(l == m, eio, N_EXPERTS), axis=0, keepdims=True)
        vals.append(m)
        idxs.append(ik)
        l = jnp.where(eio == ik, -jnp.inf, l)
    es = [jnp.exp(v - vals[0]) for v in vals]
    den = es[0] + es[1] + es[2] + es[3]
    sel = jnp.zeros((N_EXPERTS, tt), F32)
    for ik in idxs:
        sel = sel + (eio == ik).astype(F32)
    si = lax.broadcasted_iota(jnp.int32, (tt, tt), 0)
    ti = lax.broadcasted_iota(jnp.int32, (tt, tt), 1)
    before = (si < ti).astype(F32).astype(BF16)
    rank_full = _mm(sel.astype(BF16), before) + carry_ref[:, 0:1]
    for k in range(TOP_K):
        idx_ref[k:k + 1, :] = idxs[k]
        w_ref[k:k + 1, :] = es[k] / den
        rk = jnp.sum(jnp.where(eio == idxs[k], rank_full, 0.0), axis=0, keepdims=True)
        rank_ref[k:k + 1, :] = rk.astype(jnp.int32)
    carry_ref[...] = carry_ref[...] + jnp.sum(sel, axis=1, keepdims=True)
    cnt_ref[...] = carry_ref[...]


def _route(logits_t):
    ne, t = logits_t.shape
    tt = ROUTE_TT
    spec4 = pl.BlockSpec((TOP_K, tt), lambda i: (0, i))
    return pl.pallas_call(
        _route_kernel,
        grid=(t // tt,),
        in_specs=[pl.BlockSpec((ne, tt), lambda i: (0, i))],
        out_specs=[spec4, spec4, spec4, pl.BlockSpec((ne, LANE), lambda i: (0, 0))],
        out_shape=[jax.ShapeDtypeStruct((TOP_K, t), jnp.int32), jax.ShapeDtypeStruct((TOP_K, t), F32),
                   jax.ShapeDtypeStruct((TOP_K, t), jnp.int32), jax.ShapeDtypeStruct((ne, LANE), F32)],
        scratch_shapes=[pltpu.VMEM((ne, LANE), F32)],
        compiler_params=_cparams(("arbitrary",)),
        name="route",
    )(logits_t)


def _pad_fill(ps_ref, pn_ref, zero_ref, xg_ref, sem, wait):
    def copy(pos, rows):
        cp = pltpu.make_async_copy(zero_ref.at[pl.ds(0, rows), :], xg_ref.at[pl.ds(pos, rows), :], sem)
        cp.wait() if wait else cp.start()

    def per_expert(e, carry):
        pos = ps_ref[e]
        head = (-pos) & (SUBLANE - 1)
        for r in range(SUBLANE - 1):
            @pl.when(r < head)
            def _(r=r):
                copy(pos + r, 1)

        pos = pos + head
        n = pn_ref[e] - head
        bit = MOE_G // 2
        while bit >= SUBLANE:
            on = (n & bit) != 0

            @pl.when(on)
            def _(pos=pos, bit=bit):
                copy(pl.multiple_of(pos, SUBLANE), bit)

            pos = pos + jnp.where(on, bit, 0)
            bit //= 2
        return carry

    lax.fori_loop(0, N_EXPERTS, per_expert, 0)

    zr = zero_ref.shape[0]

    def tail(i, carry):
        pos = pl.multiple_of(ps_ref[N_EXPERTS] + i * zr, zr)
        cp = pltpu.make_async_copy(zero_ref, xg_ref.at[pl.ds(pos, zr), :], sem)
        cp.wait() if wait else cp.start()
        return carry

    lax.fori_loop(0, pn_ref[N_EXPERTS], tail, 0)


def _dispatch_kernel(ps_ref, pn_ref, dest_ref, h_ref, xg_ref, zero_ref, sem, zsem):
    tt = h_ref.shape[0]

    @pl.when(pl.program_id(0) == 0)
    def _():
        zero_ref[...] = jnp.zeros_like(zero_ref)
        _pad_fill(ps_ref, pn_ref, zero_ref, xg_ref, zsem, wait=False)

    def issue(t, carry):
        for k in range(TOP_K):
            pltpu.make_async_copy(h_ref.at[pl.ds(t, 1), :], xg_ref.at[pl.ds(dest_ref[0, k, t], 1), :], sem).start()
        return carry

    lax.fori_loop(0, tt, issue, 0)
    for k in range(TOP_K):
        pltpu.make_async_copy(h_ref, xg_ref.at[pl.ds(0, tt), :], sem).wait()

    @pl.when(pl.program_id(0) == 0)
    def _():
        _pad_fill(ps_ref, pn_ref, zero_ref, xg_ref, zsem, wait=True)


def _dispatch(hp, dest, pad_start, pad_len, n_slots):
    t, w = hp.shape
    tt = DISPATCH_TT
    dest3 = dest.reshape(TOP_K, t // tt, tt).transpose(1, 0, 2)
    grid_spec = pltpu.PrefetchScalarGridSpec(
        num_scalar_prefetch=2,
        grid=(t // tt,),
        in_specs=[pl.BlockSpec((1, TOP_K, tt), lambda i, ps, pn: (i, 0, 0), memory_space=pltpu.SMEM),
                  pl.BlockSpec((tt, w), lambda i, ps, pn: (i, 0))],
        out_specs=pl.BlockSpec(memory_space=pl.ANY),
        scratch_shapes=[pltpu.VMEM((MOE_G // 2, w), hp.dtype), pltpu.SemaphoreType.DMA(()),
                        pltpu.SemaphoreType.DMA(())],
    )
    return pl.pallas_call(
        _dispatch_kernel,
        grid_spec=grid_spec,
        out_shape=jax.ShapeDtypeStruct((n_slots, w), hp.dtype),
        compiler_params=_cparams(("arbitrary",)),
        name="dispatch",
    )(pad_start, pad_len, dest3, hp)


def _ffn_kernel(te_ref, tv_ref, x_ref, w1g_ref, w1l_ref, b1g_ref, b1l_ref, w2_ref, b2_ref, y_ref,
                xb_ref, *, n_fc, n_tiles):
    i = pl.program_id(0)
    j = pl.program_id(1)
    valid = tv_ref[i]
    g = y_ref.shape[0]
    chunk = x_ref.shape[0]

    @pl.when(i < n_tiles)
    def _():
        xb_ref[i % 2, pl.ds(pl.multiple_of(j * chunk, chunk), chunk), :] = x_ref[...].astype(BF16)

    cur = (i + 1) % 2

    @pl.when((valid > 0) & (j == 0))
    def _():
        y_ref[...] = jnp.broadcast_to(b2_ref[0, 0], y_ref.shape)

    for nr in range(MOE_SB, g + 1, MOE_SB):
        @pl.when((valid > nr - MOE_SB) & (valid <= nr))
        def _(nr=nr):
            wg = w1g_ref[0, 0].astype(BF16)
            wl = w1l_ref[0, 0].astype(BF16)
            w2 = w2_ref[0, 0].astype(BF16)
            for r0 in range(0, nr, MOE_CHAIN):
                r1 = min(r0 + MOE_CHAIN, nr)
                x = xb_ref[cur, r0:r1, :]
                ug = _mm(x, wg) + b1g_ref[0, 0]
                ul = _mm(x, wl) + b1l_ref[0, 0]
                xg = jnp.minimum(ug, SWIGLU_LIMIT)
                xl = jnp.clip(ul, -SWIGLU_LIMIT, SWIGLU_LIMIT)
                act = xg * _sigmoid(SWIGLU_ALPHA * xg) * (xl + 1.0)
                y_ref[r0:r1, :] = y_ref[r0:r1, :] + _mm(act.astype(BF16), w2)

    @pl.when((j == n_fc - 1) & (valid == 0))
    def _():
        y_ref[...] = jnp.zeros_like(y_ref)


def _ffn(layer, tile_expert, tile_valid, xg, w1, b1, w2, b2):
    n_slots, d = xg.shape
    nl, ne, _, ff2 = w1.shape
    ff = ff2 // 2
    n_fc = ff // MOE_FC
    n_tiles = n_slots // MOE_G
    chunk = MOE_G // n_fc
    kern = functools.partial(_ffn_kernel, n_fc=n_fc, n_tiles=n_tiles)
    tile_expert = jnp.concatenate([tile_expert[:1], tile_expert])
    tile_valid = jnp.concatenate([jnp.zeros((1,), tile_valid.dtype), tile_valid])

    def jj(j, tv, i):
        return jnp.where(tv[i] > 0, j, n_fc - 1)

    grid_spec = pltpu.PrefetchScalarGridSpec(
        num_scalar_prefetch=2,
        grid=(n_tiles + 1, n_fc),
        in_specs=[pl.BlockSpec((chunk, d), lambda i, j, te, tv: (jnp.minimum(i, n_tiles - 1) * n_fc + j, 0)),
                  pl.BlockSpec((1, 1, d, MOE_FC), lambda i, j, te, tv: (layer, te[i], 0, jj(j, tv, i))),
                  pl.BlockSpec((1, 1, d, MOE_FC), lambda i, j, te, tv: (layer, te[i], 0, jj(j, tv, i) + n_fc)),
                  pl.BlockSpec((1, 1, 1, MOE_FC), lambda i, j, te, tv: (layer, te[i], 0, jj(j, tv, i))),
                  pl.BlockSpec((1, 1, 1, MOE_FC), lambda i, j, te, tv: (layer, te[i], 0, jj(j, tv, i) + n_fc)),
                  pl.BlockSpec((1, 1, MOE_FC, d), lambda i, j, te, tv: (layer, te[i], jj(j, tv, i), 0)),
                  pl.BlockSpec((1, 1, 1, d), lambda i, j, te, tv: (layer, te[i], 0, 0))],
        out_specs=pl.BlockSpec((MOE_G, d), lambda i, j, te, tv: (jnp.maximum(i - 1, 0), 0)),
        scratch_shapes=[pltpu.VMEM((2, MOE_G, d), BF16)],
    )
    return pl.pallas_call(
        kern,
        grid_spec=grid_spec,
        out_shape=jax.ShapeDtypeStruct((n_slots, d), F32),
        compiler_params=_cparams(("arbitrary", "arbitrary")),
        name="moe_ffn",
    )(tile_expert, tile_valid, xg, w1, w1, b1.reshape(nl, ne, 1, ff2), b1.reshape(nl, ne, 1, ff2), w2,
      b2.reshape(nl, ne, 1, d))


def _combine_kernel(dest_ref, destn_ref, x_ref, wt_ref, gt2_ref, gn_ref, sc_ref, sh_ref, yg_ref, o_ref, *rest,
                    final_norm):
    h_ref = None if final_norm else rest[0]
    buf_ref, sem = rest[-2:]
    i = pl.program_id(0)
    n = pl.num_programs(0)
    tt = x_ref.shape[1]
    slot = i % 2

    def issue(d_ref, s):
        def body(t, carry):
            for k in range(TOP_K):
                pltpu.make_async_copy(yg_ref.at[pl.ds(d_ref[0, k, t], 1), :],
                                      buf_ref.at[s, k, pl.ds(t, 1), :], sem.at[s]).start()
            return carry
        lax.fori_loop(0, tt, body, 0)

    @pl.when(i == 0)
    def _():
        issue(dest_ref, 0)

    @pl.when(i + 1 < n)
    def _():
        issue(destn_ref, 1 - slot)

    for k in range(TOP_K):
        pltpu.make_async_copy(yg_ref.at[pl.ds(0, tt), :], buf_ref.at[slot, k], sem.at[slot]).wait()

    wt = wt_ref[...]
    y = buf_ref[slot, 0] * wt[:, 0:1]
    for k in range(1, TOP_K):
        y = y + buf_ref[slot, k] * wt[:, k:k + 1]
    x = x_ref[0] + gt2_ref[0] * y
    ms = jnp.mean(x * x, axis=-1, keepdims=True)
    xn = x * lax.rsqrt(ms + EPS) * gn_ref[...]
    if final_norm:
        o_ref[0] = xn
    else:
        o_ref[0] = x
        h_ref[0] = (xn * (1.0 + sc_ref[0]) + sh_ref[0]).astype(h_ref.dtype)


def _combine(x, yg, dest, wts, mod, gn, mod_next, ctx_first, final_norm):
    b, rows, d = x.shape
    tt = COMBINE_TT
    nj = rows // tt
    t = b * rows
    nt = t // tt
    dest3 = dest.reshape(TOP_K, nt, tt).transpose(1, 0, 2)
    wt = wts.T
    kern = functools.partial(_combine_kernel, final_norm=final_norm)

    def mod_spec(part):
        def imap(i):
            bi = i // nj
            row = jnp.where((i % nj) * tt < CTX_LEN, 8, bi) if ctx_first else bi
            return (row * 6 + part, 0, 0)
        return pl.BlockSpec((1, 1, d), imap)

    row_spec = pl.BlockSpec((1, tt, d), lambda i: (i // nj, i % nj, 0))
    out_specs = [row_spec]
    out_shape = [jax.ShapeDtypeStruct((b, rows, d), F32)]
    if not final_norm:
        out_specs.append(row_spec)
        out_shape.append(jax.ShapeDtypeStruct((b, rows, d), BF16))

    return pl.pallas_call(
        kern,
        grid=(nt,),
        in_specs=[pl.BlockSpec((1, TOP_K, tt), lambda i: (i, 0, 0), memory_space=pltpu.SMEM),
                  pl.BlockSpec((1, TOP_K, tt), lambda i: (jnp.minimum(i + 1, nt - 1), 0, 0),
                               memory_space=pltpu.SMEM),
                  row_spec,
                  pl.BlockSpec((tt, TOP_K), lambda i: (i, 0)),
                  mod_spec(5),
                  pl.BlockSpec((1, d), lambda i: (0, 0)),
                  mod_spec(1), mod_spec(0),
                  pl.BlockSpec(memory_space=pl.ANY)],
        out_specs=out_specs,
        out_shape=out_shape,
        scratch_shapes=[pltpu.VMEM((2, TOP_K, tt, d), yg.dtype), pltpu.SemaphoreType.DMA((2,))],
        compiler_params=_cparams(("arbitrary",)),
        name="combine",
    )(dest3, dest3, x, wt, mod, gn.reshape(1, d), mod_next, mod_next, yg)


def _proj_weight(w_in):
    d = w_in.shape[0]
    splits = (GLA_QK_W, GLA_QK_W, GLA_V_W, GLA_V_W, GLA_GATE_RANK, GLA_GATE_RANK,
              NA_W, NA_W, NA_W, MLA_Q_RANK, MLA_KV_RANK, MLA_ROPE, N_BRANCH * D_MODEL)
    pts = np.cumsum((0,) + splits)
    (gq, gk, gv, gr, gaf, gab, nq, nk, nv, mcq, mckv, mkr, gate) = [w_in[:, pts[i]:pts[i + 1]] for i in range(13)]
    q16 = MLA_ROPE // 4
    mkrp = jnp.concatenate([mkr[:, q16:2 * q16], mkr[:, :q16], mkr[:, 3 * q16:], mkr[:, 2 * q16:3 * q16]], axis=1)
    z = lambda n: jnp.zeros((d, n), w_in.dtype)
    cols = [gate, gq, gk, gv, gr, nq, nk, nv, mcq, mckv,
            mkr, z(LANE - MLA_ROPE), mkrp, z(LANE - MLA_ROPE),
            gaf, gab, z(LANE - 2 * GLA_GATE_RANK)]
    w = jnp.concatenate(cols, axis=1)
    w = jnp.concatenate([w, z(PROJ_W - w.shape[1])], axis=1)
    return w.astype(BF16)


def _rope_tables(rb):
    n = rb - CTX_LEN
    t = np.arange(n)
    nf = MLA_ROPE // 4
    freqs = ROPE_BASE ** (-np.arange(nf, dtype=np.float64) / nf)
    cos = np.zeros((rb, LANE), np.float32)
    sin = np.zeros((rb, LANE), np.float32)
    cos[:CTX_LEN, :MLA_ROPE] = 1.0
    for a, pos in enumerate((t // GRID_W, t % GRID_W)):
        ang = (pos.astype(np.float32)[:, None] * freqs.astype(np.float32)[None, :]).astype(np.float32)
        c, s = np.cos(ang), np.sin(ang)
        base = a * 2 * nf
        cos[CTX_LEN:, base:base + nf] = c
        cos[CTX_LEN:, base + nf:base + 2 * nf] = c
        sin[CTX_LEN:, base:base + nf] = -s
        sin[CTX_LEN:, base + nf:base + 2 * nf] = s
    return jnp.asarray(cos), jnp.asarray(sin)


def _mla_weights(w_q_up, w_kv_up):
    r = w_q_up.shape[0]
    wq = w_q_up.reshape(r, MLA_HEADS, MLA_NOPE + MLA_ROPE)
    nope, rope = wq[..., :MLA_NOPE], wq[..., MLA_NOPE:]
    q16 = MLA_ROPE // 4
    ropep = jnp.concatenate([rope[..., q16:2 * q16], rope[..., :q16], rope[..., 3 * q16:], rope[..., 2 * q16:3 * q16]],
                            axis=-1)
    zpad = jnp.zeros((r, MLA_HEADS, LANE - MLA_ROPE), w_q_up.dtype)
    wq1 = jnp.concatenate([nope, rope, zpad], axis=-1).reshape(r, MLA_HEADS * 2 * LANE).astype(BF16)
    wq2 = jnp.concatenate([ropep, zpad], axis=-1).reshape(r, MLA_HEADS * LANE).astype(BF16)
    rk = w_kv_up.shape[0]
    wkv = w_kv_up.reshape(rk, MLA_HEADS, 2, MLA_NOPE).transpose(0, 2, 1, 3).reshape(rk, 2 * MLA_HEADS * MLA_NOPE)
    return wq1, wq2, wkv.astype(BF16)


def _moe_plan(idx, rank, counts, n_tiles):
    cnt = counts[:, 0].astype(jnp.int32)
    padded = ((cnt + MOE_G - 1) // MOE_G) * MOE_G
    ends = jnp.cumsum(padded)
    starts = ends - padded
    e_ids = jnp.arange(N_EXPERTS, dtype=jnp.int32)
    dest = jnp.sum(jnp.where(idx[..., None] == e_ids, starts, 0), axis=-1) + rank
    tile_start = jnp.arange(n_tiles, dtype=jnp.int32) * MOE_G
    te = jnp.sum((tile_start[:, None] >= ends[None, :]).astype(jnp.int32), axis=1)
    active = te < N_EXPERTS
    te_c = jnp.minimum(te, N_EXPERTS - 1)
    tile_is = te_c[:, None] == e_ids[None, :]
    cnt_t = jnp.sum(jnp.where(tile_is, cnt, 0), axis=1)
    start_t = jnp.sum(jnp.where(tile_is, starts, 0), axis=1)
    valid = jnp.clip(cnt_t - (tile_start - start_t), 0, MOE_G)
    valid = jnp.where(active, valid, 0)
    last_e = jnp.max(jnp.where(cnt > 0, jnp.arange(N_EXPERTS, dtype=jnp.int32), 0))
    te_f = jnp.where(active, te_c, last_e)
    tail_blocks = (n_tiles * MOE_G - ends[-1]) // (MOE_G // 2)
    pad_start = jnp.concatenate([starts + cnt, ends[-1:]]).astype(jnp.int32)
    pad_len = jnp.concatenate([padded - cnt, tail_blocks[None]]).astype(jnp.int32)
    return dest, te_f, valid, pad_start, pad_len


def kernel(x, c, ctx, c_ctx, norm1_g, norm2_g, ada_w, ada_b, w_in, b_gate, gla_wa, gla_ba, gla_norm_g,
           na_rpb, mla_q_norm_g, mla_w_q_up, mla_kv_norm_g, mla_w_kv_up, w_branch_gla, w_branch_na,
           w_branch_mla, w_out, router_w, router_b, moe_w1, moe_b1, moe_w2, moe_b2, final_norm_g):
    b, n, d = x.shape
    rb = CTX_LEN + n
    assert b <= 8 and d == D_MODEL and ctx.shape[1] == CTX_LEN

    cc = jnp.zeros((16, d), F32).at[:b].set(c).at[8].set(c_ctx)
    mod_all = _ada_mod(cc, ada_w, ada_b)
    cos_t, sin_t = _rope_tables(rb)
    xs = jnp.concatenate([ctx, x], axis=1)

    for l in range(DEPTH):
        last = l == DEPTH - 1
        mod = mod_all[l].reshape(16 * 6, 1, d)
        if l == 0:
            h = _norm_mod(xs, norm1_g[l], mod)
        p = _matmul(h.reshape(b * rb, d), _proj_weight(w_in[l]), 1024 if (b * rb) % 1024 == 0 else ROW_TILE,
                    PROJ_TN, BF16)
        p3 = p.reshape(b, rb, PROJ_W)

        zpad = jnp.zeros((LANE - 2 * GLA_GATE_RANK, GLA_QK_W), F32)
        zr = jnp.zeros((GLA_GATE_RANK, GLA_QK_W), F32)
        waf = jnp.concatenate([gla_wa[l, 0], zr, zpad], axis=0).astype(BF16)
        wab = jnp.concatenate([zr, gla_wa[l, 1], zpad], axis=0).astype(BF16)
        og = _gla(p3, waf, wab, gla_ba[l], gla_norm_g[l].reshape(1, GLA_V_W))

        on = _na(p3, _na_bias_table(na_rpb[l], n // GRID_W))

        wq1, wq2, wkv = _mla_weights(mla_w_q_up[l], mla_w_kv_up[l])
        q_m, k_m, v_m = _mla_proj(p3, cos_t, sin_t, mla_q_norm_g[l].reshape(1, -1), mla_kv_norm_g[l].reshape(1, -1),
                                  wq1, wq2, wkv)
        om = _mla_attn(q_m, k_m, v_m, ctx_queries=False)
        om_ctx = None if last else _mla_attn(q_m, k_m, v_m, ctx_queries=True)

        xs, hp, logits_t = _merge(
            xs, p3, og, on, om, om_ctx, b_gate[l].reshape(1, -1),
            w_branch_gla[l].astype(BF16), w_branch_na[l].astype(BF16), w_branch_mla[l].astype(BF16),
            w_out[l].astype(BF16), mod, norm2_g[l].reshape(1, d), router_w[l].T,
            jnp.broadcast_to(router_b[l][:, None], (N_EXPERTS, LANE)), skip_ctx=last)

        t_tok = hp.shape[0]
        idx, wts, rank, counts = _route(logits_t)
        n_tiles = (TOP_K * t_tok) // MOE_G + N_EXPERTS
        dest, te, tv, pad_start, pad_len = _moe_plan(idx, rank, counts, n_tiles)
        xg = _dispatch(hp, dest, pad_start, pad_len, n_tiles * MOE_G)
        yg = _ffn(l, te, tv, xg, moe_w1, moe_b1, moe_w2, moe_b2)
        if last:
            (xs,) = _combine(xs, yg, dest, wts, mod, final_norm_g, mod, ctx_first=False, final_norm=True)
        else:
            xs, h = _combine(xs, yg, dest, wts, mod, norm1_g[l + 1], mod_all[l + 1].reshape(16 * 6, 1, d),
                             ctx_first=True, final_norm=False)
    return xs
```

```python
import functools

import numpy as np
import jax
import jax.numpy as jnp
from jax import lax
from jax.experimental import pallas as pl
from jax.experimental.pallas import tpu as pltpu

F32 = jnp.float32
BF16 = jnp.bfloat16

D_MODEL = 2048
DEPTH = 2
GRID_W = 64
CTX_LEN = 256
EPS = 1e-6
ROPE_BASE = 10000.0
NEG_INF = -1e30

GLA_HEADS = 4
GLA_DK = 64
GLA_DV = 128
GLA_GATE_RANK = 16
GLA_TAU = 16.0
GLA_CHUNK = 64
NA_HEADS = 8
NA_HEAD_DIM = 64
NA_KH = 8
NA_KW = 16
NA_SCALE = NA_HEAD_DIM ** -0.5
MLA_HEADS = 8
MLA_Q_RANK = 512
MLA_KV_RANK = 512
MLA_NOPE = 128
MLA_ROPE = 64
MLA_V = 128
MLA_SCALE = (MLA_NOPE + MLA_ROPE) ** -0.5
N_BRANCH = 3
N_EXPERTS = 32
TOP_K = 4
EXPERT_FF = D_MODEL
SWIGLU_LIMIT = 7.0
SWIGLU_ALPHA = 1.702

GLA_QK_W = GLA_HEADS * GLA_DK
GLA_V_W = GLA_HEADS * GLA_DV
NA_W = NA_HEADS * NA_HEAD_DIM
MLA_V_W = MLA_HEADS * MLA_V

LANE = 128
SUBLANE = 8
ROW_TILE = 256
VMEM_LIMIT = 56 * 1024 * 1024
FFN_VMEM_LIMIT = 62 * 1024 * 1024

COL_GATE = 0
COL_GQ = COL_GATE + N_BRANCH * D_MODEL
COL_GK = COL_GQ + GLA_QK_W
COL_GV = COL_GK + GLA_QK_W
COL_GR = COL_GV + GLA_V_W
COL_NQ = COL_GR + GLA_V_W
COL_NK = COL_NQ + NA_W
COL_NV = COL_NK + NA_W
COL_MCQ = COL_NV + NA_W
COL_MCKV = COL_MCQ + MLA_Q_RANK
COL_KR = COL_MCKV + MLA_KV_RANK
COL_KRP = COL_KR + LANE
COL_GA = COL_KRP + LANE
PROJ_TN = 1536
PROJ_W = 7 * PROJ_TN
assert COL_GA + LANE <= PROJ_W

MOE_G = 1024
MOE_SB = 256
MOE_CHAIN = 1024
MOE_FC = 512
DISPATCH_TT = 256
COMBINE_TT = 128
ROUTE_TT = 256
MLA_TQ = 2048
MLA_CHAIN = 256


def _cparams(sem, vmem_limit=VMEM_LIMIT):
    return pltpu.CompilerParams(dimension_semantics=sem, vmem_limit_bytes=vmem_limit)


def _nt(a, b):
    return lax.dot_general(a, b, (((1,), (1,)), ((), ())), preferred_element_type=F32)


def _tn(a, b):
    return lax.dot_general(a, b, (((0,), (0,)), ((), ())), preferred_element_type=F32)


def _mm(a, b):
    return jnp.dot(a, b, preferred_element_type=F32)


def _sigmoid(x):
    return 1.0 / (1.0 + jnp.exp(-x))


def _split_hi_lo(x):
    hi = x.astype(BF16)
    lo = (x - hi.astype(F32)).astype(BF16)
    return hi, lo


def _ada_kernel(c_ref, w_ref, b_ref, o_ref):
    c = c_ref[...]
    s = (c * _sigmoid(c)).astype(BF16)
    o_ref[0] = _mm(s, w_ref[0].astype(BF16)) + b_ref[0]


def _ada_mod(cc, ada_w, ada_b):
    nl, d, n6 = ada_w.shape
    tn = 1024
    return pl.pallas_call(
        _ada_kernel,
        grid=(nl, n6 // tn),
        in_specs=[pl.BlockSpec((16, d), lambda l, j: (0, 0)),
                  pl.BlockSpec((1, d, tn), lambda l, j: (l, 0, j)),
                  pl.BlockSpec((1, 1, tn), lambda l, j: (l, 0, j))],
        out_specs=pl.BlockSpec((1, 16, tn), lambda l, j: (l, 0, j)),
        out_shape=jax.ShapeDtypeStruct((nl, 16, n6), F32),
        compiler_params=_cparams(("arbitrary", "arbitrary")),
        name="ada_mod",
    )(cc, ada_w, ada_b.reshape(nl, 1, n6))


def _mod_spec(part, ctx_first):
    def imap(b, j):
        row = jnp.where(j == 0, 8, b) if ctx_first else b
        return (row * 6 + part, 0, 0)
    return pl.BlockSpec((1, 1, D_MODEL), imap)


def _norm_mod_kernel(ctx_ref, x_ref, g_ref, sc_ref, sh_ref, xs_ref, o_ref):
    x = jnp.where(pl.program_id(1) == 0, ctx_ref[0], x_ref[0])
    xs_ref[0] = x
    ms = jnp.mean(x * x, axis=-1, keepdims=True)
    y = x * lax.rsqrt(ms + EPS) * g_ref[...]
    o_ref[0] = (y * (1.0 + sc_ref[0]) + sh_ref[0]).astype(o_ref.dtype)


def _norm_mod(ctx, x, g, mod):
    b, n, d = x.shape
    rb = ctx.shape[1] + n
    assert ctx.shape[1] == ROW_TILE
    row_spec = pl.BlockSpec((1, ROW_TILE, d), lambda i, j: (i, j, 0))
    return pl.pallas_call(
        _norm_mod_kernel,
        grid=(b, rb // ROW_TILE),
        in_specs=[pl.BlockSpec((1, ROW_TILE, d), lambda i, j: (i, 0, 0)),
                  pl.BlockSpec((1, ROW_TILE, d), lambda i, j: (i, jnp.maximum(j - 1, 0), 0)),
                  pl.BlockSpec((1, d), lambda i, j: (0, 0)),
                  _mod_spec(1, True), _mod_spec(0, True)],
        out_specs=[row_spec, row_spec],
        out_shape=[jax.ShapeDtypeStruct((b, rb, d), F32), jax.ShapeDtypeStruct((b, rb, d), BF16)],
        compiler_params=_cparams(("arbitrary", "arbitrary")),
        name="norm_mod",
    )(ctx, x, g.reshape(1, d), mod, mod)


def _matmul_kernel(x_ref, w_ref, o_ref):
    o_ref[...] = _mm(x_ref[...], w_ref[...]).astype(o_ref.dtype)


def _matmul(x, w, tm, tn, out_dtype):
    m, k = x.shape
    n = w.shape[1]
    return pl.pallas_call(
        _matmul_kernel,
        grid=(n // tn, m // tm),
        in_specs=[pl.BlockSpec((tm, k), lambda j, i: (i, 0)),
                  pl.BlockSpec((k, tn), lambda j, i: (0, j))],
        out_specs=pl.BlockSpec((tm, tn), lambda j, i: (i, j)),
        out_shape=jax.ShapeDtypeStruct((m, n), out_dtype),
        compiler_params=_cparams(("arbitrary", "arbitrary")),
        name="in_proj",
    )(x, w)


def _gla_kernel(q_ref, k_ref, v_ref, r_ref, ab_ref, waf_ref, wab_ref, ba_ref, g_ref, o_ref,
                laf_ref, lab_ref, of_ref, ob_ref, st_ref, *, n_ctx_chunks, n_chunks):
    c = GLA_CHUNK
    ab = ab_ref[0]
    zf = _mm(ab, waf_ref[...]) + ba_ref[0:1, :]
    zb = _mm(ab, wab_ref[...]) + ba_ref[1:2, :]
    laf_ref[...] = (jnp.minimum(zf, 0.0) - jnp.log(1.0 + jnp.exp(-jnp.abs(zf)))) * (1.0 / GLA_TAU)
    lab_ref[...] = (jnp.minimum(zb, 0.0) - jnp.log(1.0 + jnp.exp(-jnp.abs(zb)))) * (1.0 / GLA_TAU)
    st_ref[...] = jnp.zeros_like(st_ref)

    ri = lax.broadcasted_iota(jnp.int32, (c, c), 0)
    ci = lax.broadcasted_iota(jnp.int32, (c, c), 1)
    tri = [(ri >= ci), (ri <= ci)]
    tri_bf = [t.astype(F32).astype(BF16) for t in tri]
    tri4 = [jnp.concatenate([t] * GLA_HEADS, axis=0) for t in tri]
    lane_q = lax.broadcasted_iota(jnp.int32, (c, GLA_QK_W), 1) // GLA_DK
    lane_v = lax.broadcasted_iota(jnp.int32, (c, GLA_V_W), 1) // GLA_DV
    st_row_h = lax.broadcasted_iota(jnp.int32, (GLA_V_W, GLA_QK_W), 0) // GLA_DV
    st_col_h = lax.broadcasted_iota(jnp.int32, (GLA_V_W, GLA_QK_W), 1) // GLA_DK
    st_mask = st_row_h == st_col_h

    def one_dir(d, chunk, la_ref, out_ref):
        r0 = pl.multiple_of(chunk * c, c)
        la = la_ref[pl.ds(r0, c), :]
        hi, lo = _split_hi_lo(la)
        bc = _mm(tri_bf[d], hi) + _mm(tri_bf[d], lo)
        btot = bc[c - 1:c, :] if d == 0 else bc[0:1, :]
        q = q_ref[0, pl.ds(r0, c), :].astype(F32) * (GLA_DK ** -0.5)
        k = k_ref[0, pl.ds(r0, c), :].astype(F32)
        v = v_ref[0, pl.ds(r0, c), :]
        qd = (q * jnp.exp(bc)).astype(BF16)
        ki = (k * jnp.exp(-bc)).astype(BF16)
        kd = (k * jnp.exp(btot - bc)).astype(BF16)
        zero = jnp.zeros_like(qd)
        qs = jnp.concatenate([jnp.where(lane_q == h, qd, zero) for h in range(GLA_HEADS)], axis=0)
        att = _nt(qs, ki)
        att = jnp.where(tri4[d], att, 0.0).astype(BF16)
        rr = _mm(att, v)
        o = _nt(qd, st_ref[d].astype(BF16))
        for h in range(GLA_HEADS):
            o = o + jnp.where(lane_v == h, rr[h * c:(h + 1) * c, :], 0.0)
        out_ref[pl.ds(r0, c), :] = o
        upd = _tn(v, kd)
        st_ref[d] = jnp.where(st_mask, st_ref[d] * jnp.exp(btot) + upd, 0.0)

    def step(i, carry):
        one_dir(0, i, laf_ref, of_ref)
        cb = jnp.where(i < n_ctx_chunks, n_ctx_chunks - 1 - i, n_chunks + n_ctx_chunks - 1 - i)
        one_dir(1, cb, lab_ref, ob_ref)
        return carry

    lax.fori_loop(0, n_chunks, step, 0, unroll=4)

    def epilogue(j, carry):
        r0 = pl.multiple_of(j * ROW_TILE, ROW_TILE)
        o = of_ref[pl.ds(r0, ROW_TILE), :] + ob_ref[pl.ds(r0, ROW_TILE), :]
        r = r_ref[0, pl.ds(r0, ROW_TILE), :].astype(F32)
        gate = r * _sigmoid(r)
        for h in range(GLA_HEADS):
            sl = slice(h * GLA_DV, (h + 1) * GLA_DV)
            oh = o[:, sl]
            ms = jnp.mean(oh * oh, axis=-1, keepdims=True)
            y = oh * lax.rsqrt(ms + EPS) * g_ref[:, sl]
            o_ref[0, pl.ds(r0, ROW_TILE), sl] = (y * gate[:, sl]).astype(o_ref.dtype)
        return carry

    lax.fori_loop(0, (n_chunks * c) // ROW_TILE, epilogue, 0)


def _gla(p3, waf, wab, ba, g):
    b, rb, _ = p3.shape
    n_chunks = rb // GLA_CHUNK
    kern = functools.partial(_gla_kernel, n_ctx_chunks=CTX_LEN // GLA_CHUNK, n_chunks=n_chunks)

    def col(width, off):
        return pl.BlockSpec((1, rb, width), lambda i: (i, 0, off // width))

    return pl.pallas_call(
        kern,
        grid=(b,),
        in_specs=[col(GLA_QK_W, COL_GQ), col(GLA_QK_W, COL_GK), col(GLA_V_W, COL_GV), col(GLA_V_W, COL_GR),
                  col(LANE, COL_GA),
                  pl.BlockSpec((LANE, GLA_QK_W), lambda i: (0, 0)),
                  pl.BlockSpec((LANE, GLA_QK_W), lambda i: (0, 0)),
                  pl.BlockSpec((2, GLA_QK_W), lambda i: (0, 0)),
                  pl.BlockSpec((1, GLA_V_W), lambda i: (0, 0))],
        out_specs=pl.BlockSpec((1, rb, GLA_V_W), lambda i: (i, 0, 0)),
        out_shape=jax.ShapeDtypeStruct((b, rb, GLA_V_W), BF16),
        scratch_shapes=[pltpu.VMEM((rb, GLA_QK_W), F32), pltpu.VMEM((rb, GLA_QK_W), F32),
                        pltpu.VMEM((rb, GLA_V_W), F32), pltpu.VMEM((rb, GLA_V_W), F32),
                        pltpu.VMEM((2, GLA_V_W, GLA_QK_W), F32)],
        compiler_params=_cparams(("arbitrary",)),
        name="gla",
    )(p3, p3, p3, p3, p3, waf, wab, ba, g)


def _softmax_pv(parts):
    m = parts[0][0].max(axis=-1, keepdims=True)
    for s, _ in parts[1:]:
        m = jnp.maximum(m, s.max(axis=-1, keepdims=True))
    acc = None
    den = None
    for s, v in parts:
        e = jnp.exp(s - m)
        l = e.sum(axis=-1, keepdims=True)
        pv = _mm(e.astype(BF16), v)
        acc = pv if acc is None else acc + pv
        den = l if den is None else den + l
    return acc / den


assert float(np.log2(NA_SCALE)).is_integer()
NA_QR = 4
NA_KR = NA_QR + NA_KH


def _na_block_rows(n_rows):
    r0s = (0, 2 * NA_QR, n_rows - NA_QR)
    return [(r0, int(np.clip(r0 - NA_KH // 2, 0, n_rows - NA_KR))) for r0 in r0s]


def _na_kernel(q_ref, k_ref, v_ref, bias_ref, o_ref, *, n_rows):
    nq = NA_QR * GRID_W
    nk = NA_KR * GRID_W
    n_blocks = n_rows // NA_QR
    lane = lax.broadcasted_iota(jnp.int32, (nq, LANE), 1)
    head_mask = [lane < NA_HEAD_DIM, lane >= NA_HEAD_DIM]
    kc = k_ref[0, 0:CTX_LEN, :]
    vc = v_ref[0, 0:CTX_LEN, :]

    def block(bi, carry):
        r0 = bi * NA_QR
        ks = jnp.clip(r0 - NA_KH // 2, 0, n_rows - NA_KR)
        var = jnp.where(bi == 0, 0, jnp.where(bi == n_blocks - 1, 2, 1))
        q0 = pl.multiple_of(CTX_LEN + r0 * GRID_W, GRID_W)
        k0 = pl.multiple_of(CTX_LEN + ks * GRID_W, GRID_W)
        q = q_ref[0, pl.ds(q0, nq), :] * NA_SCALE
        kl = k_ref[0, pl.ds(k0, nk), :]
        vl = v_ref[0, pl.ds(k0, nk), :]
        outs = []
        for hh in range(2):
            qm = jnp.where(head_mask[hh], q, jnp.zeros_like(q))
            s_loc = _nt(qm, kl) + bias_ref[hh, var]
            s_ctx = _nt(qm, kc)
            outs.append(_softmax_pv([(s_loc, vl), (s_ctx, vc)]))
        o_ref[0, pl.ds(q0, nq), :] = jnp.where(head_mask[0], outs[0], outs[1]).astype(o_ref.dtype)
        return carry

    lax.fori_loop(0, n_blocks, block, 0, unroll=2)

    qc = q_ref[0, 0:CTX_LEN, :]
    lane_c = lax.broadcasted_iota(jnp.int32, (CTX_LEN, LANE), 1)
    outs = []
    for hh in range(2):
        msk = (lane_c < NA_HEAD_DIM) if hh == 0 else (lane_c >= NA_HEAD_DIM)
        qm = jnp.where(msk, qc, jnp.zeros_like(qc))
        outs.append(_softmax_pv([(_nt(qm, kc) * NA_SCALE, vc)]))
    o_ref[0, 0:CTX_LEN, :] = jnp.where(lane_c < NA_HEAD_DIM, outs[0], outs[1]).astype(o_ref.dtype)


def _na_bias_table(rpb, n_rows):
    qcol = np.arange(GRID_W)[:, None]
    kcol = np.arange(GRID_W)[None, :]
    wstart = np.clip(qcol - NA_KW // 2, 0, GRID_W - NA_KW)
    in_win = (kcol >= wstart) & (kcol < wstart + NA_KW)
    dc = kcol - qcol + NA_KW - 1
    col_hot = ((dc[:, :, None] == np.arange(2 * NA_KW - 1)) & in_win[:, :, None]).astype(np.float32)
    geo = _na_block_rows(n_rows)
    row_hot = np.zeros((len(geo), NA_QR, NA_KR, 2 * NA_KH - 1), np.float32)
    for v, (r0, ks) in enumerate(geo):
        for a in range(NA_QR):
            rs = int(np.clip(r0 + a - NA_KH // 2, 0, n_rows - NA_KH))
            for i in range(NA_KR):
                if rs <= ks + i < rs + NA_KH:
                    row_hot[v, a, i, ks + i - (r0 + a) + NA_KH - 1] = 1.0
    t = jnp.einsum('hrc,vair,qkc->hvaqik', rpb.astype(F32), jnp.asarray(row_hot), jnp.asarray(col_hot),
                   precision=lax.Precision.HIGHEST)
    inside = (row_hot.sum(-1) > 0)[:, :, None, :, None] & in_win[None, None, :, None, :]
    t = jnp.where(inside[None], t, NEG_INF)
    return t.reshape(NA_HEADS, len(geo), NA_QR * GRID_W, NA_KR * GRID_W)


def _na(p3, bias):
    b, rb, _ = p3.shape
    n_rows = (rb - CTX_LEN) // GRID_W
    kern = functools.partial(_na_kernel, n_rows=n_rows)

    def col(off):
        return pl.BlockSpec((1, rb, LANE), lambda p, i: (i, 0, off // LANE + p))

    return pl.pallas_call(
        kern,
        grid=(NA_HEADS // 2, b),
        in_specs=[col(COL_NQ), col(COL_NK), col(COL_NV),
                  pl.BlockSpec((2,) + bias.shape[1:], lambda p, i: (p, 0, 0, 0))],
        out_specs=pl.BlockSpec((1, rb, LANE), lambda p, i: (i, 0, p)),
        out_shape=jax.ShapeDtypeStruct((b, rb, NA_W), BF16),
        compiler_params=_cparams(("arbitrary", "arbitrary")),
        name="na",
    )(p3, p3, p3, bias)


def _mla_proj_kernel(cq_ref, ckv_ref, kr_ref, krp_ref, cos_ref, sin_ref, gq_ref, gkv_ref,
                     wq_ref, wq2_ref, wkv_ref, q_ref, k_ref, v_ref):
    def norm(x, g):
        x = x.astype(F32)
        ms = jnp.mean(x * x, axis=-1, keepdims=True)
        return (x * lax.rsqrt(ms + EPS) * g).astype(BF16)

    cos = cos_ref[...]
    sin = sin_ref[...]
    nq = norm(cq_ref[0], gq_ref[...])
    yq = _mm(nq, wq_ref[...])
    yq2 = _mm(nq, wq2_ref[...])
    nkv = norm(ckv_ref[0], gkv_ref[...])
    ykv = _mm(nkv, wkv_ref[...])
    k_rot = (kr_ref[0].astype(F32) * cos + krp_ref[0].astype(F32) * sin).astype(BF16)
    for h in range(MLA_HEADS):
        a = 2 * h * LANE
        q_ref[0, :, a:a + LANE] = (yq[:, a:a + LANE] * MLA_SCALE).astype(BF16)
        rot = yq[:, a + LANE:a + 2 * LANE] * cos + yq2[:, h * LANE:(h + 1) * LANE] * sin
        q_ref[0, :, a + LANE:a + 2 * LANE] = (rot * MLA_SCALE).astype(BF16)
        k_ref[0, :, a:a + LANE] = ykv[:, h * LANE:(h + 1) * LANE].astype(BF16)
        k_ref[0, :, a + LANE:a + 2 * LANE] = k_rot
        v0 = MLA_HEADS * MLA_NOPE + h * MLA_V
        v_ref[0, :, a:a + LANE] = ykv[:, v0:v0 + MLA_V].astype(BF16)
        v_ref[0, :, a + LANE:a + 2 * LANE] = jnp.ones((ykv.shape[0], LANE), BF16)


def _mla_proj(p3, cos_t, sin_t, gq, gkv, wq, wq2, wkv):
    b, rb, _ = p3.shape
    tm = ROW_TILE

    def col(width, off):
        return pl.BlockSpec((1, tm, width), lambda i, j: (i, j, off // width))

    def full(a):
        return pl.BlockSpec(a.shape, lambda i, j: (0, 0))

    hw = MLA_HEADS * 2 * LANE
    nj = rb // tm
    return pl.pallas_call(
        _mla_proj_kernel,
        grid=(b, nj),
        in_specs=[col(MLA_Q_RANK, COL_MCQ), col(MLA_KV_RANK, COL_MCKV), col(LANE, COL_KR), col(LANE, COL_KRP),
                  pl.BlockSpec((tm, LANE), lambda i, j: (j, 0)), pl.BlockSpec((tm, LANE), lambda i, j: (j, 0)),
                  full(gq), full(gkv), full(wq), full(wq2), full(wkv)],
        out_specs=[pl.BlockSpec((1, tm, hw), lambda i, j: (i, (j + nj - 1) % nj, 0)),
                   pl.BlockSpec((1, tm, hw), lambda i, j: (i, (j + nj - 1) % nj, 0)),
                   pl.BlockSpec((1, tm, hw), lambda i, j: (i, (j + nj - 1) % nj, 0))],
        out_shape=[jax.ShapeDtypeStruct((b, rb, hw), BF16), jax.ShapeDtypeStruct((b, rb, hw), BF16),
                   jax.ShapeDtypeStruct((b, rb, hw), BF16)],
        compiler_params=_cparams(("arbitrary", "arbitrary")),
        name="mla_proj",
    )(p3, p3, p3, p3, cos_t, sin_t, gq, gkv, wq, wq2, wkv)


def _mla_attn_kernel(q_ref, k_ref, v_ref, o_ref):
    k = k_ref[0]
    v = v_ref[0]
    tq = q_ref.shape[1]
    sub = min(tq, MLA_CHAIN)
    for r0 in range(0, tq, sub):
        s = _nt(q_ref[0, r0:r0 + sub, :], k)
        e = jnp.exp((s - s.max(axis=-1, keepdims=True)).astype(BF16))
        pv = _mm(e, v)
        o_ref[0, r0:r0 + sub, :] = (pv[:, :MLA_V] / pv[:, MLA_V:]).astype(o_ref.dtype)


def _mla_attn(q, k, v, ctx_queries):
    b, rb, _ = q.shape
    n_lat = rb - CTX_LEN
    if ctx_queries:
        tq, nq, n_keys = CTX_LEN, 1, CTX_LEN
        q_off = kv_blk = n_lat // CTX_LEN
    else:
        tq = MLA_TQ if n_lat % MLA_TQ == 0 else ROW_TILE
        nq, n_keys, q_off, kv_blk = n_lat // tq, rb, 0, 0
    return pl.pallas_call(
        _mla_attn_kernel,
        grid=(b, MLA_HEADS, nq),
        in_specs=[pl.BlockSpec((1, tq, 2 * LANE), lambda i, h, j: (i, j + q_off, h)),
                  pl.BlockSpec((1, n_keys, 2 * LANE), lambda i, h, j: (i, kv_blk, h)),
                  pl.BlockSpec((1, n_keys, 2 * LANE), lambda i, h, j: (i, kv_blk, h))],
        out_specs=pl.BlockSpec((1, tq, MLA_V), lambda i, h, j: (i, j, h)),
        out_shape=jax.ShapeDtypeStruct((b, nq * tq, MLA_V_W), BF16),
        compiler_params=_cparams(("arbitrary", "arbitrary", "arbitrary")),
        name="mla_attn_ctx" if ctx_queries else "mla_attn",
    )(q, k, v)


def _merge_kernel(x_ref, oa_ref, ob_ref, oc_ref, occ_ref, ga_ref, gb_ref, gc_ref, bg_ref,
                  wa_ref, wb_ref, wc_ref, wo_ref, gt1_ref, sc2_ref, sh2_ref, g2_ref, rw_ref, rb_ref,
                  xo_ref, h_ref, lg_ref, *, ctx_tile):
    d = D_MODEL

    wh, wl = _split_hi_lo(rw_ref[...])
    tm = x_ref.shape[1]
    sub = tm
    for r0 in range(0, tm, sub):
        rows = slice(r0, r0 + sub)

        def gate(g_ref, k):
            return _sigmoid(g_ref[0, rows, :].astype(F32) + bg_ref[:, k * d:(k + 1) * d])

        oc = oc_ref[0, rows, :]
        if ctx_tile:
            oc = jnp.where(pl.program_id(1) == 0, occ_ref[0, rows, :], oc)
        m = gate(ga_ref, 0) * _mm(oa_ref[0, rows, :], wa_ref[...])
        m = m + gate(gb_ref, 1) * _mm(ob_ref[0, rows, :], wb_ref[...])
        m = m + gate(gc_ref, 2) * _mm(oc, wc_ref[...])
        y = _mm(m.astype(BF16), wo_ref[...])
        x = x_ref[0, rows, :] + gt1_ref[0] * y
        xo_ref[0, rows, :] = x
        ms = jnp.mean(x * x, axis=-1, keepdims=True)
        h = x * lax.rsqrt(ms + EPS) * g2_ref[...]
        h = h * (1.0 + sc2_ref[0]) + sh2_ref[0]
        h_ref[rows, :] = h
        hh, hl = _split_hi_lo(h)
        lg_ref[:, rows] = _nt(wh, hh) + _nt(wh, hl) + _nt(wl, hh) + rb_ref[:, 0:1]


def _merge(x, p3, og, on, om, om_ctx, bg, wa, wb, wc, wo, mod, g2, rwt, rbias, skip_ctx):
    b, rb, d = x.shape
    tm = ROW_TILE
    jo = 1 if skip_ctx else 0
    nj = rb // tm - jo
    rows_out = nj * tm

    def rows(width, cblk=0):
        return pl.BlockSpec((1, tm, width), lambda i, j: (i, j + jo, cblk))

    om_spec = pl.BlockSpec((1, tm, MLA_V_W), lambda i, j: (i, jnp.maximum(j + jo - 1, 0), 0))
    omc_spec = pl.BlockSpec((1, tm, MLA_V_W), lambda i, j: (i, 0, 0))
    kern = functools.partial(_merge_kernel, ctx_tile=not skip_ctx)

    def full(a):
        return pl.BlockSpec(a.shape, lambda i, j: (0,) * a.ndim, pipeline_mode=pl.Buffered(1))

    def mod_spec(part):
        def imap(i, j):
            row = i if skip_ctx else jnp.where(j == 0, 8, i)
            return (row * 6 + part, 0, 0)
        return pl.BlockSpec((1, 1, d), imap)

    return pl.pallas_call(
        kern,
        grid=(b, nj),
        in_specs=[rows(d), rows(GLA_V_W), rows(NA_W), om_spec, omc_spec,
                  rows(d, 0), rows(d, 1), rows(d, 2), full(bg),
                  full(wa), full(wb), full(wc), full(wo),
                  mod_spec(2), mod_spec(4), mod_spec(3), full(g2), full(rwt), full(rbias)],
        out_specs=[pl.BlockSpec((1, tm, d), lambda i, j: (i, j, 0)),
                   pl.BlockSpec((tm, d), lambda i, j: (i * nj + j, 0)),
                   pl.BlockSpec((N_EXPERTS, tm), lambda i, j: (0, i * nj + j))],
        out_shape=[jax.ShapeDtypeStruct((b, rows_out, d), F32),
                   jax.ShapeDtypeStruct((b * rows_out, d), F32),
                   jax.ShapeDtypeStruct((N_EXPERTS, b * rows_out), F32)],
        compiler_params=_cparams(("arbitrary", "arbitrary")),
        name="merge",
    )(x, og, on, om, om if skip_ctx else om_ctx, p3, p3, p3, bg, wa, wb, wc, wo, mod, mod, mod, g2, rwt, rbias)


def _route_kernel(l_ref, idx_ref, w_ref, rank_ref, cnt_ref, carry_ref):
    i = pl.program_id(0)
    tt = l_ref.shape[1]

    @pl.when(i == 0)
    def _():
        carry_ref[...] = jnp.zeros_like(carry_ref)

    l = l_ref[...]
    eio = lax.broadcasted_iota(jnp.int32, (N_EXPERTS, tt), 0)
    vals, idxs = [], []
    for _ in range(TOP_K):
        m = l.max(axis=0, keepdims=True)
        ik = jnp.min(jnp.where(l == m, eio, N_EXPERTS), axis=0, keepdims=True)
        vals.append(m)
        idxs.append(ik)
        l = jnp.where(eio == ik, -jnp.inf, l)
    es = [jnp.exp(v - vals[0]) for v in vals]
    den = es[0] + es[1] + es[2] + es[3]
    sel = jnp.zeros((N_EXPERTS, tt), F32)
    for ik in idxs:
        sel = sel + (eio == ik).astype(F32)
    si = lax.broadcasted_iota(jnp.int32, (tt, tt), 0)
    ti = lax.broadcasted_iota(jnp.int32, (tt, tt), 1)
    before = (si < ti).astype(F32).astype(BF16)
    rank_full = _mm(sel.astype(BF16), before) + carry_ref[:, 0:1]
    for k in range(TOP_K):
        idx_ref[k:k + 1, :] = idxs[k]
        w_ref[k:k + 1, :] = es[k] / den
        rk = jnp.sum(jnp.where(eio == idxs[k], rank_full, 0.0), axis=0, keepdims=True)
        rank_ref[k:k + 1, :] = rk.astype(jnp.int32)
    carry_ref[...] = carry_ref[...] + jnp.sum(sel, axis=1, keepdims=True)
    cnt_ref[...] = carry_ref[...]


def _route(logits_t):
    ne, t = logits_t.shape
    tt = ROUTE_TT
    spec4 = pl.BlockSpec((TOP_K, tt), lambda i: (0, i))
    return pl.pallas_call(
        _route_kernel,
        grid=(t // tt,),
        in_specs=[pl.BlockSpec((ne, tt), lambda i: (0, i))],
        out_specs=[spec4, spec4, spec4, pl.BlockSpec((ne, LANE), lambda i: (0, 0))],
        out_shape=[jax.ShapeDtypeStruct((TOP_K, t), jnp.int32), jax.ShapeDtypeStruct((TOP_K, t), F32),
                   jax.ShapeDtypeStruct((TOP_K, t), jnp.int32), jax.ShapeDtypeStruct((ne, LANE), F32)],
        scratch_shapes=[pltpu.VMEM((ne, LANE), F32)],
        compiler_params=_cparams(("arbitrary",)),
        name="route",
    )(logits_t)


def _pad_fill(ps_ref, pn_ref, zero_ref, xg_ref, sem, wait):
    def copy(pos, rows):
        cp = pltpu.make_async_copy(zero_ref.at[pl.ds(0, rows), :], xg_ref.at[pl.ds(pos, rows), :], sem)
        cp.wait() if wait else cp.start()

    def per_expert(e, carry):
        pos = ps_ref[e]
        head = (-pos) & (SUBLANE - 1)
        for r in range(SUBLANE - 1):
            @pl.when(r < head)
            def _(r=r):
                copy(pos + r, 1)

        pos = pos + head
        n = pn_ref[e] - head
        bit = MOE_G // 2
        while bit >= SUBLANE:
            on = (n & bit) != 0

            @pl.when(on)
            def _(pos=pos, bit=bit):
                copy(pl.multiple_of(pos, SUBLANE), bit)

            pos = pos + jnp.where(on, bit, 0)
            bit //= 2
        return carry

    lax.fori_loop(0, N_EXPERTS, per_expert, 0)

    zr = zero_ref.shape[0]

    def tail(i, carry):
        pos = pl.multiple_of(ps_ref[N_EXPERTS] + i * zr, zr)
        cp = pltpu.make_async_copy(zero_ref, xg_ref.at[pl.ds(pos, zr), :], sem)
        cp.wait() if wait else cp.start()
        return carry

    lax.fori_loop(0, pn_ref[N_EXPERTS], tail, 0)


def _dispatch_kernel(ps_ref, pn_ref, dest_ref, h_ref, xg_ref, zero_ref, sem, zsem):
    tt = h_ref.shape[0]

    @pl.when(pl.program_id(0) == 0)
    def _():
        zero_ref[...] = jnp.zeros_like(zero_ref)
        _pad_fill(ps_ref, pn_ref, zero_ref, xg_ref, zsem, wait=False)

    def issue(t, carry):
        for k in range(TOP_K):
            pltpu.make_async_copy(h_ref.at[pl.ds(t, 1), :], xg_ref.at[pl.ds(dest_ref[0, k, t], 1), :], sem).start()
        return carry

    lax.fori_loop(0, tt, issue, 0)
    for k in range(TOP_K):
        pltpu.make_async_copy(h_ref, xg_ref.at[pl.ds(0, tt), :], sem).wait()

    @pl.when(pl.program_id(0) == 0)
    def _():
        _pad_fill(ps_ref, pn_ref, zero_ref, xg_ref, zsem, wait=True)


def _dispatch(hp, dest, pad_start, pad_len, n_slots):
    t, w = hp.shape
    tt = DISPATCH_TT
    dest3 = dest.reshape(TOP_K, t // tt, tt).transpose(1, 0, 2)
    grid_spec = pltpu.PrefetchScalarGridSpec(
        num_scalar_prefetch=2,
        grid=(t // tt,),
        in_specs=[pl.BlockSpec((1, TOP_K, tt), lambda i, ps, pn: (i, 0, 0), memory_space=pltpu.SMEM),
                  pl.BlockSpec((tt, w), lambda i, ps, pn: (i, 0))],
        out_specs=pl.BlockSpec(memory_space=pl.ANY),
        scratch_shapes=[pltpu.VMEM((MOE_G // 2, w), hp.dtype), pltpu.SemaphoreType.DMA(()),
                        pltpu.SemaphoreType.DMA(())],
    )
    return pl.pallas_call(
        _dispatch_kernel,
        grid_spec=grid_spec,
        out_shape=jax.ShapeDtypeStruct((n_slots, w), hp.dtype),
        compiler_params=_cparams(("arbitrary",)),
        name="dispatch",
    )(pad_start, pad_len, dest3, hp)


def _ffn_kernel(te_ref, tv_ref, x_ref, w1g_ref, w1l_ref, b1g_ref, b1l_ref, w2_ref, b2_ref, y_ref,
                xb_ref, *, n_fc, n_tiles):
    i = pl.program_id(0)
    j = pl.program_id(1)
    valid = tv_ref[i]
    g = y_ref.shape[0]
    chunk = x_ref.shape[0]

    @pl.when(i < n_tiles)
    def _():
        xb_ref[i % 2, pl.ds(pl.multiple_of(j * chunk, chunk), chunk), :] = x_ref[...].astype(BF16)

    cur = (i + 1) % 2

    @pl.when((valid > 0) & (j == 0))
    def _():
        y_ref[...] = jnp.broadcast_to(b2_ref[0, 0], y_ref.shape)

    for nr in range(MOE_SB, g + 1, MOE_SB):
        @pl.when((valid > nr - MOE_SB) & (valid <= nr))
        def _(nr=nr):
            wg = w1g_ref[0, 0].astype(BF16)
            wl = w1l_ref[0, 0].astype(BF16)
            w2 = w2_ref[0, 0].astype(BF16)
            for r0 in range(0, nr, MOE_CHAIN):
                r1 = min(r0 + MOE_CHAIN, nr)
                x = xb_ref[cur, r0:r1, :]
                ug = _mm(x, wg) + b1g_ref[0, 0]
                ul = _mm(x, wl) + b1l_ref[0, 0]
                xg = jnp.minimum(ug, SWIGLU_LIMIT)
                xl = jnp.clip(ul, -SWIGLU_LIMIT, SWIGLU_LIMIT)
                act = xg * _sigmoid(SWIGLU_ALPHA * xg) * (xl + 1.0)
                y_ref[r0:r1, :] = y_ref[r0:r1, :] + _mm(act.astype(BF16), w2)

    @pl.when((j == n_fc - 1) & (valid == 0))
    def _():
        y_ref[...] = jnp.zeros_like(y_ref)


def _ffn(layer, tile_expert, tile_valid, xg, w1, b1, w2, b2):
    n_slots, d = xg.shape
    nl, ne, _, ff2 = w1.shape
    ff = ff2 // 2
    n_fc = ff // MOE_FC
    n_tiles = n_slots // MOE_G
    chunk = MOE_G // n_fc
    kern = functools.partial(_ffn_kernel, n_fc=n_fc, n_tiles=n_tiles)
    tile_expert = jnp.concatenate([tile_expert[:1], tile_expert])
    tile_valid = jnp.concatenate([jnp.zeros((1,), tile_valid.dtype), tile_valid])

    def jj(j, tv, i):
        return jnp.where(tv[i] > 0, j, n_fc - 1)

    grid_spec = pltpu.PrefetchScalarGridSpec(
        num_scalar_prefetch=2,
        grid=(n_tiles + 1, n_fc),
        in_specs=[pl.BlockSpec((chunk, d), lambda i, j, te, tv: (jnp.minimum(i, n_tiles - 1) * n_fc + j, 0)),
                  pl.BlockSpec((1, 1, d, MOE_FC), lambda i, j, te, tv: (layer, te[i], 0, jj(j, tv, i))),
                  pl.BlockSpec((1, 1, d, MOE_FC), lambda i, j, te, tv: (layer, te[i], 0, jj(j, tv, i) + n_fc)),
                  pl.BlockSpec((1, 1, 1, MOE_FC), lambda i, j, te, tv: (layer, te[i], 0, jj(j, tv, i))),
                  pl.BlockSpec((1, 1, 1, MOE_FC), lambda i, j, te, tv: (layer, te[i], 0, jj(j, tv, i) + n_fc)),
                  pl.BlockSpec((1, 1, MOE_FC, d), lambda i, j, te, tv: (layer, te[i], jj(j, tv, i), 0)),
                  pl.BlockSpec((1, 1, 1, d), lambda i, j, te, tv: (layer, te[i], 0, 0))],
        out_specs=pl.BlockSpec((MOE_G, d), lambda i, j, te, tv: (jnp.maximum(i - 1, 0), 0)),
        scratch_shapes=[pltpu.VMEM((2, MOE_G, d), BF16)],
    )
    return pl.pallas_call(
        kern,
        grid_spec=grid_spec,
        out_shape=jax.ShapeDtypeStruct((n_slots, d), F32),
        compiler_params=_cparams(("arbitrary", "arbitrary"), FFN_VMEM_LIMIT),
        name="moe_ffn",
    )(tile_expert, tile_valid, xg, w1, w1, b1.reshape(nl, ne, 1, ff2), b1.reshape(nl, ne, 1, ff2), w2,
      b2.reshape(nl, ne, 1, d))


def _combine_kernel(dest_ref, destn_ref, x_ref, wt_ref, gt2_ref, gn_ref, sc_ref, sh_ref, yg_ref, o_ref, *rest,
                    final_norm):
    h_ref = None if final_norm else rest[0]
    buf_ref, sem = rest[-2:]
    i = pl.program_id(0)
    n = pl.num_programs(0)
    tt = x_ref.shape[1]
    slot = i % 2

    def issue(d_ref, s):
        def body(t, carry):
            for k in range(TOP_K):
                pltpu.make_async_copy(yg_ref.at[pl.ds(d_ref[0, k, t], 1), :],
                                      buf_ref.at[s, k, pl.ds(t, 1), :], sem.at[s]).start()
            return carry
        lax.fori_loop(0, tt, body, 0)

    @pl.when(i == 0)
    def _():
        issue(dest_ref, 0)

    @pl.when(i + 1 < n)
    def _():
        issue(destn_ref, 1 - slot)

    for k in range(TOP_K):
        pltpu.make_async_copy(yg_ref.at[pl.ds(0, tt), :], buf_ref.at[slot, k], sem.at[slot]).wait()

    wt = wt_ref[...]
    y = buf_ref[slot, 0] * wt[:, 0:1]
    for k in range(1, TOP_K):
        y = y + buf_ref[slot, k] * wt[:, k:k + 1]
    x = x_ref[0] + gt2_ref[0] * y
    ms = jnp.mean(x * x, axis=-1, keepdims=True)
    xn = x * lax.rsqrt(ms + EPS) * gn_ref[...]
    if final_norm:
        o_ref[0] = xn
    else:
        o_ref[0] = x
        h_ref[0] = (xn * (1.0 + sc_ref[0]) + sh_ref[0]).astype(h_ref.dtype)


def _combine(x, yg, dest, wts, mod, gn, mod_next, ctx_first, final_norm):
    b, rows, d = x.shape
    tt = COMBINE_TT
    nj = rows // tt
    t = b * rows
    nt = t // tt
    dest3 = dest.reshape(TOP_K, nt, tt).transpose(1, 0, 2)
    wt = wts.T
    kern = functools.partial(_combine_kernel, final_norm=final_norm)

    def mod_spec(part):
        def imap(i):
            bi = i // nj
            row = jnp.where((i % nj) * tt < CTX_LEN, 8, bi) if ctx_first else bi
            return (row * 6 + part, 0, 0)
        return pl.BlockSpec((1, 1, d), imap)

    row_spec = pl.BlockSpec((1, tt, d), lambda i: (i // nj, i % nj, 0))
    out_specs = [row_spec]
    out_shape = [jax.ShapeDtypeStruct((b, rows, d), F32)]
    if not final_norm:
        out_specs.append(row_spec)
        out_shape.append(jax.ShapeDtypeStruct((b, rows, d), BF16))

    return pl.pallas_call(
        kern,
        grid=(nt,),
        in_specs=[pl.BlockSpec((1, TOP_K, tt), lambda i: (i, 0, 0), memory_space=pltpu.SMEM),
                  pl.BlockSpec((1, TOP_K, tt), lambda i: (jnp.minimum(i + 1, nt - 1), 0, 0),
                               memory_space=pltpu.SMEM),
                  row_spec,
                  pl.BlockSpec((tt, TOP_K), lambda i: (i, 0)),
                  mod_spec(5),
                  pl.BlockSpec((1, d), lambda i: (0, 0)),
                  mod_spec(1), mod_spec(0),
                  pl.BlockSpec(memory_space=pl.ANY)],
        out_specs=out_specs,
        out_shape=out_shape,
        scratch_shapes=[pltpu.VMEM((2, TOP_K, tt, d), yg.dtype), pltpu.SemaphoreType.DMA((2,))],
        compiler_params=_cparams(("arbitrary",)),
        name="combine",
    )(dest3, dest3, x, wt, mod, gn.reshape(1, d), mod_next, mod_next, yg)


def _proj_weight(w_in):
    d = w_in.shape[0]
    splits = (GLA_QK_W, GLA_QK_W, GLA_V_W, GLA_V_W, GLA_GATE_RANK, GLA_GATE_RANK,
              NA_W, NA_W, NA_W, MLA_Q_RANK, MLA_KV_RANK, MLA_ROPE, N_BRANCH * D_MODEL)
    pts = np.cumsum((0,) + splits)
    (gq, gk, gv, gr, gaf, gab, nq, nk, nv, mcq, mckv, mkr, gate) = [w_in[:, pts[i]:pts[i + 1]] for i in range(13)]
    q16 = MLA_ROPE // 4
    mkrp = jnp.concatenate([mkr[:, q16:2 * q16], mkr[:, :q16], mkr[:, 3 * q16:], mkr[:, 2 * q16:3 * q16]], axis=1)
    z = lambda n: jnp.zeros((d, n), w_in.dtype)
    cols = [gate, gq, gk, gv, gr, nq, nk, nv, mcq, mckv,
            mkr, z(LANE - MLA_ROPE), mkrp, z(LANE - MLA_ROPE),
            gaf, gab, z(LANE - 2 * GLA_GATE_RANK)]
    w = jnp.concatenate(cols, axis=1)
    w = jnp.concatenate([w, z(PROJ_W - w.shape[1])], axis=1)
    return w.astype(BF16)


def _rope_tables(rb):
    n = rb - CTX_LEN
    t = np.arange(n)
    nf = MLA_ROPE // 4
    freqs = ROPE_BASE ** (-np.arange(nf, dtype=np.float64) / nf)
    cos = np.zeros((rb, LANE), np.float32)
    sin = np.zeros((rb, LANE), np.float32)
    cos[:CTX_LEN, :MLA_ROPE] = 1.0
    for a, pos in enumerate((t // GRID_W, t % GRID_W)):
        ang = (pos.astype(np.float32)[:, None] * freqs.astype(np.float32)[None, :]).astype(np.float32)
        c, s = np.cos(ang), np.sin(ang)
        base = a * 2 * nf
        cos[CTX_LEN:, base:base + nf] = c
        cos[CTX_LEN:, base + nf:base + 2 * nf] = c
        sin[CTX_LEN:, base:base + nf] = -s
        sin[CTX_LEN:, base + nf:base + 2 * nf] = s
    return jnp.asarray(cos), jnp.asarray(sin)


def _mla_weights(w_q_up, w_kv_up):
    r = w_q_up.shape[0]
    wq = w_q_up.reshape(r, MLA_HEADS, MLA_NOPE + MLA_ROPE)
    nope, rope = wq[..., :MLA_NOPE], wq[..., MLA_NOPE:]
    q16 = MLA_ROPE // 4
    ropep = jnp.concatenate([rope[..., q16:2 * q16], rope[..., :q16], rope[..., 3 * q16:], rope[..., 2 * q16:3 * q16]],
                            axis=-1)
    zpad = jnp.zeros((r, MLA_HEADS, LANE - MLA_ROPE), w_q_up.dtype)
    wq1 = jnp.concatenate([nope, rope, zpad], axis=-1).reshape(r, MLA_HEADS * 2 * LANE).astype(BF16)
    wq2 = jnp.concatenate([ropep, zpad], axis=-1).reshape(r, MLA_HEADS * LANE).astype(BF16)
    rk = w_kv_up.shape[0]
    wkv = w_kv_up.reshape(rk, MLA_HEADS, 2, MLA_NOPE).transpose(0, 2, 1, 3).reshape(rk, 2 * MLA_HEADS * MLA_NOPE)
    return wq1, wq2, wkv.astype(BF16)


def _moe_plan(idx, rank, counts, n_tiles):
    cnt = counts[:, 0].astype(jnp.int32)
    padded = ((cnt + MOE_G - 1) // MOE_G) * MOE_G
    ends = jnp.cumsum(padded)
    starts = ends - padded
    e_ids = jnp.arange(N_EXPERTS, dtype=jnp.int32)
    dest = jnp.sum(jnp.where(idx[..., None] == e_ids, starts, 0), axis=-1) + rank
    tile_start = jnp.arange(n_tiles, dtype=jnp.int32) * MOE_G
    te = jnp.sum((tile_start[:, None] >= ends[None, :]).astype(jnp.int32), axis=1)
    active = te < N_EXPERTS
    te_c = jnp.minimum(te, N_EXPERTS - 1)
    tile_is = te_c[:, None] == e_ids[None, :]
    cnt_t = jnp.sum(jnp.where(tile_is, cnt, 0), axis=1)
    start_t = jnp.sum(jnp.where(tile_is, starts, 0), axis=1)
    valid = jnp.clip(cnt_t - (tile_start - start_t), 0, MOE_G)
    valid = jnp.where(active, valid, 0)
    last_e = jnp.max(jnp.where(cnt > 0, jnp.arange(N_EXPERTS, dtype=jnp.int32), 0))
    te_f = jnp.where(active, te_c, last_e)
    tail_blocks = (n_tiles * MOE_G - ends[-1]) // (MOE_G // 2)
    pad_start = jnp.concatenate([starts + cnt, ends[-1:]]).astype(jnp.int32)
    pad_len = jnp.concatenate([padded - cnt, tail_blocks[None]]).astype(jnp.int32)
    return dest, te_f, valid, pad_start, pad_len


def kernel(x, c, ctx, c_ctx, norm1_g, norm2_g, ada_w, ada_b, w_in, b_gate, gla_wa, gla_ba, gla_norm_g,
           na_rpb, mla_q_norm_g, mla_w_q_up, mla_kv_norm_g, mla_w_kv_up, w_branch_gla, w_branch_na,
           w_branch_mla, w_out, router_w, router_b, moe_w1, moe_b1, moe_w2, moe_b2, final_norm_g):
    b, n, d = x.shape
    rb = CTX_LEN + n
    assert b <= 8 and d == D_MODEL and ctx.shape[1] == CTX_LEN

    cc = jnp.zeros((16, d), F32).at[:b].set(c).at[8].set(c_ctx)
    mod_all = _ada_mod(cc, ada_w, ada_b)
    cos_t, sin_t = _rope_tables(rb)

    for l in range(DEPTH):
        last = l == DEPTH - 1
        mod = mod_all[l].reshape(16 * 6, 1, d)
        if l == 0:
            xs, h = _norm_mod(ctx, x, norm1_g[l], mod)
        p = _matmul(h.reshape(b * rb, d), _proj_weight(w_in[l]), 1024 if (b * rb) % 1024 == 0 else ROW_TILE,
                    PROJ_TN, BF16)
        p3 = p.reshape(b, rb, PROJ_W)

        zpad = jnp.zeros((LANE - 2 * GLA_GATE_RANK, GLA_QK_W), F32)
        zr = jnp.zeros((GLA_GATE_RANK, GLA_QK_W), F32)
        waf = jnp.concatenate([gla_wa[l, 0], zr, zpad], axis=0).astype(BF16)
        wab = jnp.concatenate([zr, gla_wa[l, 1], zpad], axis=0).astype(BF16)
        og = _gla(p3, waf, wab, gla_ba[l], gla_norm_g[l].reshape(1, GLA_V_W))

        on = _na(p3, _na_bias_table(na_rpb[l], n // GRID_W))

        wq1, wq2, wkv = _mla_weights(mla_w_q_up[l], mla_w_kv_up[l])
        q_m, k_m, v_m = _mla_proj(p3, cos_t, sin_t, mla_q_norm_g[l].reshape(1, -1), mla_kv_norm_g[l].reshape(1, -1),
                                  wq1, wq2, wkv)
        om = _mla_attn(q_m, k_m, v_m, ctx_queries=False)
        om_ctx = None if last else _mla_attn(q_m, k_m, v_m, ctx_queries=True)

        xs, hp, logits_t = _merge(
            xs, p3, og, on, om, om_ctx, b_gate[l].reshape(1, -1),
            w_branch_gla[l].astype(BF16), w_branch_na[l].astype(BF16), w_branch_mla[l].astype(BF16),
            w_out[l].astype(BF16), mod, norm2_g[l].reshape(1, d), router_w[l].T,
            jnp.broadcast_to(router_b[l][:, None], (N_EXPERTS, LANE)), skip_ctx=last)

        t_tok = hp.shape[0]
        idx, wts, rank, counts = _route(logits_t)
        n_tiles = (TOP_K * t_tok) // MOE_G + N_EXPERTS
        dest, te, tv, pad_start, pad_len = _moe_plan(idx, rank, counts, n_tiles)
        xg = _dispatch(hp, dest, pad_start, pad_len, n_tiles * MOE_G)
        yg = _ffn(l, te, tv, xg, moe_w1, moe_b1, moe_w2, moe_b2)
        if last:
            (xs,) = _combine(xs, yg, dest, wts, mod, final_norm_g, mod, ctx_first=False, final_norm=True)
        else:
            xs, h = _combine(xs, yg, dest, wts, mod, norm1_g[l + 1], mod_all[l + 1].reshape(16 * 6, 1, d),
                             ctx_first=True, final_norm=False)
    return xs
```

```python
import functools

import numpy as np
import jax
import jax.numpy as jnp
from jax import lax
from jax.experimental import pallas as pl
from jax.experimental.pallas import tpu as pltpu

F32 = jnp.float32
BF16 = jnp.bfloat16

D_MODEL = 2048
DEPTH = 2
GRID_W = 64
CTX_LEN = 256
EPS = 1e-6
ROPE_BASE = 10000.0
NEG_INF = -1e30

GLA_HEADS = 4
GLA_DK = 64
GLA_DV = 128
GLA_GATE_RANK = 16
GLA_TAU = 16.0
GLA_CHUNK = 64
NA_HEADS = 8
NA_HEAD_DIM = 64
NA_KH = 8
NA_KW = 16
NA_SCALE = NA_HEAD_DIM ** -0.5
MLA_HEADS = 8
MLA_Q_RANK = 512
MLA_KV_RANK = 512
MLA_NOPE = 128
MLA_ROPE = 64
MLA_V = 128
MLA_SCALE = (MLA_NOPE + MLA_ROPE) ** -0.5
N_BRANCH = 3
N_EXPERTS = 32
TOP_K = 4
EXPERT_FF = D_MODEL
SWIGLU_LIMIT = 7.0
SWIGLU_ALPHA = 1.702

GLA_QK_W = GLA_HEADS * GLA_DK
GLA_V_W = GLA_HEADS * GLA_DV
NA_W = NA_HEADS * NA_HEAD_DIM
MLA_V_W = MLA_HEADS * MLA_V

LANE = 128
SUBLANE = 8
ROW_TILE = 256
VMEM_LIMIT = 56 * 1024 * 1024
FFN_VMEM_LIMIT = 62 * 1024 * 1024

COL_GATE = 0
COL_GQ = COL_GATE + N_BRANCH * D_MODEL
COL_GK = COL_GQ + GLA_QK_W
COL_GV = COL_GK + GLA_QK_W
COL_GR = COL_GV + GLA_V_W
COL_NQ = COL_GR + GLA_V_W
COL_NK = COL_NQ + NA_W
COL_NV = COL_NK + NA_W
COL_MCQ = COL_NV + NA_W
COL_MCKV = COL_MCQ + MLA_Q_RANK
COL_KR = COL_MCKV + MLA_KV_RANK
COL_KRP = COL_KR + LANE
COL_GA = COL_KRP + LANE
PROJ_TN = 1536
PROJ_W = 7 * PROJ_TN
assert COL_GA + LANE <= PROJ_W

MOE_G = 1024
MOE_SB = 256
MOE_CHAIN = 1024
MOE_FC = 512
DISPATCH_TT = 256
COMBINE_TT = 128
ROUTE_TT = 256
MLA_TQ = 2048
MLA_CHAIN = 256


def _cparams(sem, vmem_limit=VMEM_LIMIT):
    return pltpu.CompilerParams(dimension_semantics=sem, vmem_limit_bytes=vmem_limit)


def _nt(a, b):
    return lax.dot_general(a, b, (((1,), (1,)), ((), ())), preferred_element_type=F32)


def _tn(a, b):
    return lax.dot_general(a, b, (((0,), (0,)), ((), ())), preferred_element_type=F32)


def _mm(a, b):
    return jnp.dot(a, b, preferred_element_type=F32)


def _sigmoid(x):
    return 1.0 / (1.0 + jnp.exp(-x))


def _split_hi_lo(x):
    hi = x.astype(BF16)
    lo = (x - hi.astype(F32)).astype(BF16)
    return hi, lo


def _ada_kernel(c_ref, w_ref, b_ref, o_ref):
    c = c_ref[...]
    s = (c * _sigmoid(c)).astype(BF16)
    o_ref[0] = _mm(s, w_ref[0].astype(BF16)) + b_ref[0]


def _ada_mod(cc, ada_w, ada_b):
    nl, d, n6 = ada_w.shape
    tn = 1024
    return pl.pallas_call(
        _ada_kernel,
        grid=(nl, n6 // tn),
        in_specs=[pl.BlockSpec((16, d), lambda l, j: (0, 0)),
                  pl.BlockSpec((1, d, tn), lambda l, j: (l, 0, j)),
                  pl.BlockSpec((1, 1, tn), lambda l, j: (l, 0, j))],
        out_specs=pl.BlockSpec((1, 16, tn), lambda l, j: (l, 0, j)),
        out_shape=jax.ShapeDtypeStruct((nl, 16, n6), F32),
        compiler_params=_cparams(("arbitrary", "arbitrary")),
        name="ada_mod",
    )(cc, ada_w, ada_b.reshape(nl, 1, n6))


def _mod_spec(part, ctx_first):
    def imap(b, j):
        row = jnp.where(j == 0, 8, b) if ctx_first else b
        return (row * 6 + part, 0, 0)
    return pl.BlockSpec((1, 1, D_MODEL), imap)


def _norm_mod_kernel(ctx_ref, x_ref, g_ref, sc_ref, sh_ref, xs_ref, o_ref):
    x = jnp.where(pl.program_id(1) == 0, ctx_ref[0], x_ref[0])
    xs_ref[0] = x
    ms = jnp.mean(x * x, axis=-1, keepdims=True)
    y = x * lax.rsqrt(ms + EPS) * g_ref[...]
    o_ref[0] = (y * (1.0 + sc_ref[0]) + sh_ref[0]).astype(o_ref.dtype)


def _norm_mod(ctx, x, g, mod):
    b, n, d = x.shape
    rb = ctx.shape[1] + n
    assert ctx.shape[1] == ROW_TILE
    row_spec = pl.BlockSpec((1, ROW_TILE, d), lambda i, j: (i, j, 0))
    return pl.pallas_call(
        _norm_mod_kernel,
        grid=(b, rb // ROW_TILE),
        in_specs=[pl.BlockSpec((1, ROW_TILE, d), lambda i, j: (i, 0, 0)),
                  pl.BlockSpec((1, ROW_TILE, d), lambda i, j: (i, jnp.maximum(j - 1, 0), 0)),
                  pl.BlockSpec((1, d), lambda i, j: (0, 0)),
                  _mod_spec(1, True), _mod_spec(0, True)],
        out_specs=[row_spec, row_spec],
        out_shape=[jax.ShapeDtypeStruct((b, rb, d), F32), jax.ShapeDtypeStruct((b, rb, d), BF16)],
        compiler_params=_cparams(("arbitrary", "arbitrary")),
        name="norm_mod",
    )(ctx, x, g.reshape(1, d), mod, mod)


def _matmul_kernel(x_ref, w_ref, o_ref):
    o_ref[...] = _mm(x_ref[...], w_ref[...]).astype(o_ref.dtype)


def _matmul(x, w, tm, tn, out_dtype):
    m, k = x.shape
    n = w.shape[1]
    return pl.pallas_call(
        _matmul_kernel,
        grid=(n // tn, m // tm),
        in_specs=[pl.BlockSpec((tm, k), lambda j, i: (i, 0)),
                  pl.BlockSpec((k, tn), lambda j, i: (0, j))],
        out_specs=pl.BlockSpec((tm, tn), lambda j, i: (i, j)),
        out_shape=jax.ShapeDtypeStruct((m, n), out_dtype),
        compiler_params=_cparams(("arbitrary", "arbitrary")),
        name="in_proj",
    )(x, w)


def _gla_kernel(q_ref, k_ref, v_ref, r_ref, ab_ref, waf_ref, wab_ref, ba_ref, g_ref, o_ref,
                laf_ref, lab_ref, of_ref, ob_ref, st_ref, *, n_ctx_chunks, n_chunks):
    c = GLA_CHUNK
    ab = ab_ref[0]
    zf = _mm(ab, waf_ref[...]) + ba_ref[0:1, :]
    zb = _mm(ab, wab_ref[...]) + ba_ref[1:2, :]
    laf_ref[...] = (jnp.minimum(zf, 0.0) - jnp.log(1.0 + jnp.exp(-jnp.abs(zf)))) * (1.0 / GLA_TAU)
    lab_ref[...] = (jnp.minimum(zb, 0.0) - jnp.log(1.0 + jnp.exp(-jnp.abs(zb)))) * (1.0 / GLA_TAU)
    st_ref[...] = jnp.zeros_like(st_ref)

    ri = lax.broadcasted_iota(jnp.int32, (c, c), 0)
    ci = lax.broadcasted_iota(jnp.int32, (c, c), 1)
    tri = [(ri >= ci), (ri <= ci)]
    tri_bf = [t.astype(F32).astype(BF16) for t in tri]
    tri4 = [jnp.concatenate([t] * GLA_HEADS, axis=0) for t in tri]
    lane_q = lax.broadcasted_iota(jnp.int32, (c, GLA_QK_W), 1) // GLA_DK
    lane_v = lax.broadcasted_iota(jnp.int32, (c, GLA_V_W), 1) // GLA_DV
    st_row_h = lax.broadcasted_iota(jnp.int32, (GLA_V_W, GLA_QK_W), 0) // GLA_DV
    st_col_h = lax.broadcasted_iota(jnp.int32, (GLA_V_W, GLA_QK_W), 1) // GLA_DK
    st_mask = st_row_h == st_col_h

    def one_dir(d, chunk, la_ref, out_ref):
        r0 = pl.multiple_of(chunk * c, c)
        la = la_ref[pl.ds(r0, c), :]
        hi, lo = _split_hi_lo(la)
        bc = _mm(tri_bf[d], hi) + _mm(tri_bf[d], lo)
        btot = bc[c - 1:c, :] if d == 0 else bc[0:1, :]
        q = q_ref[0, pl.ds(r0, c), :].astype(F32) * (GLA_DK ** -0.5)
        k = k_ref[0, pl.ds(r0, c), :].astype(F32)
        v = v_ref[0, pl.ds(r0, c), :]
        qd = (q * jnp.exp(bc)).astype(BF16)
        ki = (k * jnp.exp(-bc)).astype(BF16)
        kd = (k * jnp.exp(btot - bc)).astype(BF16)
        zero = jnp.zeros_like(qd)
        qs = jnp.concatenate([jnp.where(lane_q == h, qd, zero) for h in range(GLA_HEADS)], axis=0)
        att = _nt(qs, ki)
        att = jnp.where(tri4[d], att, 0.0).astype(BF16)
        rr = _mm(att, v)
        o = _nt(qd, st_ref[d].astype(BF16))
        for h in range(GLA_HEADS):
            o = o + jnp.where(lane_v == h, rr[h * c:(h + 1) * c, :], 0.0)
        out_ref[pl.ds(r0, c), :] = o
        upd = _tn(v, kd)
        st_ref[d] = jnp.where(st_mask, st_ref[d] * jnp.exp(btot) + upd, 0.0)

    def step(i, carry):
        one_dir(0, i, laf_ref, of_ref)
        cb = jnp.where(i < n_ctx_chunks, n_ctx_chunks - 1 - i, n_chunks + n_ctx_chunks - 1 - i)
        one_dir(1, cb, lab_ref, ob_ref)
        return carry

    lax.fori_loop(0, n_chunks, step, 0, unroll=4)

    def epilogue(j, carry):
        r0 = pl.multiple_of(j * ROW_TILE, ROW_TILE)
        o = of_ref[pl.ds(r0, ROW_TILE), :] + ob_ref[pl.ds(r0, ROW_TILE), :]
        r = r_ref[0, pl.ds(r0, ROW_TILE), :].astype(F32)
        gate = r * _sigmoid(r)
        for h in range(GLA_HEADS):
            sl = slice(h * GLA_DV, (h + 1) * GLA_DV)
            oh = o[:, sl]
            ms = jnp.mean(oh * oh, axis=-1, keepdims=True)
            y = oh * lax.rsqrt(ms + EPS) * g_ref[:, sl]
            o_ref[0, pl.ds(r0, ROW_TILE), sl] = (y * gate[:, sl]).astype(o_ref.dtype)
        return carry

    lax.fori_loop(0, (n_chunks * c) // ROW_TILE, epilogue, 0)


def _gla(p3, waf, wab, ba, g):
    b, rb, _ = p3.shape
    n_chunks = rb // GLA_CHUNK
    kern = functools.partial(_gla_kernel, n_ctx_chunks=CTX_LEN // GLA_CHUNK, n_chunks=n_chunks)

    def col(width, off):
        return pl.BlockSpec((1, rb, width), lambda i: (i, 0, off // width))

    return pl.pallas_call(
        kern,
        grid=(b,),
        in_specs=[col(GLA_QK_W, COL_GQ), col(GLA_QK_W, COL_GK), col(GLA_V_W, COL_GV), col(GLA_V_W, COL_GR),
                  col(LANE, COL_GA),
                  pl.BlockSpec((LANE, GLA_QK_W), lambda i: (0, 0)),
                  pl.BlockSpec((LANE, GLA_QK_W), lambda i: (0, 0)),
                  pl.BlockSpec((2, GLA_QK_W), lambda i: (0, 0)),
                  pl.BlockSpec((1, GLA_V_W), lambda i: (0, 0))],
        out_specs=pl.BlockSpec((1, rb, GLA_V_W), lambda i: (i, 0, 0)),
        out_shape=jax.ShapeDtypeStruct((b, rb, GLA_V_W), BF16),
        scratch_shapes=[pltpu.VMEM((rb, GLA_QK_W), F32), pltpu.VMEM((rb, GLA_QK_W), F32),
                        pltpu.VMEM((rb, GLA_V_W), F32), pltpu.VMEM((rb, GLA_V_W), F32),
                        pltpu.VMEM((2, GLA_V_W, GLA_QK_W), F32)],
        compiler_params=_cparams(("arbitrary",)),
        name="gla",
    )(p3, p3, p3, p3, p3, waf, wab, ba, g)


def _softmax_pv(parts):
    m = parts[0][0].max(axis=-1, keepdims=True)
    for s, _ in parts[1:]:
        m = jnp.maximum(m, s.max(axis=-1, keepdims=True))
    acc = None
    den = None
    for s, v in parts:
        e = jnp.exp(s - m)
        l = e.sum(axis=-1, keepdims=True)
        pv = _mm(e.astype(BF16), v)
        acc = pv if acc is None else acc + pv
        den = l if den is None else den + l
    return acc / den


assert float(np.log2(NA_SCALE)).is_integer()
NA_QR = 4
NA_KR = NA_QR + NA_KH


def _na_block_rows(n_rows):
    r0s = (0, 2 * NA_QR, n_rows - NA_QR)
    return [(r0, int(np.clip(r0 - NA_KH // 2, 0, n_rows - NA_KR))) for r0 in r0s]


def _na_kernel(q_ref, k_ref, v_ref, bias_ref, o_ref, *, n_rows):
    nq = NA_QR * GRID_W
    nk = NA_KR * GRID_W
    n_blocks = n_rows // NA_QR
    lane = lax.broadcasted_iota(jnp.int32, (nq, LANE), 1)
    head_mask = [lane < NA_HEAD_DIM, lane >= NA_HEAD_DIM]
    kc = k_ref[0, 0:CTX_LEN, :]
    vc = v_ref[0, 0:CTX_LEN, :]

    def block(bi, carry):
        r0 = bi * NA_QR
        ks = jnp.clip(r0 - NA_KH // 2, 0, n_rows - NA_KR)
        var = jnp.where(bi == 0, 0, jnp.where(bi == n_blocks - 1, 2, 1))
        q0 = pl.multiple_of(CTX_LEN + r0 * GRID_W, GRID_W)
        k0 = pl.multiple_of(CTX_LEN + ks * GRID_W, GRID_W)
        q = q_ref[0, pl.ds(q0, nq), :] * NA_SCALE
        kl = k_ref[0, pl.ds(k0, nk), :]
        vl = v_ref[0, pl.ds(k0, nk), :]
        outs = []
        for hh in range(2):
            qm = jnp.where(head_mask[hh], q, jnp.zeros_like(q))
            s_loc = _nt(qm, kl) + bias_ref[hh, var]
            s_ctx = _nt(qm, kc)
            outs.append(_softmax_pv([(s_loc, vl), (s_ctx, vc)]))
        o_ref[0, pl.ds(q0, nq), :] = jnp.where(head_mask[0], outs[0], outs[1]).astype(o_ref.dtype)
        return carry

    lax.fori_loop(0, n_blocks, block, 0, unroll=2)

    qc = q_ref[0, 0:CTX_LEN, :]
    lane_c = lax.broadcasted_iota(jnp.int32, (CTX_LEN, LANE), 1)
    outs = []
    for hh in range(2):
        msk = (lane_c < NA_HEAD_DIM) if hh == 0 else (lane_c >= NA_HEAD_DIM)
        qm = jnp.where(msk, qc, jnp.zeros_like(qc))
        outs.append(_softmax_pv([(_nt(qm, kc) * NA_SCALE, vc)]))
    o_ref[0, 0:CTX_LEN, :] = jnp.where(lane_c < NA_HEAD_DIM, outs[0], outs[1]).astype(o_ref.dtype)


def _na_bias_table(rpb, n_rows):
    qcol = np.arange(GRID_W)[:, None]
    kcol = np.arange(GRID_W)[None, :]
    wstart = np.clip(qcol - NA_KW // 2, 0, GRID_W - NA_KW)
    in_win = (kcol >= wstart) & (kcol < wstart + NA_KW)
    dc = kcol - qcol + NA_KW - 1
    col_hot = ((dc[:, :, None] == np.arange(2 * NA_KW - 1)) & in_win[:, :, None]).astype(np.float32)
    geo = _na_block_rows(n_rows)
    row_hot = np.zeros((len(geo), NA_QR, NA_KR, 2 * NA_KH - 1), np.float32)
    for v, (r0, ks) in enumerate(geo):
        for a in range(NA_QR):
            rs = int(np.clip(r0 + a - NA_KH // 2, 0, n_rows - NA_KH))
            for i in range(NA_KR):
                if rs <= ks + i < rs + NA_KH:
                    row_hot[v, a, i, ks + i - (r0 + a) + NA_KH - 1] = 1.0
    t = jnp.einsum('hrc,vair,qkc->hvaqik', rpb.astype(F32), jnp.asarray(row_hot), jnp.asarray(col_hot),
                   precision=lax.Precision.HIGHEST)
    inside = (row_hot.sum(-1) > 0)[:, :, None, :, None] & in_win[None, None, :, None, :]
    t = jnp.where(inside[None], t, NEG_INF)
    return t.reshape(NA_HEADS, len(geo), NA_QR * GRID_W, NA_KR * GRID_W)


def _na(p3, bias):
    b, rb, _ = p3.shape
    n_rows = (rb - CTX_LEN) // GRID_W
    kern = functools.partial(_na_kernel, n_rows=n_rows)

    def col(off):
        return pl.BlockSpec((1, rb, LANE), lambda p, i: (i, 0, off // LANE + p))

    return pl.pallas_call(
        kern,
        grid=(NA_HEADS // 2, b),
        in_specs=[col(COL_NQ), col(COL_NK), col(COL_NV),
                  pl.BlockSpec((2,) + bias.shape[1:], lambda p, i: (p, 0, 0, 0))],
        out_specs=pl.BlockSpec((1, rb, LANE), lambda p, i: (i, 0, p)),
        out_shape=jax.ShapeDtypeStruct((b, rb, NA_W), BF16),
        compiler_params=_cparams(("arbitrary", "arbitrary")),
        name="na",
    )(p3, p3, p3, bias)


def _mla_proj_kernel(cq_ref, ckv_ref, kr_ref, krp_ref, cos_ref, sin_ref, gq_ref, gkv_ref,
                     wq_ref, wq2_ref, wkv_ref, q_ref, k_ref, v_ref):
    def norm(x, g):
        x = x.astype(F32)
        ms = jnp.mean(x * x, axis=-1, keepdims=True)
        return (x * lax.rsqrt(ms + EPS) * g).astype(BF16)

    cos = cos_ref[...]
    sin = sin_ref[...]
    nq = norm(cq_ref[0], gq_ref[...])
    yq = _mm(nq, wq_ref[...])
    yq2 = _mm(nq, wq2_ref[...])
    nkv = norm(ckv_ref[0], gkv_ref[...])
    ykv = _mm(nkv, wkv_ref[...])
    k_rot = (kr_ref[0].astype(F32) * cos + krp_ref[0].astype(F32) * sin).astype(BF16)
    for h in range(MLA_HEADS):
        a = 2 * h * LANE
        q_ref[0, :, a:a + LANE] = (yq[:, a:a + LANE] * MLA_SCALE).astype(BF16)
        rot = yq[:, a + LANE:a + 2 * LANE] * cos + yq2[:, h * LANE:(h + 1) * LANE] * sin
        q_ref[0, :, a + LANE:a + 2 * LANE] = (rot * MLA_SCALE).astype(BF16)
        k_ref[0, :, a:a + LANE] = ykv[:, h * LANE:(h + 1) * LANE].astype(BF16)
        k_ref[0, :, a + LANE:a + 2 * LANE] = k_rot
        v0 = MLA_HEADS * MLA_NOPE + h * MLA_V
        v_ref[0, :, a:a + LANE] = ykv[:, v0:v0 + MLA_V].astype(BF16)
        v_ref[0, :, a + LANE:a + 2 * LANE] = jnp.ones((ykv.shape[0], LANE), BF16)


def _mla_proj(p3, cos_t, sin_t, gq, gkv, wq, wq2, wkv):
    b, rb, _ = p3.shape
    tm = ROW_TILE

    def col(width, off):
        return pl.BlockSpec((1, tm, width), lambda i, j: (i, j, off // width))

    def full(a):
        return pl.BlockSpec(a.shape, lambda i, j: (0, 0))

    hw = MLA_HEADS * 2 * LANE
    nj = rb // tm
    return pl.pallas_call(
        _mla_proj_kernel,
        grid=(b, nj),
        in_specs=[col(MLA_Q_RANK, COL_MCQ), col(MLA_KV_RANK, COL_MCKV), col(LANE, COL_KR), col(LANE, COL_KRP),
                  pl.BlockSpec((tm, LANE), lambda i, j: (j, 0)), pl.BlockSpec((tm, LANE), lambda i, j: (j, 0)),
                  full(gq), full(gkv), full(wq), full(wq2), full(wkv)],
        out_specs=[pl.BlockSpec((1, tm, hw), lambda i, j: (i, (j + nj - 1) % nj, 0)),
                   pl.BlockSpec((1, tm, hw), lambda i, j: (i, (j + nj - 1) % nj, 0)),
                   pl.BlockSpec((1, tm, hw), lambda i, j: (i, (j + nj - 1) % nj, 0))],
        out_shape=[jax.ShapeDtypeStruct((b, rb, hw), BF16), jax.ShapeDtypeStruct((b, rb, hw), BF16),
                   jax.ShapeDtypeStruct((b, rb, hw), BF16)],
        compiler_params=_cparams(("arbitrary", "arbitrary")),
        name="mla_proj",
    )(p3, p3, p3, p3, cos_t, sin_t, gq, gkv, wq, wq2, wkv)


def _mla_attn_kernel(q_ref, k_ref, v_ref, o_ref):
    k = k_ref[0]
    v = v_ref[0]
    tq = q_ref.shape[1]
    sub = min(tq, MLA_CHAIN)
    for r0 in range(0, tq, sub):
        s = _nt(q_ref[0, r0:r0 + sub, :], k)
        e = jnp.exp((s - s.max(axis=-1, keepdims=True)).astype(BF16))
        pv = _mm(e, v)
        o_ref[0, r0:r0 + sub, :] = (pv[:, :MLA_V] / pv[:, MLA_V:]).astype(o_ref.dtype)


def _mla_attn(q, k, v, ctx_queries):
    b, rb, _ = q.shape
    n_lat = rb - CTX_LEN
    if ctx_queries:
        tq, nq, n_keys = CTX_LEN, 1, CTX_LEN
        q_off = kv_blk = n_lat // CTX_LEN
    else:
        tq = MLA_TQ if n_lat % MLA_TQ == 0 else ROW_TILE
        nq, n_keys, q_off, kv_blk = n_lat // tq, rb, 0, 0
    return pl.pallas_call(
        _mla_attn_kernel,
        grid=(b, MLA_HEADS, nq),
        in_specs=[pl.BlockSpec((1, tq, 2 * LANE), lambda i, h, j: (i, j + q_off, h)),
                  pl.BlockSpec((1, n_keys, 2 * LANE), lambda i, h, j: (i, kv_blk, h)),
                  pl.BlockSpec((1, n_keys, 2 * LANE), lambda i, h, j: (i, kv_blk, h))],
        out_specs=pl.BlockSpec((1, tq, MLA_V), lambda i, h, j: (i, j, h)),
        out_shape=jax.ShapeDtypeStruct((b, nq * tq, MLA_V_W), BF16),
        compiler_params=_cparams(("arbitrary", "arbitrary", "arbitrary")),
        name="mla_attn_ctx" if ctx_queries else "mla_attn",
    )(q, k, v)


def _merge_kernel(x_ref, oa_ref, ob_ref, oc_ref, occ_ref, ga_ref, gb_ref, gc_ref, bg_ref,
                  wa_ref, wb_ref, wc_ref, wo_ref, gt1_ref, sc2_ref, sh2_ref, g2_ref, rw_ref, rb_ref,
                  xo_ref, h_ref, lg_ref, *, ctx_tile):
    d = D_MODEL

    wh, wl = _split_hi_lo(rw_ref[...])
    tm = x_ref.shape[1]
    sub = tm
    for r0 in range(0, tm, sub):
        rows = slice(r0, r0 + sub)

        def gate(g_ref, k):
            return _sigmoid(g_ref[0, rows, :].astype(F32) + bg_ref[:, k * d:(k + 1) * d])

        oc = oc_ref[0, rows, :]
        if ctx_tile:
            oc = jnp.where(pl.program_id(1) == 0, occ_ref[0, rows, :], oc)
        m = gate(ga_ref, 0) * _mm(oa_ref[0, rows, :], wa_ref[...])
        m = m + gate(gb_ref, 1) * _mm(ob_ref[0, rows, :], wb_ref[...])
        m = m + gate(gc_ref, 2) * _mm(oc, wc_ref[...])
        y = _mm(m.astype(BF16), wo_ref[...])
        x = x_ref[0, rows, :] + gt1_ref[0] * y
        xo_ref[0, rows, :] = x
        ms = jnp.mean(x * x, axis=-1, keepdims=True)
        h = x * lax.rsqrt(ms + EPS) * g2_ref[...]
        h = h * (1.0 + sc2_ref[0]) + sh2_ref[0]
        h_ref[rows, :] = h
        hh, hl = _split_hi_lo(h)
        lg_ref[:, rows] = _nt(wh, hh) + _nt(wh, hl) + _nt(wl, hh) + rb_ref[:, 0:1]


def _merge(x, p3, og, on, om, om_ctx, bg, wa, wb, wc, wo, mod, g2, rwt, rbias, skip_ctx):
    b, rb, d = x.shape
    tm = ROW_TILE
    jo = 1 if skip_ctx else 0
    nj = rb // tm - jo
    rows_out = nj * tm

    def rows(width, cblk=0):
        return pl.BlockSpec((1, tm, width), lambda i, j: (i, j + jo, cblk))

    om_spec = pl.BlockSpec((1, tm, MLA_V_W), lambda i, j: (i, jnp.maximum(j + jo - 1, 0), 0))
    omc_spec = pl.BlockSpec((1, tm, MLA_V_W), lambda i, j: (i, 0, 0))
    kern = functools.partial(_merge_kernel, ctx_tile=not skip_ctx)

    def full(a):
        return pl.BlockSpec(a.shape, lambda i, j: (0,) * a.ndim, pipeline_mode=pl.Buffered(1))

    def mod_spec(part):
        def imap(i, j):
            row = i if skip_ctx else jnp.where(j == 0, 8, i)
            return (row * 6 + part, 0, 0)
        return pl.BlockSpec((1, 1, d), imap)

    return pl.pallas_call(
        kern,
        grid=(b, nj),
        in_specs=[rows(d), rows(GLA_V_W), rows(NA_W), om_spec, omc_spec,
                  rows(d, 0), rows(d, 1), rows(d, 2), full(bg),
                  full(wa), full(wb), full(wc), full(wo),
                  mod_spec(2), mod_spec(4), mod_spec(3), full(g2), full(rwt), full(rbias)],
        out_specs=[pl.BlockSpec((1, tm, d), lambda i, j: (i, j, 0)),
                   pl.BlockSpec((tm, d), lambda i, j: (i * nj + j, 0)),
                   pl.BlockSpec((N_EXPERTS, tm), lambda i, j: (0, i * nj + j))],
        out_shape=[jax.ShapeDtypeStruct((b, rows_out, d), F32),
                   jax.ShapeDtypeStruct((b * rows_out, d), F32),
                   jax.ShapeDtypeStruct((N_EXPERTS, b * rows_out), F32)],
        compiler_params=_cparams(("arbitrary", "arbitrary")),
        name="merge",
    )(x, og, on, om, om if skip_ctx else om_ctx, p3, p3, p3, bg, wa, wb, wc, wo, mod, mod, mod, g2, rwt, rbias)


def _route_kernel(l_ref, idx_ref, w_ref, rank_ref, cnt_ref, carry_ref):
    i = pl.program_id(0)
    tt = l_ref.shape[1]

    @pl.when(i == 0)
    def _():
        carry_ref[...] = jnp.zeros_like(carry_ref)

    l = l_ref[...]
    eio = lax.broadcasted_iota(jnp.int32, (N_EXPERTS, tt), 0)
    vals, idxs = [], []
    for _ in range(TOP_K):
        m = l.max(axis=0, keepdims=True)
        ik = jnp.min(jnp.where(l == m, eio, N_EXPERTS), axis=0, keepdims=True)
        vals.append(m)
        idxs.append(ik)
        l = jnp.where(eio == ik, -jnp.inf, l)
    es = [jnp.exp(v - vals[0]) for v in vals]
    den = es[0] + es[1] + es[2] + es[3]
    sel = jnp.zeros((N_EXPERTS, tt), F32)
    for ik in idxs:
        sel = sel + (eio == ik).astype(F32)
    si = lax.broadcasted_iota(jnp.int32, (tt, tt), 0)
    ti = lax.broadcasted_iota(jnp.int32, (tt, tt), 1)
    before = (si < ti).astype(F32).astype(BF16)
    rank_full = _mm(sel.astype(BF16), before) + carry_ref[:, 0:1]
    for k in range(TOP_K):
        idx_ref[k:k + 1, :] = idxs[k]
        w_ref[k:k + 1, :] = es[k] / den
        rk = jnp.sum(jnp.where(eio == idxs[k], rank_full, 0.0), axis=0, keepdims=True)
        rank_ref[k:k + 1, :] = rk.astype(jnp.int32)
    carry_ref[...] = carry_ref[...] + jnp.sum(sel, axis=1, keepdims=True)
    cnt_ref[...] = carry_ref[...]


def _route(logits_t):
    ne, t = logits_t.shape
    tt = ROUTE_TT
    spec4 = pl.BlockSpec((TOP_K, tt), lambda i: (0, i))
    return pl.pallas_call(
        _route_kernel,
        grid=(t // tt,),
        in_specs=[pl.BlockSpec((ne, tt), lambda i: (0, i))],
        out_specs=[spec4, spec4, spec4, pl.BlockSpec((ne, LANE), lambda i: (0, 0))],
        out_shape=[jax.ShapeDtypeStruct((TOP_K, t), jnp.int32), jax.ShapeDtypeStruct((TOP_K, t), F32),
                   jax.ShapeDtypeStruct((TOP_K, t), jnp.int32), jax.ShapeDtypeStruct((ne, LANE), F32)],
        scratch_shapes=[pltpu.VMEM((ne, LANE), F32)],
        compiler_params=_cparams(("arbitrary",)),
        name="route",
    )(logits_t)


def _pad_fill(ps_ref, pn_ref, zero_ref, xg_ref, sem, wait):
    def copy(pos, rows):
        cp = pltpu.make_async_copy(zero_ref.at[pl.ds(0, rows), :], xg_ref.at[pl.ds(pos, rows), :], sem)
        cp.wait() if wait else cp.start()

    def per_expert(e, carry):
        pos = ps_ref[e]
        head = (-pos) & (SUBLANE - 1)
        for r in range(SUBLANE - 1):
            @pl.when(r < head)
            def _(r=r):
                copy(pos + r, 1)

        pos = pos + head
        n = pn_ref[e] - head
        bit = MOE_G // 2
        while bit >= SUBLANE:
            on = (n & bit) != 0

            @pl.when(on)
            def _(pos=pos, bit=bit):
                copy(pl.multiple_of(pos, SUBLANE), bit)

            pos = pos + jnp.where(on, bit, 0)
            bit //= 2
        return carry

    lax.fori_loop(0, N_EXPERTS, per_expert, 0)

    zr = zero_ref.shape[0]

    def tail(i, carry):
        pos = pl.multiple_of(ps_ref[N_EXPERTS] + i * zr, zr)
        cp = pltpu.make_async_copy(zero_ref, xg_ref.at[pl.ds(pos, zr), :], sem)
        cp.wait() if wait else cp.start()
        return carry

    lax.fori_loop(0, pn_ref[N_EXPERTS], tail, 0)


def _dispatch_kernel(ps_ref, pn_ref, dest_ref, h_ref, xg_ref, zero_ref, sem, zsem):
    tt = h_ref.shape[0]

    @pl.when(pl.program_id(0) == 0)
    def _():
        zero_ref[...] = jnp.zeros_like(zero_ref)
        _pad_fill(ps_ref, pn_ref, zero_ref, xg_ref, zsem, wait=False)

    def issue(t, carry):
        for k in range(TOP_K):
            pltpu.make_async_copy(h_ref.at[pl.ds(t, 1), :], xg_ref.at[pl.ds(dest_ref[0, k, t], 1), :],
                                  sem).start(priority=k % 2)
        return carry

    lax.fori_loop(0, tt, issue, 0, unroll=4)
    for k in range(TOP_K):
        pltpu.make_async_copy(h_ref, xg_ref.at[pl.ds(0, tt), :], sem).wait()

    @pl.when(pl.program_id(0) == 0)
    def _():
        _pad_fill(ps_ref, pn_ref, zero_ref, xg_ref, zsem, wait=True)


def _dispatch(hp, dest, pad_start, pad_len, n_slots):
    t, w = hp.shape
    tt = DISPATCH_TT
    dest3 = dest.reshape(TOP_K, t // tt, tt).transpose(1, 0, 2)
    grid_spec = pltpu.PrefetchScalarGridSpec(
        num_scalar_prefetch=2,
        grid=(t // tt,),
        in_specs=[pl.BlockSpec((1, TOP_K, tt), lambda i, ps, pn: (i, 0, 0), memory_space=pltpu.SMEM),
                  pl.BlockSpec((tt, w), lambda i, ps, pn: (i, 0))],
        out_specs=pl.BlockSpec(memory_space=pl.ANY),
        scratch_shapes=[pltpu.VMEM((MOE_G // 2, w), hp.dtype), pltpu.SemaphoreType.DMA(()),
                        pltpu.SemaphoreType.DMA(())],
    )
    return pl.pallas_call(
        _dispatch_kernel,
        grid_spec=grid_spec,
        out_shape=jax.ShapeDtypeStruct((n_slots, w), hp.dtype),
        compiler_params=_cparams(("arbitrary",)),
        name="dispatch",
    )(pad_start, pad_len, dest3, hp)


def _ffn_kernel(te_ref, tv_ref, x_ref, w1g_ref, w1l_ref, b1g_ref, b1l_ref, w2_ref, b2_ref, y_ref,
                xb_ref, *, n_fc, n_tiles):
    i = pl.program_id(0)
    j = pl.program_id(1)
    valid = tv_ref[i]
    g = y_ref.shape[0]
    chunk = x_ref.shape[0]

    @pl.when(i < n_tiles)
    def _():
        xb_ref[i % 2, pl.ds(pl.multiple_of(j * chunk, chunk), chunk), :] = x_ref[...].astype(BF16)

    cur = (i + 1) % 2

    @pl.when((valid > 0) & (j == 0))
    def _():
        y_ref[...] = jnp.broadcast_to(b2_ref[0, 0], y_ref.shape)

    for nr in range(MOE_SB, g + 1, MOE_SB):
        @pl.when((valid > nr - MOE_SB) & (valid <= nr))
        def _(nr=nr):
            wg = w1g_ref[0, 0].astype(BF16)
            wl = w1l_ref[0, 0].astype(BF16)
            w2 = w2_ref[0, 0].astype(BF16)
            for r0 in range(0, nr, MOE_CHAIN):
                r1 = min(r0 + MOE_CHAIN, nr)
                x = xb_ref[cur, r0:r1, :]
                ug = _mm(x, wg) + b1g_ref[0, 0]
                ul = _mm(x, wl) + b1l_ref[0, 0]
                xg = jnp.minimum(ug, SWIGLU_LIMIT)
                xl = jnp.clip(ul, -SWIGLU_LIMIT, SWIGLU_LIMIT)
                act = xg * _sigmoid(SWIGLU_ALPHA * xg) * (xl + 1.0)
                y_ref[r0:r1, :] = y_ref[r0:r1, :] + _mm(act.astype(BF16), w2)

    @pl.when((j == n_fc - 1) & (valid == 0))
    def _():
        y_ref[...] = jnp.zeros_like(y_ref)


def _ffn(layer, tile_expert, tile_valid, xg, w1, b1, w2, b2):
    n_slots, d = xg.shape
    nl, ne, _, ff2 = w1.shape
    ff = ff2 // 2
    n_fc = ff // MOE_FC
    n_tiles = n_slots // MOE_G
    chunk = MOE_G // n_fc
    kern = functools.partial(_ffn_kernel, n_fc=n_fc, n_tiles=n_tiles)
    tile_expert = jnp.concatenate([tile_expert[:1], tile_expert])
    tile_valid = jnp.concatenate([jnp.zeros((1,), tile_valid.dtype), tile_valid])

    def jj(j, tv, i):
        return jnp.where(tv[i] > 0, j, n_fc - 1)

    grid_spec = pltpu.PrefetchScalarGridSpec(
        num_scalar_prefetch=2,
        grid=(n_tiles + 1, n_fc),
        in_specs=[pl.BlockSpec((chunk, d), lambda i, j, te, tv: (jnp.minimum(i, n_tiles - 1) * n_fc + j, 0)),
                  pl.BlockSpec((1, 1, d, MOE_FC), lambda i, j, te, tv: (layer, te[i], 0, jj(j, tv, i))),
                  pl.BlockSpec((1, 1, d, MOE_FC), lambda i, j, te, tv: (layer, te[i], 0, jj(j, tv, i) + n_fc)),
                  pl.BlockSpec((1, 1, 1, MOE_FC), lambda i, j, te, tv: (layer, te[i], 0, jj(j, tv, i))),
                  pl.BlockSpec((1, 1, 1, MOE_FC), lambda i, j, te, tv: (layer, te[i], 0, jj(j, tv, i) + n_fc)),
                  pl.BlockSpec((1, 1, MOE_FC, d), lambda i, j, te, tv: (layer, te[i], jj(j, tv, i), 0)),
                  pl.BlockSpec((1, 1, 1, d), lambda i, j, te, tv: (layer, te[i], 0, 0))],
        out_specs=pl.BlockSpec((MOE_G, d), lambda i, j, te, tv: (jnp.maximum(i - 1, 0), 0)),
        scratch_shapes=[pltpu.VMEM((2, MOE_G, d), BF16)],
    )
    return pl.pallas_call(
        kern,
        grid_spec=grid_spec,
        out_shape=jax.ShapeDtypeStruct((n_slots, d), F32),
        compiler_params=_cparams(("arbitrary", "arbitrary"), FFN_VMEM_LIMIT),
        name="moe_ffn",
    )(tile_expert, tile_valid, xg, w1, w1, b1.reshape(nl, ne, 1, ff2), b1.reshape(nl, ne, 1, ff2), w2,
      b2.reshape(nl, ne, 1, d))


def _combine_kernel(dest_ref, destn_ref, x_ref, wt_ref, gt2_ref, gn_ref, sc_ref, sh_ref, yg_ref, o_ref, *rest,
                    final_norm):
    h_ref = None if final_norm else rest[0]
    buf_ref, sem = rest[-2:]
    i = pl.program_id(0)
    n = pl.num_programs(0)
    tt = x_ref.shape[1]
    slot = i % 2

    def issue(d_ref, s):
        def body(t, carry):
            for k in range(TOP_K):
                pltpu.make_async_copy(yg_ref.at[pl.ds(d_ref[0, k, t], 1), :],
                                      buf_ref.at[s, k, pl.ds(t, 1), :], sem.at[s]).start(priority=k % 2)
            return carry
        lax.fori_loop(0, tt, body, 0, unroll=4)

    @pl.when(i == 0)
    def _():
        issue(dest_ref, 0)

    @pl.when(i + 1 < n)
    def _():
        issue(destn_ref, 1 - slot)

    for k in range(TOP_K):
        pltpu.make_async_copy(yg_ref.at[pl.ds(0, tt), :], buf_ref.at[slot, k], sem.at[slot]).wait()

    wt = wt_ref[...]
    y = buf_ref[slot, 0] * wt[:, 0:1]
    for k in range(1, TOP_K):
        y = y + buf_ref[slot, k] * wt[:, k:k + 1]
    x = x_ref[0] + gt2_ref[0] * y
    ms = jnp.mean(x * x, axis=-1, keepdims=True)
    xn = x * lax.rsqrt(ms + EPS) * gn_ref[...]
    if final_norm:
        o_ref[0] = xn
    else:
        o_ref[0] = x
        h_ref[0] = (xn * (1.0 + sc_ref[0]) + sh_ref[0]).astype(h_ref.dtype)


def _combine(x, yg, dest, wts, mod, gn, mod_next, ctx_first, final_norm):
    b, rows, d = x.shape
    tt = COMBINE_TT
    nj = rows // tt
    t = b * rows
    nt = t // tt
    dest3 = dest.reshape(TOP_K, nt, tt).transpose(1, 0, 2)
    wt = wts.T
    kern = functools.partial(_combine_kernel, final_norm=final_norm)

    def mod_spec(part):
        def imap(i):
            bi = i // nj
            row = jnp.where((i % nj) * tt < CTX_LEN, 8, bi) if ctx_first else bi
            return (row * 6 + part, 0, 0)
        return pl.BlockSpec((1, 1, d), imap)

    row_spec = pl.BlockSpec((1, tt, d), lambda i: (i // nj, i % nj, 0))
    out_specs = [row_spec]
    out_shape = [jax.ShapeDtypeStruct((b, rows, d), F32)]
    if not final_norm:
        out_specs.append(row_spec)
        out_shape.append(jax.ShapeDtypeStruct((b, rows, d), BF16))

    return pl.pallas_call(
        kern,
        grid=(nt,),
        in_specs=[pl.BlockSpec((1, TOP_K, tt), lambda i: (i, 0, 0), memory_space=pltpu.SMEM),
                  pl.BlockSpec((1, TOP_K, tt), lambda i: (jnp.minimum(i + 1, nt - 1), 0, 0),
                               memory_space=pltpu.SMEM),
                  row_spec,
                  pl.BlockSpec((tt, TOP_K), lambda i: (i, 0)),
                  mod_spec(5),
                  pl.BlockSpec((1, d), lambda i: (0, 0)),
                  mod_spec(1), mod_spec(0),
                  pl.BlockSpec(memory_space=pl.ANY)],
        out_specs=out_specs,
        out_shape=out_shape,
        scratch_shapes=[pltpu.VMEM((2, TOP_K, tt, d), yg.dtype), pltpu.SemaphoreType.DMA((2,))],
        compiler_params=_cparams(("arbitrary",)),
        name="combine",
    )(dest3, dest3, x, wt, mod, gn.reshape(1, d), mod_next, mod_next, yg)


def _proj_weight(w_in):
    d = w_in.shape[0]
    splits = (GLA_QK_W, GLA_QK_W, GLA_V_W, GLA_V_W, GLA_GATE_RANK, GLA_GATE_RANK,
              NA_W, NA_W, NA_W, MLA_Q_RANK, MLA_KV_RANK, MLA_ROPE, N_BRANCH * D_MODEL)
    pts = np.cumsum((0,) + splits)
    (gq, gk, gv, gr, gaf, gab, nq, nk, nv, mcq, mckv, mkr, gate) = [w_in[:, pts[i]:pts[i + 1]] for i in range(13)]
    q16 = MLA_ROPE // 4
    mkrp = jnp.concatenate([mkr[:, q16:2 * q16], mkr[:, :q16], mkr[:, 3 * q16:], mkr[:, 2 * q16:3 * q16]], axis=1)
    z = lambda n: jnp.zeros((d, n), w_in.dtype)
    cols = [gate, gq, gk, gv, gr, nq, nk, nv, mcq, mckv,
            mkr, z(LANE - MLA_ROPE), mkrp, z(LANE - MLA_ROPE),
            gaf, gab, z(LANE - 2 * GLA_GATE_RANK)]
    w = jnp.concatenate(cols, axis=1)
    w = jnp.concatenate([w, z(PROJ_W - w.shape[1])], axis=1)
    return w.astype(BF16)


def _rope_tables(rb):
    n = rb - CTX_LEN
    t = np.arange(n)
    nf = MLA_ROPE // 4
    freqs = ROPE_BASE ** (-np.arange(nf, dtype=np.float64) / nf)
    cos = np.zeros((rb, LANE), np.float32)
    sin = np.zeros((rb, LANE), np.float32)
    cos[:CTX_LEN, :MLA_ROPE] = 1.0
    for a, pos in enumerate((t // GRID_W, t % GRID_W)):
        ang = (pos.astype(np.float32)[:, None] * freqs.astype(np.float32)[None, :]).astype(np.float32)
        c, s = np.cos(ang), np.sin(ang)
        base = a * 2 * nf
        cos[CTX_LEN:, base:base + nf] = c
        cos[CTX_LEN:, base + nf:base + 2 * nf] = c
        sin[CTX_LEN:, base:base + nf] = -s
        sin[CTX_LEN:, base + nf:base + 2 * nf] = s
    return jnp.asarray(cos), jnp.asarray(sin)


def _mla_weights(w_q_up, w_kv_up):
    r = w_q_up.shape[0]
    wq = w_q_up.reshape(r, MLA_HEADS, MLA_NOPE + MLA_ROPE)
    nope, rope = wq[..., :MLA_NOPE], wq[..., MLA_NOPE:]
    q16 = MLA_ROPE // 4
    ropep = jnp.concatenate([rope[..., q16:2 * q16], rope[..., :q16], rope[..., 3 * q16:], rope[..., 2 * q16:3 * q16]],
                            axis=-1)
    zpad = jnp.zeros((r, MLA_HEADS, LANE - MLA_ROPE), w_q_up.dtype)
    wq1 = jnp.concatenate([nope, rope, zpad], axis=-1).reshape(r, MLA_HEADS * 2 * LANE).astype(BF16)
    wq2 = jnp.concatenate([ropep, zpad], axis=-1).reshape(r, MLA_HEADS * LANE).astype(BF16)
    rk = w_kv_up.shape[0]
    wkv = w_kv_up.reshape(rk, MLA_HEADS, 2, MLA_NOPE).transpose(0, 2, 1, 3).reshape(rk, 2 * MLA_HEADS * MLA_NOPE)
    return wq1, wq2, wkv.astype(BF16)


def _moe_plan(idx, rank, counts, n_tiles):
    cnt = counts[:, 0].astype(jnp.int32)
    padded = ((cnt + MOE_G - 1) // MOE_G) * MOE_G
    ends = jnp.cumsum(padded)
    starts = ends - padded
    e_ids = jnp.arange(N_EXPERTS, dtype=jnp.int32)
    dest = jnp.sum(jnp.where(idx[..., None] == e_ids, starts, 0), axis=-1) + rank
    tile_start = jnp.arange(n_tiles, dtype=jnp.int32) * MOE_G
    te = jnp.sum((tile_start[:, None] >= ends[None, :]).astype(jnp.int32), axis=1)
    active = te < N_EXPERTS
    te_c = jnp.minimum(te, N_EXPERTS - 1)
    tile_is = te_c[:, None] == e_ids[None, :]
    cnt_t = jnp.sum(jnp.where(tile_is, cnt, 0), axis=1)
    start_t = jnp.sum(jnp.where(tile_is, starts, 0), axis=1)
    valid = jnp.clip(cnt_t - (tile_start - start_t), 0, MOE_G)
    valid = jnp.where(active, valid, 0)
    last_e = jnp.max(jnp.where(cnt > 0, jnp.arange(N_EXPERTS, dtype=jnp.int32), 0))
    te_f = jnp.where(active, te_c, last_e)
    tail_blocks = (n_tiles * MOE_G - ends[-1]) // (MOE_G // 2)
    pad_start = jnp.concatenate([starts + cnt, ends[-1:]]).astype(jnp.int32)
    pad_len = jnp.concatenate([padded - cnt, tail_blocks[None]]).astype(jnp.int32)
    return dest, te_f, valid, pad_start, pad_len


def kernel(x, c, ctx, c_ctx, norm1_g, norm2_g, ada_w, ada_b, w_in, b_gate, gla_wa, gla_ba, gla_norm_g,
           na_rpb, mla_q_norm_g, mla_w_q_up, mla_kv_norm_g, mla_w_kv_up, w_branch_gla, w_branch_na,
           w_branch_mla, w_out, router_w, router_b, moe_w1, moe_b1, moe_w2, moe_b2, final_norm_g):
    b, n, d = x.shape
    rb = CTX_LEN + n
    assert b <= 8 and d == D_MODEL and ctx.shape[1] == CTX_LEN

    cc = jnp.zeros((16, d), F32).at[:b].set(c).at[8].set(c_ctx)
    mod_all = _ada_mod(cc, ada_w, ada_b)
    cos_t, sin_t = _rope_tables(rb)

    for l in range(DEPTH):
        last = l == DEPTH - 1
        mod = mod_all[l].reshape(16 * 6, 1, d)
        if l == 0:
            xs, h = _norm_mod(ctx, x, norm1_g[l], mod)
        p = _matmul(h.reshape(b * rb, d), _proj_weight(w_in[l]), 1024 if (b * rb) % 1024 == 0 else ROW_TILE,
                    PROJ_TN, BF16)
        p3 = p.reshape(b, rb, PROJ_W)

        zpad = jnp.zeros((LANE - 2 * GLA_GATE_RANK, GLA_QK_W), F32)
        zr = jnp.zeros((GLA_GATE_RANK, GLA_QK_W), F32)
        waf = jnp.concatenate([gla_wa[l, 0], zr, zpad], axis=0).astype(BF16)
        wab = jnp.concatenate([zr, gla_wa[l, 1], zpad], axis=0).astype(BF16)
        og = _gla(p3, waf, wab, gla_ba[l], gla_norm_g[l].reshape(1, GLA_V_W))

        on = _na(p3, _na_bias_table(na_rpb[l], n // GRID_W))

        wq1, wq2, wkv = _mla_weights(mla_w_q_up[l], mla_w_kv_up[l])
        q_m, k_m, v_m = _mla_proj(p3, cos_t, sin_t, mla_q_norm_g[l].reshape(1, -1), mla_kv_norm_g[l].reshape(1, -1),
                                  wq1, wq2, wkv)
        om = _mla_attn(q_m, k_m, v_m, ctx_queries=False)
        om_ctx = None if last else _mla_attn(q_m, k_m, v_m, ctx_queries=True)

        xs, hp, logits_t = _merge(
            xs, p3, og, on, om, om_ctx, b_gate[l].reshape(1, -1),
            w_branch_gla[l].astype(BF16), w_branch_na[l].astype(BF16), w_branch_mla[l].astype(BF16),
            w_out[l].astype(BF16), mod, norm2_g[l].reshape(1, d), router_w[l].T,
            jnp.broadcast_to(router_b[l][:, None], (N_EXPERTS, LANE)), skip_ctx=last)

        t_tok = hp.shape[0]
        idx, wts, rank, counts = _route(logits_t)
        n_tiles = (TOP_K * t_tok) // MOE_G + N_EXPERTS
        dest, te, tv, pad_start, pad_len = _moe_plan(idx, rank, counts, n_tiles)
        xg = _dispatch(hp, dest, pad_start, pad_len, n_tiles * MOE_G)
        yg = _ffn(l, te, tv, xg, moe_w1, moe_b1, moe_w2, moe_b2)
        if last:
            (xs,) = _combine(xs, yg, dest, wts, mod, final_norm_g, mod, ctx_first=False, final_norm=True)
        else:
            xs, h = _combine(xs, yg, dest, wts, mod, norm1_g[l + 1], mod_all[l + 1].reshape(16 * 6, 1, d),
                             ctx_first=True, final_norm=False)
    return xs
```

```python
import functools

import numpy as np
import jax
import jax.numpy as jnp
from jax import lax
from jax.experimental import pallas as pl
from jax.experimental.pallas import tpu as pltpu

F32 = jnp.float32
BF16 = jnp.bfloat16

D_MODEL = 2048
DEPTH = 2
GRID_W = 64
CTX_LEN = 256
EPS = 1e-6
ROPE_BASE = 10000.0
NEG_INF = -1e30

GLA_HEADS = 4
GLA_DK = 64
GLA_DV = 128
GLA_GATE_RANK = 16
GLA_TAU = 16.0
GLA_CHUNK = 64
NA_HEADS = 8
NA_HEAD_DIM = 64
NA_KH = 8
NA_KW = 16
NA_SCALE = NA_HEAD_DIM ** -0.5
MLA_HEADS = 8
MLA_Q_RANK = 512
MLA_KV_RANK = 512
MLA_NOPE = 128
MLA_ROPE = 64
MLA_V = 128
MLA_SCALE = (MLA_NOPE + MLA_ROPE) ** -0.5
N_BRANCH = 3
N_EXPERTS = 32
TOP_K = 4
EXPERT_FF = D_MODEL
SWIGLU_LIMIT = 7.0
SWIGLU_ALPHA = 1.702

GLA_QK_W = GLA_HEADS * GLA_DK
GLA_V_W = GLA_HEADS * GLA_DV
NA_W = NA_HEADS * NA_HEAD_DIM
MLA_V_W = MLA_HEADS * MLA_V

LANE = 128
SUBLANE = 8
ROW_TILE = 256
VMEM_LIMIT = 56 * 1024 * 1024
FFN_VMEM_LIMIT = 62 * 1024 * 1024

COL_GATE = 0
COL_GQ = COL_GATE + N_BRANCH * D_MODEL
COL_GK = COL_GQ + GLA_QK_W
COL_GV = COL_GK + GLA_QK_W
COL_GR = COL_GV + GLA_V_W
COL_NQ = COL_GR + GLA_V_W
COL_NK = COL_NQ + NA_W
COL_NV = COL_NK + NA_W
COL_MCQ = COL_NV + NA_W
COL_MCKV = COL_MCQ + MLA_Q_RANK
COL_KR = COL_MCKV + MLA_KV_RANK
COL_KRP = COL_KR + LANE
COL_GA = COL_KRP + LANE
PROJ_TN = 1536
PROJ_W = 7 * PROJ_TN
assert COL_GA + LANE <= PROJ_W

MOE_G = 1024
MOE_SB = 256
MOE_CHAIN = 1024
MOE_FC = 512
DISPATCH_TT = 256
COMBINE_TT = 128
ROUTE_TT = 256
MLA_TQ = 2048
MLA_CHAIN = 256


def _cparams(sem, vmem_limit=VMEM_LIMIT):
    return pltpu.CompilerParams(dimension_semantics=sem, vmem_limit_bytes=vmem_limit)


def _nt(a, b):
    return lax.dot_general(a, b, (((1,), (1,)), ((), ())), preferred_element_type=F32)


def _tn(a, b):
    return lax.dot_general(a, b, (((0,), (0,)), ((), ())), preferred_element_type=F32)


def _mm(a, b):
    return jnp.dot(a, b, preferred_element_type=F32)


def _sigmoid(x):
    return 1.0 / (1.0 + jnp.exp(-x))


def _split_hi_lo(x):
    hi = x.astype(BF16)
    lo = (x - hi.astype(F32)).astype(BF16)
    return hi, lo


def _ada_kernel(c_ref, w_ref, b_ref, o_ref):
    c = c_ref[...]
    s = (c * _sigmoid(c)).astype(BF16)
    o_ref[0] = _mm(s, w_ref[0].astype(BF16)) + b_ref[0]


def _ada_mod(cc, ada_w, ada_b):
    nl, d, n6 = ada_w.shape
    tn = 1024
    return pl.pallas_call(
        _ada_kernel,
        grid=(nl, n6 // tn),
        in_specs=[pl.BlockSpec((16, d), lambda l, j: (0, 0)),
                  pl.BlockSpec((1, d, tn), lambda l, j: (l, 0, j)),
                  pl.BlockSpec((1, 1, tn), lambda l, j: (l, 0, j))],
        out_specs=pl.BlockSpec((1, 16, tn), lambda l, j: (l, 0, j)),
        out_shape=jax.ShapeDtypeStruct((nl, 16, n6), F32),
        compiler_params=_cparams(("arbitrary", "arbitrary")),
        name="ada_mod",
    )(cc, ada_w, ada_b.reshape(nl, 1, n6))


def _mod_spec(part, ctx_first):
    def imap(b, j):
        row = jnp.where(j == 0, 8, b) if ctx_first else b
        return (row * 6 + part, 0, 0)
    return pl.BlockSpec((1, 1, D_MODEL), imap)


def _norm_mod_kernel(ctx_ref, x_ref, g_ref, sc_ref, sh_ref, xs_ref, o_ref):
    x = jnp.where(pl.program_id(1) == 0, ctx_ref[0], x_ref[0])
    xs_ref[0] = x
    ms = jnp.mean(x * x, axis=-1, keepdims=True)
    y = x * lax.rsqrt(ms + EPS) * g_ref[...]
    o_ref[0] = (y * (1.0 + sc_ref[0]) + sh_ref[0]).astype(o_ref.dtype)


def _norm_mod(ctx, x, g, mod):
    b, n, d = x.shape
    rb = ctx.shape[1] + n
    assert ctx.shape[1] == ROW_TILE
    row_spec = pl.BlockSpec((1, ROW_TILE, d), lambda i, j: (i, j, 0))
    return pl.pallas_call(
        _norm_mod_kernel,
        grid=(b, rb // ROW_TILE),
        in_specs=[pl.BlockSpec((1, ROW_TILE, d), lambda i, j: (i, 0, 0)),
                  pl.BlockSpec((1, ROW_TILE, d), lambda i, j: (i, jnp.maximum(j - 1, 0), 0)),
                  pl.BlockSpec((1, d), lambda i, j: (0, 0)),
                  _mod_spec(1, True), _mod_spec(0, True)],
        out_specs=[row_spec, row_spec],
        out_shape=[jax.ShapeDtypeStruct((b, rb, d), F32), jax.ShapeDtypeStruct((b, rb, d), BF16)],
        compiler_params=_cparams(("arbitrary", "arbitrary")),
        name="norm_mod",
    )(ctx, x, g.reshape(1, d), mod, mod)


def _matmul_kernel(x_ref, w_ref, o_ref):
    o_ref[...] = _mm(x_ref[...], w_ref[...]).astype(o_ref.dtype)


def _matmul(x, w, tm, tn, out_dtype):
    m, k = x.shape
    n = w.shape[1]
    return pl.pallas_call(
        _matmul_kernel,
        grid=(n // tn, m // tm),
        in_specs=[pl.BlockSpec((tm, k), lambda j, i: (i, 0)),
                  pl.BlockSpec((k, tn), lambda j, i: (0, j))],
        out_specs=pl.BlockSpec((tm, tn), lambda j, i: (i, j)),
        out_shape=jax.ShapeDtypeStruct((m, n), out_dtype),
        compiler_params=_cparams(("arbitrary", "arbitrary")),
        name="in_proj",
    )(x, w)


def _gla_kernel(q_ref, k_ref, v_ref, r_ref, ab_ref, waf_ref, wab_ref, ba_ref, g_ref, o_ref,
                laf_ref, lab_ref, of_ref, ob_ref, st_ref, *, n_ctx_chunks, n_chunks):
    c = GLA_CHUNK
    ab = ab_ref[0]
    zf = _mm(ab, waf_ref[...]) + ba_ref[0:1, :]
    zb = _mm(ab, wab_ref[...]) + ba_ref[1:2, :]
    laf_ref[...] = (jnp.minimum(zf, 0.0) - jnp.log(1.0 + jnp.exp(-jnp.abs(zf)))) * (1.0 / GLA_TAU)
    lab_ref[...] = (jnp.minimum(zb, 0.0) - jnp.log(1.0 + jnp.exp(-jnp.abs(zb)))) * (1.0 / GLA_TAU)
    st_ref[...] = jnp.zeros_like(st_ref)

    ri = lax.broadcasted_iota(jnp.int32, (c, c), 0)
    ci = lax.broadcasted_iota(jnp.int32, (c, c), 1)
    tri = [(ri >= ci), (ri <= ci)]
    tri_bf = [t.astype(F32).astype(BF16) for t in tri]
    tri4 = [jnp.concatenate([t] * GLA_HEADS, axis=0) for t in tri]
    lane_q = lax.broadcasted_iota(jnp.int32, (c, GLA_QK_W), 1) // GLA_DK
    lane_v = lax.broadcasted_iota(jnp.int32, (c, GLA_V_W), 1) // GLA_DV
    st_row_h = lax.broadcasted_iota(jnp.int32, (GLA_V_W, GLA_QK_W), 0) // GLA_DV
    st_col_h = lax.broadcasted_iota(jnp.int32, (GLA_V_W, GLA_QK_W), 1) // GLA_DK
    st_mask = st_row_h == st_col_h

    def one_dir(d, chunk, la_ref, out_ref):
        r0 = pl.multiple_of(chunk * c, c)
        la = la_ref[pl.ds(r0, c), :]
        hi, lo = _split_hi_lo(la)
        bc = _mm(tri_bf[d], hi) + _mm(tri_bf[d], lo)
        btot = bc[c - 1:c, :] if d == 0 else bc[0:1, :]
        q = q_ref[0, pl.ds(r0, c), :].astype(F32) * (GLA_DK ** -0.5)
        k = k_ref[0, pl.ds(r0, c), :].astype(F32)
        v = v_ref[0, pl.ds(r0, c), :]
        qd = (q * jnp.exp(bc)).astype(BF16)
        ki = (k * jnp.exp(-bc)).astype(BF16)
        kd = (k * jnp.exp(btot - bc)).astype(BF16)
        zero = jnp.zeros_like(qd)
        qs = jnp.concatenate([jnp.where(lane_q == h, qd, zero) for h in range(GLA_HEADS)], axis=0)
        att = _nt(qs, ki)
        att = jnp.where(tri4[d], att, 0.0).astype(BF16)
        rr = _mm(att, v)
        o = _nt(qd, st_ref[d].astype(BF16))
        for h in range(GLA_HEADS):
            o = o + jnp.where(lane_v == h, rr[h * c:(h + 1) * c, :], 0.0)
        out_ref[pl.ds(r0, c), :] = o
        upd = _tn(v, kd)
        st_ref[d] = jnp.where(st_mask, st_ref[d] * jnp.exp(btot) + upd, 0.0)

    def step(i, carry):
        one_dir(0, i, laf_ref, of_ref)
        cb = jnp.where(i < n_ctx_chunks, n_ctx_chunks - 1 - i, n_chunks + n_ctx_chunks - 1 - i)
        one_dir(1, cb, lab_ref, ob_ref)
        return carry

    lax.fori_loop(0, n_chunks, step, 0, unroll=4)

    def epilogue(j, carry):
        r0 = pl.multiple_of(j * ROW_TILE, ROW_TILE)
        o = of_ref[pl.ds(r0, ROW_TILE), :] + ob_ref[pl.ds(r0, ROW_TILE), :]
        r = r_ref[0, pl.ds(r0, ROW_TILE), :].astype(F32)
        gate = r * _sigmoid(r)
        for h in range(GLA_HEADS):
            sl = slice(h * GLA_DV, (h + 1) * GLA_DV)
            oh = o[:, sl]
            ms = jnp.mean(oh * oh, axis=-1, keepdims=True)
            y = oh * lax.rsqrt(ms + EPS) * g_ref[:, sl]
            o_ref[0, pl.ds(r0, ROW_TILE), sl] = (y * gate[:, sl]).astype(o_ref.dtype)
        return carry

    lax.fori_loop(0, (n_chunks * c) // ROW_TILE, epilogue, 0)


def _gla(p3, waf, wab, ba, g):
    b, rb, _ = p3.shape
    n_chunks = rb // GLA_CHUNK
    kern = functools.partial(_gla_kernel, n_ctx_chunks=CTX_LEN // GLA_CHUNK, n_chunks=n_chunks)

    def col(width, off):
        return pl.BlockSpec((1, rb, width), lambda i: (i, 0, off // width))

    return pl.pallas_call(
        kern,
        grid=(b,),
        in_specs=[col(GLA_QK_W, COL_GQ), col(GLA_QK_W, COL_GK), col(GLA_V_W, COL_GV), col(GLA_V_W, COL_GR),
                  col(LANE, COL_GA),
                  pl.BlockSpec((LANE, GLA_QK_W), lambda i: (0, 0)),
                  pl.BlockSpec((LANE, GLA_QK_W), lambda i: (0, 0)),
                  pl.BlockSpec((2, GLA_QK_W), lambda i: (0, 0)),
                  pl.BlockSpec((1, GLA_V_W), lambda i: (0, 0))],
        out_specs=pl.BlockSpec((1, rb, GLA_V_W), lambda i: (i, 0, 0)),
        out_shape=jax.ShapeDtypeStruct((b, rb, GLA_V_W), BF16),
        scratch_shapes=[pltpu.VMEM((rb, GLA_QK_W), F32), pltpu.VMEM((rb, GLA_QK_W), F32),
                        pltpu.VMEM((rb, GLA_V_W), F32), pltpu.VMEM((rb, GLA_V_W), F32),
                        pltpu.VMEM((2, GLA_V_W, GLA_QK_W), F32)],
        compiler_params=_cparams(("arbitrary",)),
        name="gla",
    )(p3, p3, p3, p3, p3, waf, wab, ba, g)


def _softmax_pv(parts):
    m = parts[0][0].max(axis=-1, keepdims=True)
    for s, _ in parts[1:]:
        m = jnp.maximum(m, s.max(axis=-1, keepdims=True))
    acc = None
    den = None
    for s, v in parts:
        e = jnp.exp(s - m)
        l = e.sum(axis=-1, keepdims=True)
        pv = _mm(e.astype(BF16), v)
        acc = pv if acc is None else acc + pv
        den = l if den is None else den + l
    return acc / den


assert float(np.log2(NA_SCALE)).is_integer()
NA_QR = 4
NA_KR = NA_QR + NA_KH


def _na_block_rows(n_rows):
    r0s = (0, 2 * NA_QR, n_rows - NA_QR)
    return [(r0, int(np.clip(r0 - NA_KH // 2, 0, n_rows - NA_KR))) for r0 in r0s]


def _na_kernel(q_ref, k_ref, v_ref, bias_ref, o_ref, *, n_rows):
    nq = NA_QR * GRID_W
    nk = NA_KR * GRID_W
    n_blocks = n_rows // NA_QR
    lane = lax.broadcasted_iota(jnp.int32, (nq, LANE), 1)
    head_mask = [lane < NA_HEAD_DIM, lane >= NA_HEAD_DIM]
    kc = k_ref[0, 0:CTX_LEN, :]
    vc = v_ref[0, 0:CTX_LEN, :]

    def block(bi, carry):
        r0 = bi * NA_QR
        ks = jnp.clip(r0 - NA_KH // 2, 0, n_rows - NA_KR)
        var = jnp.where(bi == 0, 0, jnp.where(bi == n_blocks - 1, 2, 1))
        q0 = pl.multiple_of(CTX_LEN + r0 * GRID_W, GRID_W)
        k0 = pl.multiple_of(CTX_LEN + ks * GRID_W, GRID_W)
        q = q_ref[0, pl.ds(q0, nq), :] * NA_SCALE
        kl = k_ref[0, pl.ds(k0, nk), :]
        vl = v_ref[0, pl.ds(k0, nk), :]
        outs = []
        for hh in range(2):
            qm = jnp.where(head_mask[hh], q, jnp.zeros_like(q))
            s_loc = _nt(qm, kl) + bias_ref[hh, var]
            s_ctx = _nt(qm, kc)
            outs.append(_softmax_pv([(s_loc, vl), (s_ctx, vc)]))
        o_ref[0, pl.ds(q0, nq), :] = jnp.where(head_mask[0], outs[0], outs[1]).astype(o_ref.dtype)
        return carry

    lax.fori_loop(0, n_blocks, block, 0, unroll=2)

    qc = q_ref[0, 0:CTX_LEN, :]
    lane_c = lax.broadcasted_iota(jnp.int32, (CTX_LEN, LANE), 1)
    outs = []
    for hh in range(2):
        msk = (lane_c < NA_HEAD_DIM) if hh == 0 else (lane_c >= NA_HEAD_DIM)
        qm = jnp.where(msk, qc, jnp.zeros_like(qc))
        outs.append(_softmax_pv([(_nt(qm, kc) * NA_SCALE, vc)]))
    o_ref[0, 0:CTX_LEN, :] = jnp.where(lane_c < NA_HEAD_DIM, outs[0], outs[1]).astype(o_ref.dtype)


def _na_bias_table(rpb, n_rows):
    qcol = np.arange(GRID_W)[:, None]
    kcol = np.arange(GRID_W)[None, :]
    wstart = np.clip(qcol - NA_KW // 2, 0, GRID_W - NA_KW)
    in_win = (kcol >= wstart) & (kcol < wstart + NA_KW)
    dc = kcol - qcol + NA_KW - 1
    col_hot = ((dc[:, :, None] == np.arange(2 * NA_KW - 1)) & in_win[:, :, None]).astype(np.float32)
    geo = _na_block_rows(n_rows)
    row_hot = np.zeros((len(geo), NA_QR, NA_KR, 2 * NA_KH - 1), np.float32)
    for v, (r0, ks) in enumerate(geo):
        for a in range(NA_QR):
            rs = int(np.clip(r0 + a - NA_KH // 2, 0, n_rows - NA_KH))
            for i in range(NA_KR):
                if rs <= ks + i < rs + NA_KH:
                    row_hot[v, a, i, ks + i - (r0 + a) + NA_KH - 1] = 1.0
    t = jnp.einsum('hrc,vair,qkc->hvaqik', rpb.astype(F32), jnp.asarray(row_hot), jnp.asarray(col_hot),
                   precision=lax.Precision.HIGHEST)
    inside = (row_hot.sum(-1) > 0)[:, :, None, :, None] & in_win[None, None, :, None, :]
    t = jnp.where(inside[None], t, NEG_INF)
    return t.reshape(NA_HEADS, len(geo), NA_QR * GRID_W, NA_KR * GRID_W)


def _na(p3, bias):
    b, rb, _ = p3.shape
    n_rows = (rb - CTX_LEN) // GRID_W
    kern = functools.partial(_na_kernel, n_rows=n_rows)

    def col(off):
        return pl.BlockSpec((1, rb, LANE), lambda p, i: (i, 0, off // LANE + p))

    return pl.pallas_call(
        kern,
        grid=(NA_HEADS // 2, b),
        in_specs=[col(COL_NQ), col(COL_NK), col(COL_NV),
                  pl.BlockSpec((2,) + bias.shape[1:], lambda p, i: (p, 0, 0, 0))],
        out_specs=pl.BlockSpec((1, rb, LANE), lambda p, i: (i, 0, p)),
        out_shape=jax.ShapeDtypeStruct((b, rb, NA_W), BF16),
        compiler_params=_cparams(("arbitrary", "arbitrary")),
        name="na",
    )(p3, p3, p3, bias)


def _mla_proj_kernel(cq_ref, ckv_ref, kr_ref, krp_ref, cos_ref, sin_ref, gq_ref, gkv_ref,
                     wq_ref, wq2_ref, wkv_ref, q_ref, k_ref, v_ref):
    def norm(x, g):
        x = x.astype(F32)
        ms = jnp.mean(x * x, axis=-1, keepdims=True)
        return (x * lax.rsqrt(ms + EPS) * g).astype(BF16)

    cos = cos_ref[...]
    sin = sin_ref[...]
    nq = norm(cq_ref[0], gq_ref[...])
    yq = _mm(nq, wq_ref[...])
    yq2 = _mm(nq, wq2_ref[...])
    nkv = norm(ckv_ref[0], gkv_ref[...])
    ykv = _mm(nkv, wkv_ref[...])
    k_rot = (kr_ref[0].astype(F32) * cos + krp_ref[0].astype(F32) * sin).astype(BF16)
    for h in range(MLA_HEADS):
        a = 2 * h * LANE
        q_ref[0, :, a:a + LANE] = (yq[:, a:a + LANE] * MLA_SCALE).astype(BF16)
        rot = yq[:, a + LANE:a + 2 * LANE] * cos + yq2[:, h * LANE:(h + 1) * LANE] * sin
        q_ref[0, :, a + LANE:a + 2 * LANE] = (rot * MLA_SCALE).astype(BF16)
        k_ref[0, :, a:a + LANE] = ykv[:, h * LANE:(h + 1) * LANE].astype(BF16)
        k_ref[0, :, a + LANE:a + 2 * LANE] = k_rot
        v0 = MLA_HEADS * MLA_NOPE + h * MLA_V
        v_ref[0, :, a:a + LANE] = ykv[:, v0:v0 + MLA_V].astype(BF16)
        v_ref[0, :, a + LANE:a + 2 * LANE] = jnp.ones((ykv.shape[0], LANE), BF16)


def _mla_proj(p3, cos_t, sin_t, gq, gkv, wq, wq2, wkv):
    b, rb, _ = p3.shape
    tm = ROW_TILE

    def col(width, off):
        return pl.BlockSpec((1, tm, width), lambda i, j: (i, j, off // width))

    def full(a):
        return pl.BlockSpec(a.shape, lambda i, j: (0, 0))

    hw = MLA_HEADS * 2 * LANE
    nj = rb // tm
    return pl.pallas_call(
        _mla_proj_kernel,
        grid=(b, nj),
        in_specs=[col(MLA_Q_RANK, COL_MCQ), col(MLA_KV_RANK, COL_MCKV), col(LANE, COL_KR), col(LANE, COL_KRP),
                  pl.BlockSpec((tm, LANE), lambda i, j: (j, 0)), pl.BlockSpec((tm, LANE), lambda i, j: (j, 0)),
                  full(gq), full(gkv), full(wq), full(wq2), full(wkv)],
        out_specs=[pl.BlockSpec((1, tm, hw), lambda i, j: (i, (j + nj - 1) % nj, 0)),
                   pl.BlockSpec((1, tm, hw), lambda i, j: (i, (j + nj - 1) % nj, 0)),
                   pl.BlockSpec((1, tm, hw), lambda i, j: (i, (j + nj - 1) % nj, 0))],
        out_shape=[jax.ShapeDtypeStruct((b, rb, hw), BF16), jax.ShapeDtypeStruct((b, rb, hw), BF16),
                   jax.ShapeDtypeStruct((b, rb, hw), BF16)],
        compiler_params=_cparams(("arbitrary", "arbitrary")),
        name="mla_proj",
    )(p3, p3, p3, p3, cos_t, sin_t, gq, gkv, wq, wq2, wkv)


def _mla_attn_kernel(q_ref, k_ref, v_ref, o_ref):
    k = k_ref[0]
    v = v_ref[0]
    tq = q_ref.shape[1]
    sub = min(tq, MLA_CHAIN)
    for r0 in range(0, tq, sub):
        s = _nt(q_ref[0, r0:r0 + sub, :], k)
        e = jnp.exp((s - s.max(axis=-1, keepdims=True)).astype(BF16))
        pv = _mm(e, v)
        o_ref[0, r0:r0 + sub, :] = (pv[:, :MLA_V] / pv[:, MLA_V:]).astype(o_ref.dtype)


def _mla_attn(q, k, v, ctx_queries):
    b, rb, _ = q.shape
    n_lat = rb - CTX_LEN
    if ctx_queries:
        tq, nq, n_keys = CTX_LEN, 1, CTX_LEN
        q_off = kv_blk = n_lat // CTX_LEN
    else:
        tq = MLA_TQ if n_lat % MLA_TQ == 0 else ROW_TILE
        nq, n_keys, q_off, kv_blk = n_lat // tq, rb, 0, 0
    return pl.pallas_call(
        _mla_attn_kernel,
        grid=(b, MLA_HEADS, nq),
        in_specs=[pl.BlockSpec((1, tq, 2 * LANE), lambda i, h, j: (i, j + q_off, h)),
                  pl.BlockSpec((1, n_keys, 2 * LANE), lambda i, h, j: (i, kv_blk, h)),
                  pl.BlockSpec((1, n_keys, 2 * LANE), lambda i, h, j: (i, kv_blk, h))],
        out_specs=pl.BlockSpec((1, tq, MLA_V), lambda i, h, j: (i, j, h)),
        out_shape=jax.ShapeDtypeStruct((b, nq * tq, MLA_V_W), BF16),
        compiler_params=_cparams(("arbitrary", "arbitrary", "arbitrary")),
        name="mla_attn_ctx" if ctx_queries else "mla_attn",
    )(q, k, v)


def _merge_kernel(x_ref, oa_ref, ob_ref, oc_ref, occ_ref, ga_ref, gb_ref, gc_ref, bg_ref,
                  wa_ref, wb_ref, wc_ref, wo_ref, gt1_ref, sc2_ref, sh2_ref, g2_ref, rw_ref, rb_ref,
                  xo_ref, h_ref, lg_ref, *, ctx_tile):
    d = D_MODEL

    wh, wl = _split_hi_lo(rw_ref[...])
    tm = x_ref.shape[1]
    sub = tm
    for r0 in range(0, tm, sub):
        rows = slice(r0, r0 + sub)

        def gated(g_ref, k, br):
            z = g_ref[0, rows, :] + bg_ref[:, k * d:(k + 1) * d].astype(BF16)
            return _sigmoid(z) * br.astype(BF16)

        oc = oc_ref[0, rows, :]
        if ctx_tile:
            oc = jnp.where(pl.program_id(1) == 0, occ_ref[0, rows, :], oc)
        m = gated(ga_ref, 0, _mm(oa_ref[0, rows, :], wa_ref[...]))
        m = m + gated(gb_ref, 1, _mm(ob_ref[0, rows, :], wb_ref[...]))
        m = m + gated(gc_ref, 2, _mm(oc, wc_ref[...]))
        y = _mm(m, wo_ref[...])
        x = x_ref[0, rows, :] + gt1_ref[0] * y
        xo_ref[0, rows, :] = x
        ms = jnp.mean(x * x, axis=-1, keepdims=True)
        h = x * lax.rsqrt(ms + EPS) * g2_ref[...]
        h = h * (1.0 + sc2_ref[0]) + sh2_ref[0]
        h_ref[rows, :] = h.astype(h_ref.dtype)
        hh, hl = _split_hi_lo(h)
        lg_ref[:, rows] = _nt(wh, hh) + _nt(wh, hl) + _nt(wl, hh) + rb_ref[:, 0:1]


def _merge(x, p3, og, on, om, om_ctx, bg, wa, wb, wc, wo, mod, g2, rwt, rbias, skip_ctx):
    b, rb, d = x.shape
    tm = ROW_TILE
    jo = 1 if skip_ctx else 0
    nj = rb // tm - jo
    rows_out = nj * tm

    def rows(width, cblk=0):
        return pl.BlockSpec((1, tm, width), lambda i, j: (i, j + jo, cblk))

    om_spec = pl.BlockSpec((1, tm, MLA_V_W), lambda i, j: (i, jnp.maximum(j + jo - 1, 0), 0))
    omc_spec = pl.BlockSpec((1, tm, MLA_V_W), lambda i, j: (i, 0, 0))
    kern = functools.partial(_merge_kernel, ctx_tile=not skip_ctx)

    def full(a):
        return pl.BlockSpec(a.shape, lambda i, j: (0,) * a.ndim, pipeline_mode=pl.Buffered(1))

    def mod_spec(part):
        def imap(i, j):
            row = i if skip_ctx else jnp.where(j == 0, 8, i)
            return (row * 6 + part, 0, 0)
        return pl.BlockSpec((1, 1, d), imap)

    return pl.pallas_call(
        kern,
        grid=(b, nj),
        in_specs=[rows(d), rows(GLA_V_W), rows(NA_W), om_spec, omc_spec,
                  rows(d, 0), rows(d, 1), rows(d, 2), full(bg),
                  full(wa), full(wb), full(wc), full(wo),
                  mod_spec(2), mod_spec(4), mod_spec(3), full(g2), full(rwt), full(rbias)],
        out_specs=[pl.BlockSpec((1, tm, d), lambda i, j: (i, j, 0)),
                   pl.BlockSpec((tm, d), lambda i, j: (i * nj + j, 0)),
                   pl.BlockSpec((N_EXPERTS, tm), lambda i, j: (0, i * nj + j))],
        out_shape=[jax.ShapeDtypeStruct((b, rows_out, d), F32),
                   jax.ShapeDtypeStruct((b * rows_out, d), BF16),
                   jax.ShapeDtypeStruct((N_EXPERTS, b * rows_out), F32)],
        compiler_params=_cparams(("arbitrary", "arbitrary")),
        name="merge",
    )(x, og, on, om, om if skip_ctx else om_ctx, p3, p3, p3, bg, wa, wb, wc, wo, mod, mod, mod, g2, rwt, rbias)


def _route_kernel(l_ref, idx_ref, w_ref, rank_ref, cnt_ref, carry_ref):
    i = pl.program_id(0)
    tt = l_ref.shape[1]

    @pl.when(i == 0)
    def _():
        carry_ref[...] = jnp.zeros_like(carry_ref)

    l = l_ref[...]
    eio = lax.broadcasted_iota(jnp.int32, (N_EXPERTS, tt), 0)
    vals, idxs = [], []
    for _ in range(TOP_K):
        m = l.max(axis=0, keepdims=True)
        ik = jnp.min(jnp.where(l == m, eio, N_EXPERTS), axis=0, keepdims=True)
        vals.append(m)
        idxs.append(ik)
        l = jnp.where(eio == ik, -jnp.inf, l)
    es = [jnp.exp(v - vals[0]) for v in vals]
    den = es[0] + es[1] + es[2] + es[3]
    sel = jnp.zeros((N_EXPERTS, tt), F32)
    for ik in idxs:
        sel = sel + (eio == ik).astype(F32)
    si = lax.broadcasted_iota(jnp.int32, (tt, tt), 0)
    ti = lax.broadcasted_iota(jnp.int32, (tt, tt), 1)
    before = (si < ti).astype(F32).astype(BF16)
    rank_full = _mm(sel.astype(BF16), before) + carry_ref[:, 0:1]
    for k in range(TOP_K):
        idx_ref[k:k + 1, :] = idxs[k]
        w_ref[k:k + 1, :] = es[k] / den
        rk = jnp.sum(jnp.where(eio == idxs[k], rank_full, 0.0), axis=0, keepdims=True)
        rank_ref[k:k + 1, :] = rk.astype(jnp.int32)
    carry_ref[...] = carry_ref[...] + jnp.sum(sel, axis=1, keepdims=True)
    cnt_ref[...] = carry_ref[...]


def _route(logits_t):
    ne, t = logits_t.shape
    tt = ROUTE_TT
    spec4 = pl.BlockSpec((TOP_K, tt), lambda i: (0, i))
    return pl.pallas_call(
        _route_kernel,
        grid=(t // tt,),
        in_specs=[pl.BlockSpec((ne, tt), lambda i: (0, i))],
        out_specs=[spec4, spec4, spec4, pl.BlockSpec((ne, LANE), lambda i: (0, 0))],
        out_shape=[jax.ShapeDtypeStruct((TOP_K, t), jnp.int32), jax.ShapeDtypeStruct((TOP_K, t), F32),
                   jax.ShapeDtypeStruct((TOP_K, t), jnp.int32), jax.ShapeDtypeStruct((ne, LANE), F32)],
        scratch_shapes=[pltpu.VMEM((ne, LANE), F32)],
        compiler_params=_cparams(("arbitrary",)),
        name="route",
    )(logits_t)


def _pad_fill(ps_ref, pn_ref, zero_ref, xg_ref, sem, wait):
    def copy(pos, rows):
        cp = pltpu.make_async_copy(zero_ref.at[pl.ds(0, rows), :], xg_ref.at[pl.ds(pos, rows), :], sem)
        cp.wait() if wait else cp.start()

    def per_expert(e, carry):
        pos = ps_ref[e]
        head = (-pos) & (SUBLANE - 1)
        for r in range(SUBLANE - 1):
            @pl.when(r < head)
            def _(r=r):
                copy(pos + r, 1)

        pos = pos + head
        n = pn_ref[e] - head
        bit = MOE_G // 2
        while bit >= SUBLANE:
            on = (n & bit) != 0

            @pl.when(on)
            def _(pos=pos, bit=bit):
                copy(pl.multiple_of(pos, SUBLANE), bit)

            pos = pos + jnp.where(on, bit, 0)
            bit //= 2
        return carry

    lax.fori_loop(0, N_EXPERTS, per_expert, 0)

    zr = zero_ref.shape[0]

    def tail(i, carry):
        pos = pl.multiple_of(ps_ref[N_EXPERTS] + i * zr, zr)
        cp = pltpu.make_async_copy(zero_ref, xg_ref.at[pl.ds(pos, zr), :], sem)
        cp.wait() if wait else cp.start()
        return carry

    lax.fori_loop(0, pn_ref[N_EXPERTS], tail, 0)


def _dispatch_kernel(ps_ref, pn_ref, dest_ref, h_ref, xg_ref, zero_ref, hf_ref, sem, zsem):
    tt = h_ref.shape[0]

    @pl.when(pl.program_id(0) == 0)
    def _():
        zero_ref[...] = jnp.zeros_like(zero_ref)
        _pad_fill(ps_ref, pn_ref, zero_ref, xg_ref, zsem, wait=False)

    hf_ref[...] = h_ref[...].astype(hf_ref.dtype)

    def issue(t, carry):
        for k in range(TOP_K):
            pltpu.make_async_copy(hf_ref.at[pl.ds(t, 1), :], xg_ref.at[pl.ds(dest_ref[0, k, t], 1), :],
                                  sem).start(priority=k % 2)
        return carry

    lax.fori_loop(0, tt, issue, 0, unroll=4)
    for k in range(TOP_K):
        pltpu.make_async_copy(hf_ref, xg_ref.at[pl.ds(0, tt), :], sem).wait()

    @pl.when(pl.program_id(0) == 0)
    def _():
        _pad_fill(ps_ref, pn_ref, zero_ref, xg_ref, zsem, wait=True)


def _dispatch(hp, dest, pad_start, pad_len, n_slots):
    t, w = hp.shape
    tt = DISPATCH_TT
    dest3 = dest.reshape(TOP_K, t // tt, tt).transpose(1, 0, 2)
    grid_spec = pltpu.PrefetchScalarGridSpec(
        num_scalar_prefetch=2,
        grid=(t // tt,),
        in_specs=[pl.BlockSpec((1, TOP_K, tt), lambda i, ps, pn: (i, 0, 0), memory_space=pltpu.SMEM),
                  pl.BlockSpec((tt, w), lambda i, ps, pn: (i, 0))],
        out_specs=pl.BlockSpec(memory_space=pl.ANY),
        scratch_shapes=[pltpu.VMEM((MOE_G // 2, w), F32), pltpu.VMEM((tt, w), F32),
                        pltpu.SemaphoreType.DMA(()), pltpu.SemaphoreType.DMA(())],
    )
    return pl.pallas_call(
        _dispatch_kernel,
        grid_spec=grid_spec,
        out_shape=jax.ShapeDtypeStruct((n_slots, w), F32),
        compiler_params=_cparams(("arbitrary",)),
        name="dispatch",
    )(pad_start, pad_len, dest3, hp)


def _ffn_kernel(te_ref, tv_ref, x_ref, w1g_ref, w1l_ref, b1g_ref, b1l_ref, w2_ref, b2_ref, y_ref,
                xb_ref, *, n_fc, n_tiles):
    i = pl.program_id(0)
    j = pl.program_id(1)
    valid = tv_ref[i]
    g = y_ref.shape[0]
    chunk = x_ref.shape[0]

    @pl.when(i < n_tiles)
    def _():
        xb_ref[i % 2, pl.ds(pl.multiple_of(j * chunk, chunk), chunk), :] = x_ref[...].astype(BF16)

    cur = (i + 1) % 2

    @pl.when((valid > 0) & (j == 0))
    def _():
        y_ref[...] = jnp.broadcast_to(b2_ref[0, 0], y_ref.shape)

    for nr in range(MOE_SB, g + 1, MOE_SB):
        @pl.when((valid > nr - MOE_SB) & (valid <= nr))
        def _(nr=nr):
            wg = w1g_ref[0, 0].astype(BF16)
            wl = w1l_ref[0, 0].astype(BF16)
            w2 = w2_ref[0, 0].astype(BF16)
            for r0 in range(0, nr, MOE_CHAIN):
                r1 = min(r0 + MOE_CHAIN, nr)
                x = xb_ref[cur, r0:r1, :]
                ug = _mm(x, wg) + b1g_ref[0, 0]
                ul = _mm(x, wl) + b1l_ref[0, 0]
                xg = jnp.minimum(ug, SWIGLU_LIMIT)
                xl = jnp.clip(ul, -SWIGLU_LIMIT, SWIGLU_LIMIT)
                act = xg * _sigmoid(SWIGLU_ALPHA * xg) * (xl + 1.0)
                y_ref[r0:r1, :] = y_ref[r0:r1, :] + _mm(act.astype(BF16), w2)

    @pl.when((j == n_fc - 1) & (valid == 0))
    def _():
        y_ref[...] = jnp.zeros_like(y_ref)


def _ffn(layer, tile_expert, tile_valid, xg, w1, b1, w2, b2):
    n_slots, d = xg.shape
    nl, ne, _, ff2 = w1.shape
    ff = ff2 // 2
    n_fc = ff // MOE_FC
    n_tiles = n_slots // MOE_G
    chunk = MOE_G // n_fc
    kern = functools.partial(_ffn_kernel, n_fc=n_fc, n_tiles=n_tiles)
    tile_expert = jnp.concatenate([tile_expert[:1], tile_expert])
    tile_valid = jnp.concatenate([jnp.zeros((1,), tile_valid.dtype), tile_valid])

    def jj(j, tv, i):
        return jnp.where(tv[i] > 0, j, n_fc - 1)

    grid_spec = pltpu.PrefetchScalarGridSpec(
        num_scalar_prefetch=2,
        grid=(n_tiles + 1, n_fc),
        in_specs=[pl.BlockSpec((chunk, d), lambda i, j, te, tv: (jnp.minimum(i, n_tiles - 1) * n_fc + j, 0)),
                  pl.BlockSpec((1, 1, d, MOE_FC), lambda i, j, te, tv: (layer, te[i], 0, jj(j, tv, i))),
                  pl.BlockSpec((1, 1, d, MOE_FC), lambda i, j, te, tv: (layer, te[i], 0, jj(j, tv, i) + n_fc)),
                  pl.BlockSpec((1, 1, 1, MOE_FC), lambda i, j, te, tv: (layer, te[i], 0, jj(j, tv, i))),
                  pl.BlockSpec((1, 1, 1, MOE_FC), lambda i, j, te, tv: (layer, te[i], 0, jj(j, tv, i) + n_fc)),
                  pl.BlockSpec((1, 1, MOE_FC, d), lambda i, j, te, tv: (layer, te[i], jj(j, tv, i), 0)),
                  pl.BlockSpec((1, 1, 1, d), lambda i, j, te, tv: (layer, te[i], 0, 0))],
        out_specs=pl.BlockSpec((MOE_G, d), lambda i, j, te, tv: (jnp.maximum(i - 1, 0), 0)),
        scratch_shapes=[pltpu.VMEM((2, MOE_G, d), BF16)],
    )
    return pl.pallas_call(
        kern,
        grid_spec=grid_spec,
        out_shape=jax.ShapeDtypeStruct((n_slots, d), F32),
        compiler_params=_cparams(("arbitrary", "arbitrary"), FFN_VMEM_LIMIT),
        name="moe_ffn",
    )(tile_expert, tile_valid, xg, w1, w1, b1.reshape(nl, ne, 1, ff2), b1.reshape(nl, ne, 1, ff2), w2,
      b2.reshape(nl, ne, 1, d))


def _combine_kernel(dest_ref, destn_ref, x_ref, wt_ref, gt2_ref, gn_ref, sc_ref, sh_ref, yg_ref, o_ref, *rest,
                    final_norm):
    h_ref = None if final_norm else rest[0]
    buf_ref, sem = rest[-2:]
    i = pl.program_id(0)
    n = pl.num_programs(0)
    tt = x_ref.shape[1]
    slot = i % 2

    def issue(d_ref, s):
        def body(t, carry):
            for k in range(TOP_K):
                pltpu.make_async_copy(yg_ref.at[pl.ds(d_ref[0, k, t], 1), :],
                                      buf_ref.at[s, k, pl.ds(t, 1), :], sem.at[s]).start(priority=k % 2)
            return carry
        lax.fori_loop(0, tt, body, 0, unroll=4)

    @pl.when(i == 0)
    def _():
        issue(dest_ref, 0)

    for s in range(2):
        @pl.when((i + 1 < n) & (slot == 1 - s))
        def _(s=s):
            issue(destn_ref, s)

    for k in range(TOP_K):
        pltpu.make_async_copy(yg_ref.at[pl.ds(0, tt), :], buf_ref.at[slot, k], sem.at[slot]).wait()

    wt = wt_ref[...]
    y = buf_ref[slot, 0] * wt[:, 0:1]
    for k in range(1, TOP_K):
        y = y + buf_ref[slot, k] * wt[:, k:k + 1]
    x = x_ref[0] + gt2_ref[0] * y
    ms = jnp.mean(x * x, axis=-1, keepdims=True)
    xn = x * lax.rsqrt(ms + EPS) * gn_ref[...]
    if final_norm:
        o_ref[0] = xn
    else:
        o_ref[0] = x
        h_ref[0] = (xn * (1.0 + sc_ref[0]) + sh_ref[0]).astype(h_ref.dtype)


def _combine(x, yg, dest, wts, mod, gn, mod_next, ctx_first, final_norm):
    b, rows, d = x.shape
    tt = COMBINE_TT
    nj = rows // tt
    t = b * rows
    nt = t // tt
    dest3 = dest.reshape(TOP_K, nt, tt).transpose(1, 0, 2)
    wt = wts.T
    kern = functools.partial(_combine_kernel, final_norm=final_norm)

    def mod_spec(part):
        def imap(i):
            bi = i // nj
            row = jnp.where((i % nj) * tt < CTX_LEN, 8, bi) if ctx_first else bi
            return (row * 6 + part, 0, 0)
        return pl.BlockSpec((1, 1, d), imap)

    row_spec = pl.BlockSpec((1, tt, d), lambda i: (i // nj, i % nj, 0))
    out_specs = [row_spec]
    out_shape = [jax.ShapeDtypeStruct((b, rows, d), F32)]
    if not final_norm:
        out_specs.append(row_spec)
        out_shape.append(jax.ShapeDtypeStruct((b, rows, d), BF16))

    return pl.pallas_call(
        kern,
        grid=(nt,),
        in_specs=[pl.BlockSpec((1, TOP_K, tt), lambda i: (i, 0, 0), memory_space=pltpu.SMEM),
                  pl.BlockSpec((1, TOP_K, tt), lambda i: (jnp.minimum(i + 1, nt - 1), 0, 0),
                               memory_space=pltpu.SMEM),
                  row_spec,
                  pl.BlockSpec((tt, TOP_K), lambda i: (i, 0)),
                  mod_spec(5),
                  pl.BlockSpec((1, d), lambda i: (0, 0)),
                  mod_spec(1), mod_spec(0),
                  pl.BlockSpec(memory_space=pl.ANY)],
        out_specs=out_specs,
        out_shape=out_shape,
        scratch_shapes=[pltpu.VMEM((2, TOP_K, tt, d), yg.dtype), pltpu.SemaphoreType.DMA((2,))],
        compiler_params=_cparams(("arbitrary",)),
        name="combine",
    )(dest3, dest3, x, wt, mod, gn.reshape(1, d), mod_next, mod_next, yg)


def _proj_weight(w_in):
    d = w_in.shape[0]
    splits = (GLA_QK_W, GLA_QK_W, GLA_V_W, GLA_V_W, GLA_GATE_RANK, GLA_GATE_RANK,
              NA_W, NA_W, NA_W, MLA_Q_RANK, MLA_KV_RANK, MLA_ROPE, N_BRANCH * D_MODEL)
    pts = np.cumsum((0,) + splits)
    (gq, gk, gv, gr, gaf, gab, nq, nk, nv, mcq, mckv, mkr, gate) = [w_in[:, pts[i]:pts[i + 1]] for i in range(13)]
    q16 = MLA_ROPE // 4
    mkrp = jnp.concatenate([mkr[:, q16:2 * q16], mkr[:, :q16], mkr[:, 3 * q16:], mkr[:, 2 * q16:3 * q16]], axis=1)
    z = lambda n: jnp.zeros((d, n), w_in.dtype)
    cols = [gate, gq, gk, gv, gr, nq, nk, nv, mcq, mckv,
            mkr, z(LANE - MLA_ROPE), mkrp, z(LANE - MLA_ROPE),
            gaf, gab, z(LANE - 2 * GLA_GATE_RANK)]
    w = jnp.concatenate(cols, axis=1)
    w = jnp.concatenate([w, z(PROJ_W - w.shape[1])], axis=1)
    return w.astype(BF16)


def _rope_tables(rb):
    n = rb - CTX_LEN
    t = np.arange(n)
    nf = MLA_ROPE // 4
    freqs = ROPE_BASE ** (-np.arange(nf, dtype=np.float64) / nf)
    cos = np.zeros((rb, LANE), np.float32)
    sin = np.zeros((rb, LANE), np.float32)
    cos[:CTX_LEN, :MLA_ROPE] = 1.0
    for a, pos in enumerate((t // GRID_W, t % GRID_W)):
        ang = (pos.astype(np.float32)[:, None] * freqs.astype(np.float32)[None, :]).astype(np.float32)
        c, s = np.cos(ang), np.sin(ang)
        base = a * 2 * nf
        cos[CTX_LEN:, base:base + nf] = c
        cos[CTX_LEN:, base + nf:base + 2 * nf] = c
        sin[CTX_LEN:, base:base + nf] = -s
        sin[CTX_LEN:, base + nf:base + 2 * nf] = s
    return jnp.asarray(cos), jnp.asarray(sin)


def _mla_weights(w_q_up, w_kv_up):
    r = w_q_up.shape[0]
    wq = w_q_up.reshape(r, MLA_HEADS, MLA_NOPE + MLA_ROPE)
    nope, rope = wq[..., :MLA_NOPE], wq[..., MLA_NOPE:]
    q16 = MLA_ROPE // 4
    ropep = jnp.concatenate([rope[..., q16:2 * q16], rope[..., :q16], rope[..., 3 * q16:], rope[..., 2 * q16:3 * q16]],
                            axis=-1)
    zpad = jnp.zeros((r, MLA_HEADS, LANE - MLA_ROPE), w_q_up.dtype)
    wq1 = jnp.concatenate([nope, rope, zpad], axis=-1).reshape(r, MLA_HEADS * 2 * LANE).astype(BF16)
    wq2 = jnp.concatenate([ropep, zpad], axis=-1).reshape(r, MLA_HEADS * LANE).astype(BF16)
    rk = w_kv_up.shape[0]
    wkv = w_kv_up.reshape(rk, MLA_HEADS, 2, MLA_NOPE).transpose(0, 2, 1, 3).reshape(rk, 2 * MLA_HEADS * MLA_NOPE)
    return wq1, wq2, wkv.astype(BF16)


def _moe_plan(idx, rank, counts, n_tiles):
    cnt = counts[:, 0].astype(jnp.int32)
    padded = ((cnt + MOE_G - 1) // MOE_G) * MOE_G
    ends = jnp.cumsum(padded)
    starts = ends - padded
    e_ids = jnp.arange(N_EXPERTS, dtype=jnp.int32)
    dest = jnp.sum(jnp.where(idx[..., None] == e_ids, starts, 0), axis=-1) + rank
    tile_start = jnp.arange(n_tiles, dtype=jnp.int32) * MOE_G
    te = jnp.sum((tile_start[:, None] >= ends[None, :]).astype(jnp.int32), axis=1)
    active = te < N_EXPERTS
    te_c = jnp.minimum(te, N_EXPERTS - 1)
    tile_is = te_c[:, None] == e_ids[None, :]
    cnt_t = jnp.sum(jnp.where(tile_is, cnt, 0), axis=1)
    start_t = jnp.sum(jnp.where(tile_is, starts, 0), axis=1)
    valid = jnp.clip(cnt_t - (tile_start - start_t), 0, MOE_G)
    valid = jnp.where(active, valid, 0)
    last_e = jnp.max(jnp.where(cnt > 0, jnp.arange(N_EXPERTS, dtype=jnp.int32), 0))
    te_f = jnp.where(active, te_c, last_e)
    tail_blocks = (n_tiles * MOE_G - ends[-1]) // (MOE_G // 2)
    pad_start = jnp.concatenate([starts + cnt, ends[-1:]]).astype(jnp.int32)
    pad_len = jnp.concatenate([padded - cnt, tail_blocks[None]]).astype(jnp.int32)
    return dest, te_f, valid, pad_start, pad_len


def kernel(x, c, ctx, c_ctx, norm1_g, norm2_g, ada_w, ada_b, w_in, b_gate, gla_wa, gla_ba, gla_norm_g,
           na_rpb, mla_q_norm_g, mla_w_q_up, mla_kv_norm_g, mla_w_kv_up, w_branch_gla, w_branch_na,
           w_branch_mla, w_out, router_w, router_b, moe_w1, moe_b1, moe_w2, moe_b2, final_norm_g):
    b, n, d = x.shape
    rb = CTX_LEN + n
    assert b <= 8 and d == D_MODEL and ctx.shape[1] == CTX_LEN

    cc = jnp.zeros((16, d), F32).at[:b].set(c).at[8].set(c_ctx)
    mod_all = _ada_mod(cc, ada_w, ada_b)
    cos_t, sin_t = _rope_tables(rb)

    for l in range(DEPTH):
        last = l == DEPTH - 1
        mod = mod_all[l].reshape(16 * 6, 1, d)
        if l == 0:
            xs, h = _norm_mod(ctx, x, norm1_g[l], mod)
        p = _matmul(h.reshape(b * rb, d), _proj_weight(w_in[l]), 1024 if (b * rb) % 1024 == 0 else ROW_TILE,
                    PROJ_TN, BF16)
        p3 = p.reshape(b, rb, PROJ_W)

        zpad = jnp.zeros((LANE - 2 * GLA_GATE_RANK, GLA_QK_W), F32)
        zr = jnp.zeros((GLA_GATE_RANK, GLA_QK_W), F32)
        waf = jnp.concatenate([gla_wa[l, 0], zr, zpad], axis=0).astype(BF16)
        wab = jnp.concatenate([zr, gla_wa[l, 1], zpad], axis=0).astype(BF16)
        og = _gla(p3, waf, wab, gla_ba[l], gla_norm_g[l].reshape(1, GLA_V_W))

        on = _na(p3, _na_bias_table(na_rpb[l], n // GRID_W))

        wq1, wq2, wkv = _mla_weights(mla_w_q_up[l], mla_w_kv_up[l])
        q_m, k_m, v_m = _mla_proj(p3, cos_t, sin_t, mla_q_norm_g[l].reshape(1, -1), mla_kv_norm_g[l].reshape(1, -1),
                                  wq1, wq2, wkv)
        om = _mla_attn(q_m, k_m, v_m, ctx_queries=False)
        om_ctx = None if last else _mla_attn(q_m, k_m, v_m, ctx_queries=True)

        xs, hp, logits_t = _merge(
            xs, p3, og, on, om, om_ctx, b_gate[l].reshape(1, -1),
            w_branch_gla[l].astype(BF16), w_branch_na[l].astype(BF16), w_branch_mla[l].astype(BF16),
            w_out[l].astype(BF16), mod, norm2_g[l].reshape(1, d), router_w[l].T,
            jnp.broadcast_to(router_b[l][:, None], (N_EXPERTS, LANE)), skip_ctx=last)

        t_tok = hp.shape[0]
        idx, wts, rank, counts = _route(logits_t)
        n_tiles = (TOP_K * t_tok) // MOE_G + N_EXPERTS
        dest, te, tv, pad_start, pad_len = _moe_plan(idx, rank, counts, n_tiles)
        xg = _dispatch(hp, dest, pad_start, pad_len, n_tiles * MOE_G)
        yg = _ffn(l, te, tv, xg, moe_w1, moe_b1, moe_w2, moe_b2)
        if last:
            (xs,) = _combine(xs, yg, dest, wts, mod, final_norm_g, mod, ctx_first=False, final_norm=True)
        else:
            xs, h = _combine(xs, yg, dest, wts, mod, norm1_g[l + 1], mod_all[l + 1].reshape(16 * 6, 1, d),
                             ctx_first=True, final_norm=False)
    return xs
```

```python
import functools

import numpy as np
import jax
import jax.numpy as jnp
from jax import lax
from jax.experimental import pallas as pl
from jax.experimental.pallas import tpu as pltpu

F32 = jnp.float32
BF16 = jnp.bfloat16

D_MODEL = 2048
DEPTH = 2
GRID_W = 64
CTX_LEN = 256
EPS = 1e-6
ROPE_BASE = 10000.0
NEG_INF = -1e30

GLA_HEADS = 4
GLA_DK = 64
GLA_DV = 128
GLA_GATE_RANK = 16
GLA_TAU = 16.0
GLA_CHUNK = 64
NA_HEADS = 8
NA_HEAD_DIM = 64
NA_KH = 8
NA_KW = 16
NA_SCALE = NA_HEAD_DIM ** -0.5
MLA_HEADS = 8
MLA_Q_RANK = 512
MLA_KV_RANK = 512
MLA_NOPE = 128
MLA_ROPE = 64
MLA_V = 128
MLA_SCALE = (MLA_NOPE + MLA_ROPE) ** -0.5
N_BRANCH = 3
N_EXPERTS = 32
TOP_K = 4
EXPERT_FF = D_MODEL
SWIGLU_LIMIT = 7.0
SWIGLU_ALPHA = 1.702

GLA_QK_W = GLA_HEADS * GLA_DK
GLA_V_W = GLA_HEADS * GLA_DV
NA_W = NA_HEADS * NA_HEAD_DIM
MLA_V_W = MLA_HEADS * MLA_V

LANE = 128
SUBLANE = 8
ROW_TILE = 256
VMEM_LIMIT = 56 * 1024 * 1024
FFN_VMEM_LIMIT = 62 * 1024 * 1024

COL_GATE = 0
COL_GQ = COL_GATE + N_BRANCH * D_MODEL
COL_GK = COL_GQ + GLA_QK_W
COL_GV = COL_GK + GLA_QK_W
COL_GR = COL_GV + GLA_V_W
COL_NQ = COL_GR + GLA_V_W
COL_NK = COL_NQ + NA_W
COL_NV = COL_NK + NA_W
COL_MCQ = COL_NV + NA_W
COL_MCKV = COL_MCQ + MLA_Q_RANK
COL_KR = COL_MCKV + MLA_KV_RANK
COL_KRP = COL_KR + LANE
COL_GA = COL_KRP + LANE
PROJ_TN = 1536
PROJ_W = 7 * PROJ_TN
assert COL_GA + LANE <= PROJ_W

MOE_G = 1024
MOE_SB = 256
MOE_CHAIN = 1024
MOE_FC = 512
DISPATCH_TT = 1024
COMBINE_TT = 256
ROUTE_TT = 512
MLA_TQ = 2048
MLA_CHAIN = 256


def _cparams(sem, vmem_limit=VMEM_LIMIT):
    return pltpu.CompilerParams(dimension_semantics=sem, vmem_limit_bytes=vmem_limit)


def _nt(a, b):
    return lax.dot_general(a, b, (((1,), (1,)), ((), ())), preferred_element_type=F32)


def _tn(a, b):
    return lax.dot_general(a, b, (((0,), (0,)), ((), ())), preferred_element_type=F32)


def _mm(a, b):
    return jnp.dot(a, b, preferred_element_type=F32)


def _sigmoid(x):
    return 1.0 / (1.0 + jnp.exp(-x))


def _split_hi_lo(x):
    hi = x.astype(BF16)
    lo = (x - hi.astype(F32)).astype(BF16)
    return hi, lo


def _ada_kernel(c_ref, w_ref, b_ref, o_ref):
    c = c_ref[...]
    s = (c * _sigmoid(c)).astype(BF16)
    o_ref[0] = _mm(s, w_ref[0].astype(BF16)) + b_ref[0]


def _ada_mod(cc, ada_w, ada_b):
    nl, d, n6 = ada_w.shape
    tn = 1024
    return pl.pallas_call(
        _ada_kernel,
        grid=(nl, n6 // tn),
        in_specs=[pl.BlockSpec((16, d), lambda l, j: (0, 0)),
                  pl.BlockSpec((1, d, tn), lambda l, j: (l, 0, j)),
                  pl.BlockSpec((1, 1, tn), lambda l, j: (l, 0, j))],
        out_specs=pl.BlockSpec((1, 16, tn), lambda l, j: (l, 0, j)),
        out_shape=jax.ShapeDtypeStruct((nl, 16, n6), F32),
        compiler_params=_cparams(("arbitrary", "arbitrary")),
        name="ada_mod",
    )(cc, ada_w, ada_b.reshape(nl, 1, n6))


def _mod_spec(part, ctx_first):
    def imap(b, j):
        row = jnp.where(j == 0, 8, b) if ctx_first else b
        return (row * 6 + part, 0, 0)
    return pl.BlockSpec((1, 1, D_MODEL), imap)


def _norm_mod_kernel(ctx_ref, x_ref, g_ref, sc_ref, sh_ref, xs_ref, o_ref):
    x = jnp.where(pl.program_id(1) == 0, ctx_ref[0], x_ref[0])
    xs_ref[0] = x
    ms = jnp.mean(x * x, axis=-1, keepdims=True)
    y = x * lax.rsqrt(ms + EPS) * g_ref[...]
    o_ref[0] = (y * (1.0 + sc_ref[0]) + sh_ref[0]).astype(o_ref.dtype)


def _norm_mod(ctx, x, g, mod):
    b, n, d = x.shape
    rb = ctx.shape[1] + n
    assert ctx.shape[1] == ROW_TILE
    row_spec = pl.BlockSpec((1, ROW_TILE, d), lambda i, j: (i, j, 0))
    return pl.pallas_call(
        _norm_mod_kernel,
        grid=(b, rb // ROW_TILE),
        in_specs=[pl.BlockSpec((1, ROW_TILE, d), lambda i, j: (i, 0, 0)),
                  pl.BlockSpec((1, ROW_TILE, d), lambda i, j: (i, jnp.maximum(j - 1, 0), 0)),
                  pl.BlockSpec((1, d), lambda i, j: (0, 0)),
                  _mod_spec(1, True), _mod_spec(0, True)],
        out_specs=[row_spec, row_spec],
        out_shape=[jax.ShapeDtypeStruct((b, rb, d), F32), jax.ShapeDtypeStruct((b, rb, d), BF16)],
        compiler_params=_cparams(("arbitrary", "arbitrary")),
        name="norm_mod",
    )(ctx, x, g.reshape(1, d), mod, mod)


def _matmul_kernel(x_ref, w_ref, o_ref):
    o_ref[...] = _mm(x_ref[...], w_ref[...]).astype(o_ref.dtype)


def _matmul(x, w, tm, tn, out_dtype):
    m, k = x.shape
    n = w.shape[1]
    return pl.pallas_call(
        _matmul_kernel,
        grid=(n // tn, m // tm),
        in_specs=[pl.BlockSpec((tm, k), lambda j, i: (i, 0)),
                  pl.BlockSpec((k, tn), lambda j, i: (0, j))],
        out_specs=pl.BlockSpec((tm, tn), lambda j, i: (i, j)),
        out_shape=jax.ShapeDtypeStruct((m, n), out_dtype),
        compiler_params=_cparams(("arbitrary", "arbitrary")),
        name="in_proj",
    )(x, w)


def _gla_kernel(q_ref, k_ref, v_ref, r_ref, ab_ref, waf_ref, wab_ref, ba_ref, g_ref, o_ref,
                laf_ref, lab_ref, of_ref, ob_ref, st_ref, *, n_ctx_chunks, n_chunks):
    c = GLA_CHUNK
    ab = ab_ref[0]
    zf = _mm(ab, waf_ref[...]) + ba_ref[0:1, :]
    zb = _mm(ab, wab_ref[...]) + ba_ref[1:2, :]
    laf_ref[...] = (jnp.minimum(zf, 0.0) - jnp.log(1.0 + jnp.exp(-jnp.abs(zf)))) * (1.0 / GLA_TAU)
    lab_ref[...] = (jnp.minimum(zb, 0.0) - jnp.log(1.0 + jnp.exp(-jnp.abs(zb)))) * (1.0 / GLA_TAU)
    st_ref[...] = jnp.zeros_like(st_ref)

    ri = lax.broadcasted_iota(jnp.int32, (c, c), 0)
    ci = lax.broadcasted_iota(jnp.int32, (c, c), 1)
    tri = [(ri >= ci), (ri <= ci)]
    tri_bf = [t.astype(F32).astype(BF16) for t in tri]
    tri4 = [jnp.concatenate([t] * GLA_HEADS, axis=0) for t in tri]
    lane_q = lax.broadcasted_iota(jnp.int32, (c, GLA_QK_W), 1) // GLA_DK
    lane_v = lax.broadcasted_iota(jnp.int32, (c, GLA_V_W), 1) // GLA_DV
    st_row_h = lax.broadcasted_iota(jnp.int32, (GLA_V_W, GLA_QK_W), 0) // GLA_DV
    st_col_h = lax.broadcasted_iota(jnp.int32, (GLA_V_W, GLA_QK_W), 1) // GLA_DK
    st_mask = st_row_h == st_col_h

    def one_dir(d, chunk, la_ref, out_ref):
        r0 = pl.multiple_of(chunk * c, c)
        la = la_ref[pl.ds(r0, c), :]
        hi, lo = _split_hi_lo(la)
        bc = _mm(tri_bf[d], hi) + _mm(tri_bf[d], lo)
        btot = bc[c - 1:c, :] if d == 0 else bc[0:1, :]
        q = q_ref[0, pl.ds(r0, c), :].astype(F32) * (GLA_DK ** -0.5)
        k = k_ref[0, pl.ds(r0, c), :].astype(F32)
        v = v_ref[0, pl.ds(r0, c), :]
        qd = (q * jnp.exp(bc)).astype(BF16)
        ki = (k * jnp.exp(-bc)).astype(BF16)
        kd = (k * jnp.exp(btot - bc)).astype(BF16)
        zero = jnp.zeros_like(qd)
        qs = jnp.concatenate([jnp.where(lane_q == h, qd, zero) for h in range(GLA_HEADS)], axis=0)
        att = _nt(qs, ki)
        att = jnp.where(tri4[d], att, 0.0).astype(BF16)
        rr = _mm(att, v)
        o = _nt(qd, st_ref[d].astype(BF16))
        for h in range(GLA_HEADS):
            o = o + jnp.where(lane_v == h, rr[h * c:(h + 1) * c, :], 0.0)
        out_ref[pl.ds(r0, c), :] = o
        upd = _tn(v, kd)
        st_ref[d] = jnp.where(st_mask, st_ref[d] * jnp.exp(btot) + upd, 0.0)

    def step(i, carry):
        one_dir(0, i, laf_ref, of_ref)
        cb = jnp.where(i < n_ctx_chunks, n_ctx_chunks - 1 - i, n_chunks + n_ctx_chunks - 1 - i)
        one_dir(1, cb, lab_ref, ob_ref)
        return carry

    lax.fori_loop(0, n_chunks, step, 0, unroll=4)

    def epilogue(j, carry):
        r0 = pl.multiple_of(j * ROW_TILE, ROW_TILE)
        o = of_ref[pl.ds(r0, ROW_TILE), :] + ob_ref[pl.ds(r0, ROW_TILE), :]
        r = r_ref[0, pl.ds(r0, ROW_TILE), :].astype(F32)
        gate = r * _sigmoid(r)
        for h in range(GLA_HEADS):
            sl = slice(h * GLA_DV, (h + 1) * GLA_DV)
            oh = o[:, sl]
            ms = jnp.mean(oh * oh, axis=-1, keepdims=True)
            y = oh * lax.rsqrt(ms + EPS) * g_ref[:, sl]
            o_ref[0, pl.ds(r0, ROW_TILE), sl] = (y * gate[:, sl]).astype(o_ref.dtype)
        return carry

    lax.fori_loop(0, (n_chunks * c) // ROW_TILE, epilogue, 0)


def _gla(p3, waf, wab, ba, g):
    b, rb, _ = p3.shape
    n_chunks = rb // GLA_CHUNK
    kern = functools.partial(_gla_kernel, n_ctx_chunks=CTX_LEN // GLA_CHUNK, n_chunks=n_chunks)

    def col(width, off):
        return pl.BlockSpec((1, rb, width), lambda i: (i, 0, off // width))

    return pl.pallas_call(
        kern,
        grid=(b,),
        in_specs=[col(GLA_QK_W, COL_GQ), col(GLA_QK_W, COL_GK), col(GLA_V_W, COL_GV), col(GLA_V_W, COL_GR),
                  col(LANE, COL_GA),
                  pl.BlockSpec((LANE, GLA_QK_W), lambda i: (0, 0)),
                  pl.BlockSpec((LANE, GLA_QK_W), lambda i: (0, 0)),
                  pl.BlockSpec((2, GLA_QK_W), lambda i: (0, 0)),
                  pl.BlockSpec((1, GLA_V_W), lambda i: (0, 0))],
        out_specs=pl.BlockSpec((1, rb, GLA_V_W), lambda i: (i, 0, 0)),
        out_shape=jax.ShapeDtypeStruct((b, rb, GLA_V_W), BF16),
        scratch_shapes=[pltpu.VMEM((rb, GLA_QK_W), F32), pltpu.VMEM((rb, GLA_QK_W), F32),
                        pltpu.VMEM((rb, GLA_V_W), F32), pltpu.VMEM((rb, GLA_V_W), F32),
                        pltpu.VMEM((2, GLA_V_W, GLA_QK_W), F32)],
        compiler_params=_cparams(("arbitrary",)),
        name="gla",
    )(p3, p3, p3, p3, p3, waf, wab, ba, g)


def _softmax_pv(parts, ones_block=False):
    m = parts[0][0].max(axis=-1, keepdims=True)
    for s, _ in parts[1:]:
        m = jnp.maximum(m, s.max(axis=-1, keepdims=True))
    acc = None
    den = None
    for s, v in parts:
        if ones_block:
            pv = _mm(jnp.exp((s - m).astype(BF16)), v)
        else:
            e = jnp.exp(s - m)
            l = e.sum(axis=-1, keepdims=True)
            den = l if den is None else den + l
            pv = _mm(e.astype(BF16), v)
        acc = pv if acc is None else acc + pv
    if ones_block:
        dv = acc.shape[1] // 2
        return acc[:, :dv] / acc[:, dv:]
    return acc / den


assert float(np.log2(NA_SCALE)).is_integer()
NA_QR = 4
NA_KR = NA_QR + NA_KH


def _na_block_rows(n_rows):
    r0s = (0, 2 * NA_QR, n_rows - NA_QR)
    return [(r0, int(np.clip(r0 - NA_KH // 2, 0, n_rows - NA_KR))) for r0 in r0s]


def _na_kernel(q_ref, k_ref, v_ref, bias_ref, o_ref, *, n_rows):
    nq = NA_QR * GRID_W
    nk = NA_KR * GRID_W
    n_blocks = n_rows // NA_QR
    lane = lax.broadcasted_iota(jnp.int32, (nq, LANE), 1)
    head_mask = [lane < NA_HEAD_DIM, lane >= NA_HEAD_DIM]
    kc = k_ref[0, 0:CTX_LEN, :]
    vc = v_ref[0, 0:CTX_LEN, :]
    vc1 = jnp.concatenate([vc, jnp.ones_like(vc)], axis=1)

    def block(bi, carry):
        r0 = bi * NA_QR
        ks = jnp.clip(r0 - NA_KH // 2, 0, n_rows - NA_KR)
        var = jnp.where(bi == 0, 0, jnp.where(bi == n_blocks - 1, 2, 1))
        q0 = pl.multiple_of(CTX_LEN + r0 * GRID_W, GRID_W)
        k0 = pl.multiple_of(CTX_LEN + ks * GRID_W, GRID_W)
        q = q_ref[0, pl.ds(q0, nq), :] * NA_SCALE
        kl = k_ref[0, pl.ds(k0, nk), :]
        vl = v_ref[0, pl.ds(k0, nk), :]
        vl1 = jnp.concatenate([vl, jnp.ones_like(vl)], axis=1)
        outs = []
        for hh in range(2):
            qm = jnp.where(head_mask[hh], q, jnp.zeros_like(q))
            s_loc = _nt(qm, kl) + bias_ref[hh, var]
            s_ctx = _nt(qm, kc)
            outs.append(_softmax_pv([(s_loc, vl1), (s_ctx, vc1)], ones_block=True))
        o_ref[0, pl.ds(q0, nq), :] = jnp.where(head_mask[0], outs[0], outs[1]).astype(o_ref.dtype)
        return carry

    lax.fori_loop(0, n_blocks, block, 0, unroll=2)

    qc = q_ref[0, 0:CTX_LEN, :]
    lane_c = lax.broadcasted_iota(jnp.int32, (CTX_LEN, LANE), 1)
    outs = []
    for hh in range(2):
        msk = (lane_c < NA_HEAD_DIM) if hh == 0 else (lane_c >= NA_HEAD_DIM)
        qm = jnp.where(msk, qc, jnp.zeros_like(qc))
        outs.append(_softmax_pv([(_nt(qm, kc) * NA_SCALE, vc)]))
    o_ref[0, 0:CTX_LEN, :] = jnp.where(lane_c < NA_HEAD_DIM, outs[0], outs[1]).astype(o_ref.dtype)


def _na_bias_table(rpb, n_rows):
    qcol = np.arange(GRID_W)[:, None]
    kcol = np.arange(GRID_W)[None, :]
    wstart = np.clip(qcol - NA_KW // 2, 0, GRID_W - NA_KW)
    in_win = (kcol >= wstart) & (kcol < wstart + NA_KW)
    dc = kcol - qcol + NA_KW - 1
    col_hot = ((dc[:, :, None] == np.arange(2 * NA_KW - 1)) & in_win[:, :, None]).astype(np.float32)
    geo = _na_block_rows(n_rows)
    row_hot = np.zeros((len(geo), NA_QR, NA_KR, 2 * NA_KH - 1), np.float32)
    for v, (r0, ks) in enumerate(geo):
        for a in range(NA_QR):
            rs = int(np.clip(r0 + a - NA_KH // 2, 0, n_rows - NA_KH))
            for i in range(NA_KR):
                if rs <= ks + i < rs + NA_KH:
                    row_hot[v, a, i, ks + i - (r0 + a) + NA_KH - 1] = 1.0
    t = jnp.einsum('hrc,vair,qkc->hvaqik', rpb.astype(F32), jnp.asarray(row_hot), jnp.asarray(col_hot),
                   precision=lax.Precision.HIGHEST)
    inside = (row_hot.sum(-1) > 0)[:, :, None, :, None] & in_win[None, None, :, None, :]
    t = jnp.where(inside[None], t, NEG_INF)
    return t.reshape(NA_HEADS, len(geo), NA_QR * GRID_W, NA_KR * GRID_W)


def _na(p3, bias):
    b, rb, _ = p3.shape
    n_rows = (rb - CTX_LEN) // GRID_W
    kern = functools.partial(_na_kernel, n_rows=n_rows)

    def col(off):
        return pl.BlockSpec((1, rb, LANE), lambda p, i: (i, 0, off // LANE + p))

    return pl.pallas_call(
        kern,
        grid=(NA_HEADS // 2, b),
        in_specs=[col(COL_NQ), col(COL_NK), col(COL_NV),
                  pl.BlockSpec((2,) + bias.shape[1:], lambda p, i: (p, 0, 0, 0))],
        out_specs=pl.BlockSpec((1, rb, LANE), lambda p, i: (i, 0, p)),
        out_shape=jax.ShapeDtypeStruct((b, rb, NA_W), BF16),
        compiler_params=_cparams(("arbitrary", "arbitrary")),
        name="na",
    )(p3, p3, p3, bias)


def _mla_proj_kernel(cq_ref, ckv_ref, kr_ref, krp_ref, cos_ref, sin_ref, gq_ref, gkv_ref,
                     wq_ref, wq2_ref, wkv_ref, q_ref, k_ref, v_ref):
    def norm(x, g):
        x = x.astype(F32)
        ms = jnp.mean(x * x, axis=-1, keepdims=True)
        return (x * lax.rsqrt(ms + EPS) * g).astype(BF16)

    cos = cos_ref[...]
    sin = sin_ref[...]
    nq = norm(cq_ref[0], gq_ref[...])
    yq = _mm(nq, wq_ref[...])
    yq2 = _mm(nq, wq2_ref[...])
    nkv = norm(ckv_ref[0], gkv_ref[...])
    ykv = _mm(nkv, wkv_ref[...])
    k_rot = (kr_ref[0].astype(F32) * cos + krp_ref[0].astype(F32) * sin).astype(BF16)
    for h in range(MLA_HEADS):
        a = 2 * h * LANE
        q_ref[0, :, a:a + LANE] = (yq[:, a:a + LANE] * MLA_SCALE).astype(BF16)
        rot = yq[:, a + LANE:a + 2 * LANE] * cos + yq2[:, h * LANE:(h + 1) * LANE] * sin
        q_ref[0, :, a + LANE:a + 2 * LANE] = (rot * MLA_SCALE).astype(BF16)
        k_ref[0, :, a:a + LANE] = ykv[:, h * LANE:(h + 1) * LANE].astype(BF16)
        k_ref[0, :, a + LANE:a + 2 * LANE] = k_rot
        v0 = MLA_HEADS * MLA_NOPE + h * MLA_V
        v_ref[0, :, a:a + LANE] = ykv[:, v0:v0 + MLA_V].astype(BF16)
        v_ref[0, :, a + LANE:a + 2 * LANE] = jnp.ones((ykv.shape[0], LANE), BF16)


def _mla_proj(p3, cos_t, sin_t, gq, gkv, wq, wq2, wkv):
    b, rb, _ = p3.shape
    tm = ROW_TILE

    def col(width, off):
        return pl.BlockSpec((1, tm, width), lambda i, j: (i, j, off // width))

    def full(a):
        return pl.BlockSpec(a.shape, lambda i, j: (0, 0))

    hw = MLA_HEADS * 2 * LANE
    nj = rb // tm
    return pl.pallas_call(
        _mla_proj_kernel,
        grid=(b, nj),
        in_specs=[col(MLA_Q_RANK, COL_MCQ), col(MLA_KV_RANK, COL_MCKV), col(LANE, COL_KR), col(LANE, COL_KRP),
                  pl.BlockSpec((tm, LANE), lambda i, j: (j, 0)), pl.BlockSpec((tm, LANE), lambda i, j: (j, 0)),
                  full(gq), full(gkv), full(wq), full(wq2), full(wkv)],
        out_specs=[pl.BlockSpec((1, tm, hw), lambda i, j: (i, (j + nj - 1) % nj, 0)),
                   pl.BlockSpec((1, tm, hw), lambda i, j: (i, (j + nj - 1) % nj, 0)),
                   pl.BlockSpec((1, tm, hw), lambda i, j: (i, (j + nj - 1) % nj, 0))],
        out_shape=[jax.ShapeDtypeStruct((b, rb, hw), BF16), jax.ShapeDtypeStruct((b, rb, hw), BF16),
                   jax.ShapeDtypeStruct((b, rb, hw), BF16)],
        compiler_params=_cparams(("arbitrary", "arbitrary")),
        name="mla_proj",
    )(p3, p3, p3, p3, cos_t, sin_t, gq, gkv, wq, wq2, wkv)


def _mla_attn_kernel(q_ref, k_ref, v_ref, o_ref):
    k = k_ref[0]
    v = v_ref[0]
    tq = q_ref.shape[1]
    sub = min(tq, MLA_CHAIN)
    for r0 in range(0, tq, sub):
        s = _nt(q_ref[0, r0:r0 + sub, :], k)
        e = jnp.exp((s - s.max(axis=-1, keepdims=True)).astype(BF16))
        pv = _mm(e, v)
        o_ref[0, r0:r0 + sub, :] = (pv[:, :MLA_V] / pv[:, MLA_V:]).astype(o_ref.dtype)


def _mla_attn(q, k, v, ctx_queries):
    b, rb, _ = q.shape
    n_lat = rb - CTX_LEN
    if ctx_queries:
        tq, nq, n_keys = CTX_LEN, 1, CTX_LEN
        q_off = kv_blk = n_lat // CTX_LEN
    else:
        tq = MLA_TQ if n_lat % MLA_TQ == 0 else ROW_TILE
        nq, n_keys, q_off, kv_blk = n_lat // tq, rb, 0, 0
    return pl.pallas_call(
        _mla_attn_kernel,
        grid=(b, MLA_HEADS, nq),
        in_specs=[pl.BlockSpec((1, tq, 2 * LANE), lambda i, h, j: (i, j + q_off, h)),
                  pl.BlockSpec((1, n_keys, 2 * LANE), lambda i, h, j: (i, kv_blk, h)),
                  pl.BlockSpec((1, n_keys, 2 * LANE), lambda i, h, j: (i, kv_blk, h))],
        out_specs=pl.BlockSpec((1, tq, MLA_V), lambda i, h, j: (i, j, h)),
        out_shape=jax.ShapeDtypeStruct((b, nq * tq, MLA_V_W), BF16),
        compiler_params=_cparams(("arbitrary", "arbitrary", "arbitrary")),
        name="mla_attn_ctx" if ctx_queries else "mla_attn",
    )(q, k, v)


def _merge_kernel(x_ref, oa_ref, ob_ref, oc_ref, occ_ref, ga_ref, gb_ref, gc_ref, bg_ref,
                  wa_ref, wb_ref, wc_ref, wo_ref, gt1_ref, sc2_ref, sh2_ref, g2_ref, rw_ref, rb_ref,
                  xo_ref, h_ref, lg_ref, *, ctx_tile):
    d = D_MODEL

    wh, wl = _split_hi_lo(rw_ref[...])
    tm = x_ref.shape[1]
    sub = tm
    for r0 in range(0, tm, sub):
        rows = slice(r0, r0 + sub)

        def gated(g_ref, k, br):
            z = g_ref[0, rows, :] + bg_ref[:, k * d:(k + 1) * d].astype(BF16)
            return _sigmoid(z) * br.astype(BF16)

        oc = oc_ref[0, rows, :]
        if ctx_tile:
            oc = jnp.where(pl.program_id(1) == 0, occ_ref[0, rows, :], oc)
        m = gated(ga_ref, 0, _mm(oa_ref[0, rows, :], wa_ref[...]))
        m = m + gated(gb_ref, 1, _mm(ob_ref[0, rows, :], wb_ref[...]))
        m = m + gated(gc_ref, 2, _mm(oc, wc_ref[...]))
        y = _mm(m, wo_ref[...])
        x = x_ref[0, rows, :] + gt1_ref[0] * y
        xo_ref[0, rows, :] = x
        ms = jnp.mean(x * x, axis=-1, keepdims=True)
        h = x * lax.rsqrt(ms + EPS) * g2_ref[...]
        h = h * (1.0 + sc2_ref[0]) + sh2_ref[0]
        h_ref[rows, :] = h.astype(h_ref.dtype)
        hh, hl = _split_hi_lo(h)
        lg_ref[:, rows] = _nt(wh, hh) + _nt(wh, hl) + _nt(wl, hh) + rb_ref[:, 0:1]


def _merge(x, p3, og, on, om, om_ctx, bg, wa, wb, wc, wo, mod, g2, rwt, rbias, skip_ctx):
    b, rb, d = x.shape
    tm = ROW_TILE
    jo = 1 if skip_ctx else 0
    nj = rb // tm - jo
    rows_out = nj * tm

    def rows(width, cblk=0):
        return pl.BlockSpec((1, tm, width), lambda i, j: (i, j + jo, cblk))

    om_spec = pl.BlockSpec((1, tm, MLA_V_W), lambda i, j: (i, jnp.maximum(j + jo - 1, 0), 0))
    omc_spec = pl.BlockSpec((1, tm, MLA_V_W), lambda i, j: (i, 0, 0))
    kern = functools.partial(_merge_kernel, ctx_tile=not skip_ctx)

    def full(a):
        return pl.BlockSpec(a.shape, lambda i, j: (0,) * a.ndim, pipeline_mode=pl.Buffered(1))

    def mod_spec(part):
        def imap(i, j):
            row = i if skip_ctx else jnp.where(j == 0, 8, i)
            return (row * 6 + part, 0, 0)
        return pl.BlockSpec((1, 1, d), imap)

    return pl.pallas_call(
        kern,
        grid=(b, nj),
        in_specs=[rows(d), rows(GLA_V_W), rows(NA_W), om_spec, omc_spec,
                  rows(d, 0), rows(d, 1), rows(d, 2), full(bg),
                  full(wa), full(wb), full(wc), full(wo),
                  mod_spec(2), mod_spec(4), mod_spec(3), full(g2), full(rwt), full(rbias)],
        out_specs=[pl.BlockSpec((1, tm, d), lambda i, j: (i, j, 0)),
                   pl.BlockSpec((tm, d), lambda i, j: (i * nj + j, 0)),
                   pl.BlockSpec((N_EXPERTS, tm), lambda i, j: (0, i * nj + j))],
        out_shape=[jax.ShapeDtypeStruct((b, rows_out, d), F32),
                   jax.ShapeDtypeStruct((b * rows_out, d), BF16),
                   jax.ShapeDtypeStruct((N_EXPERTS, b * rows_out), F32)],
        compiler_params=_cparams(("arbitrary", "arbitrary")),
        name="merge",
    )(x, og, on, om, om if skip_ctx else om_ctx, p3, p3, p3, bg, wa, wb, wc, wo, mod, mod, mod, g2, rwt, rbias)


def _route_kernel(l_ref, idx_ref, w_ref, rank_ref, cnt_ref, carry_ref):
    i = pl.program_id(0)
    tt = l_ref.shape[1]

    @pl.when(i == 0)
    def _():
        carry_ref[...] = jnp.zeros_like(carry_ref)

    l = l_ref[...]
    eio = lax.broadcasted_iota(jnp.int32, (N_EXPERTS, tt), 0)
    vals, idxs = [], []
    for _ in range(TOP_K):
        m = l.max(axis=0, keepdims=True)
        ik = jnp.min(jnp.where(l == m, eio, N_EXPERTS), axis=0, keepdims=True)
        vals.append(m)
        idxs.append(ik)
        l = jnp.where(eio == ik, -jnp.inf, l)
    es = [jnp.exp(v - vals[0]) for v in vals]
    den = es[0] + es[1] + es[2] + es[3]
    sel = jnp.zeros((N_EXPERTS, tt), F32)
    for ik in idxs:
        sel = sel + (eio == ik).astype(F32)
    si = lax.broadcasted_iota(jnp.int32, (tt, tt), 0)
    ti = lax.broadcasted_iota(jnp.int32, (tt, tt), 1)
    before = (si < ti).astype(F32).astype(BF16)
    rank_full = _mm(sel.astype(BF16), before) + carry_ref[:, 0:1]
    for k in range(TOP_K):
        idx_ref[k:k + 1, :] = idxs[k]
        w_ref[k:k + 1, :] = es[k] / den
        rk = jnp.sum(jnp.where(eio == idxs[k], rank_full, 0.0), axis=0, keepdims=True)
        rank_ref[k:k + 1, :] = rk.astype(jnp.int32)
    carry_ref[...] = carry_ref[...] + jnp.sum(sel, axis=1, keepdims=True)
    cnt_ref[...] = carry_ref[...]


def _route(logits_t):
    ne, t = logits_t.shape
    tt = ROUTE_TT if t % ROUTE_TT == 0 else ROW_TILE
    spec4 = pl.BlockSpec((TOP_K, tt), lambda i: (0, i))
    return pl.pallas_call(
        _route_kernel,
        grid=(t // tt,),
        in_specs=[pl.BlockSpec((ne, tt), lambda i: (0, i))],
        out_specs=[spec4, spec4, spec4, pl.BlockSpec((ne, LANE), lambda i: (0, 0))],
        out_shape=[jax.ShapeDtypeStruct((TOP_K, t), jnp.int32), jax.ShapeDtypeStruct((TOP_K, t), F32),
                   jax.ShapeDtypeStruct((TOP_K, t), jnp.int32), jax.ShapeDtypeStruct((ne, LANE), F32)],
        scratch_shapes=[pltpu.VMEM((ne, LANE), F32)],
        compiler_params=_cparams(("arbitrary",)),
        name="route",
    )(logits_t)


def _pad_fill(ps_ref, pn_ref, zero_ref, xg_ref, sem, wait):
    def copy(pos, rows):
        cp = pltpu.make_async_copy(zero_ref.at[pl.ds(0, rows), :], xg_ref.at[pl.ds(pos, rows), :], sem)
        cp.wait() if wait else cp.start()

    def per_expert(e, carry):
        pos = ps_ref[e]
        head = (-pos) & (SUBLANE - 1)
        for r in range(SUBLANE - 1):
            @pl.when(r < head)
            def _(r=r):
                copy(pos + r, 1)

        pos = pos + head
        n = pn_ref[e] - head
        bit = MOE_G // 2
        while bit >= SUBLANE:
            on = (n & bit) != 0

            @pl.when(on)
            def _(pos=pos, bit=bit):
                copy(pl.multiple_of(pos, SUBLANE), bit)

            pos = pos + jnp.where(on, bit, 0)
            bit //= 2
        return carry

    lax.fori_loop(0, N_EXPERTS, per_expert, 0)

    zr = zero_ref.shape[0]

    def tail(i, carry):
        pos = pl.multiple_of(ps_ref[N_EXPERTS] + i * zr, zr)
        cp = pltpu.make_async_copy(zero_ref, xg_ref.at[pl.ds(pos, zr), :], sem)
        cp.wait() if wait else cp.start()
        return carry

    lax.fori_loop(0, pn_ref[N_EXPERTS], tail, 0)


def _dispatch_kernel(ps_ref, pn_ref, dest_ref, h_ref, xg_ref, zero_ref, hf_ref, sem, zsem):
    tt = h_ref.shape[0]

    @pl.when(pl.program_id(0) == 0)
    def _():
        zero_ref[...] = jnp.zeros_like(zero_ref)
        _pad_fill(ps_ref, pn_ref, zero_ref, xg_ref, zsem, wait=False)

    hf_ref[...] = h_ref[...].astype(hf_ref.dtype)

    def issue(t, carry):
        for k in range(TOP_K):
            pltpu.make_async_copy(hf_ref.at[pl.ds(t, 1), :], xg_ref.at[pl.ds(dest_ref[0, k, t], 1), :],
                                  sem).start(priority=k % 2)
        return carry

    lax.fori_loop(0, tt, issue, 0, unroll=4)
    for k in range(TOP_K):
        pltpu.make_async_copy(hf_ref, xg_ref.at[pl.ds(0, tt), :], sem).wait()

    @pl.when(pl.program_id(0) == 0)
    def _():
        _pad_fill(ps_ref, pn_ref, zero_ref, xg_ref, zsem, wait=True)


def _dispatch(hp, dest, pad_start, pad_len, n_slots):
    t, w = hp.shape
    tt = DISPATCH_TT if t % DISPATCH_TT == 0 else ROW_TILE
    dest3 = dest.reshape(TOP_K, t // tt, tt).transpose(1, 0, 2)
    grid_spec = pltpu.PrefetchScalarGridSpec(
        num_scalar_prefetch=2,
        grid=(t // tt,),
        in_specs=[pl.BlockSpec((1, TOP_K, tt), lambda i, ps, pn: (i, 0, 0), memory_space=pltpu.SMEM),
                  pl.BlockSpec((tt, w), lambda i, ps, pn: (i, 0))],
        out_specs=pl.BlockSpec(memory_space=pl.ANY),
        scratch_shapes=[pltpu.VMEM((MOE_G // 2, w), F32), pltpu.VMEM((tt, w), F32),
                        pltpu.SemaphoreType.DMA(()), pltpu.SemaphoreType.DMA(())],
    )
    return pl.pallas_call(
        _dispatch_kernel,
        grid_spec=grid_spec,
        out_shape=jax.ShapeDtypeStruct((n_slots, w), F32),
        compiler_params=_cparams(("arbitrary",)),
        name="dispatch",
    )(pad_start, pad_len, dest3, hp)


def _ffn_kernel(te_ref, tv_ref, x_ref, w1g_ref, w1l_ref, b1g_ref, b1l_ref, w2_ref, b2_ref, y_ref,
                xb_ref, *, n_fc, n_tiles):
    i = pl.program_id(0)
    j = pl.program_id(1)
    valid = tv_ref[i]
    g = y_ref.shape[0]
    chunk = x_ref.shape[0]

    @pl.when(i < n_tiles)
    def _():
        xb_ref[i % 2, pl.ds(pl.multiple_of(j * chunk, chunk), chunk), :] = x_ref[...].astype(BF16)

    cur = (i + 1) % 2

    @pl.when((valid > 0) & (j == 0))
    def _():
        y_ref[...] = jnp.broadcast_to(b2_ref[0, 0], y_ref.shape)

    for nr in range(MOE_SB, g + 1, MOE_SB):
        @pl.when((valid > nr - MOE_SB) & (valid <= nr))
        def _(nr=nr):
            wg = w1g_ref[0, 0].astype(BF16)
            wl = w1l_ref[0, 0].astype(BF16)
            w2 = w2_ref[0, 0].astype(BF16)
            for r0 in range(0, nr, MOE_CHAIN):
                r1 = min(r0 + MOE_CHAIN, nr)
                x = xb_ref[cur, r0:r1, :]
                ug = _mm(x, wg) + b1g_ref[0, 0]
                ul = _mm(x, wl) + b1l_ref[0, 0]
                xg = jnp.minimum(ug, SWIGLU_LIMIT)
                xl = jnp.clip(ul, -SWIGLU_LIMIT, SWIGLU_LIMIT)
                act = xg * _sigmoid(SWIGLU_ALPHA * xg) * (xl + 1.0)
                y_ref[r0:r1, :] = y_ref[r0:r1, :] + _mm(act.astype(BF16), w2)

    @pl.when((j == n_fc - 1) & (valid == 0))
    def _():
        y_ref[...] = jnp.zeros_like(y_ref)


def _ffn(layer, tile_expert, tile_valid, xg, w1, b1, w2, b2):
    n_slots, d = xg.shape
    nl, ne, _, ff2 = w1.shape
    ff = ff2 // 2
    n_fc = ff // MOE_FC
    n_tiles = n_slots // MOE_G
    chunk = MOE_G // n_fc
    kern = functools.partial(_ffn_kernel, n_fc=n_fc, n_tiles=n_tiles)
    tile_expert = jnp.concatenate([tile_expert[:1], tile_expert])
    tile_valid = jnp.concatenate([jnp.zeros((1,), tile_valid.dtype), tile_valid])

    def jj(j, tv, i):
        return jnp.where(tv[i] > 0, j, n_fc - 1)

    grid_spec = pltpu.PrefetchScalarGridSpec(
        num_scalar_prefetch=2,
        grid=(n_tiles + 1, n_fc),
        in_specs=[pl.BlockSpec((chunk, d), lambda i, j, te, tv: (jnp.minimum(i, n_tiles - 1) * n_fc + j, 0)),
                  pl.BlockSpec((1, 1, d, MOE_FC), lambda i, j, te, tv: (layer, te[i], 0, jj(j, tv, i))),
                  pl.BlockSpec((1, 1, d, MOE_FC), lambda i, j, te, tv: (layer, te[i], 0, jj(j, tv, i) + n_fc)),
                  pl.BlockSpec((1, 1, 1, MOE_FC), lambda i, j, te, tv: (layer, te[i], 0, jj(j, tv, i))),
                  pl.BlockSpec((1, 1, 1, MOE_FC), lambda i, j, te, tv: (layer, te[i], 0, jj(j, tv, i) + n_fc)),
                  pl.BlockSpec((1, 1, MOE_FC, d), lambda i, j, te, tv: (layer, te[i], jj(j, tv, i), 0)),
                  pl.BlockSpec((1, 1, 1, d), lambda i, j, te, tv: (layer, te[i], 0, 0))],
        out_specs=pl.BlockSpec((MOE_G, d), lambda i, j, te, tv: (jnp.maximum(i - 1, 0), 0)),
        scratch_shapes=[pltpu.VMEM((2, MOE_G, d), BF16)],
    )
    return pl.pallas_call(
        kern,
        grid_spec=grid_spec,
        out_shape=jax.ShapeDtypeStruct((n_slots, d), F32),
        compiler_params=_cparams(("arbitrary", "arbitrary"), FFN_VMEM_LIMIT),
        name="moe_ffn",
    )(tile_expert, tile_valid, xg, w1, w1, b1.reshape(nl, ne, 1, ff2), b1.reshape(nl, ne, 1, ff2), w2,
      b2.reshape(nl, ne, 1, d))


def _combine_kernel(dest_ref, destn_ref, x_ref, wt_ref, gt2_ref, gn_ref, sc_ref, sh_ref, yg_ref, o_ref, *rest,
                    final_norm):
    h_ref = None if final_norm else rest[0]
    buf_ref, sem = rest[-2:]
    i = pl.program_id(0)
    n = pl.num_programs(0)
    tt = x_ref.shape[1]
    slot = i % 2

    def issue(d_ref, s):
        def body(t, carry):
            for k in range(TOP_K):
                pltpu.make_async_copy(yg_ref.at[pl.ds(d_ref[0, k, t], 1), :],
                                      buf_ref.at[s, k, pl.ds(t, 1), :], sem.at[s]).start(priority=k % 2)
            return carry
        lax.fori_loop(0, tt, body, 0, unroll=4)

    @pl.when(i == 0)
    def _():
        issue(dest_ref, 0)

    for s in range(2):
        @pl.when((i + 1 < n) & (slot == 1 - s))
        def _(s=s):
            issue(destn_ref, s)

    for k in range(TOP_K):
        pltpu.make_async_copy(yg_ref.at[pl.ds(0, tt), :], buf_ref.at[slot, k], sem.at[slot]).wait()

    wt = wt_ref[...]
    y = buf_ref[slot, 0] * wt[:, 0:1]
    for k in range(1, TOP_K):
        y = y + buf_ref[slot, k] * wt[:, k:k + 1]
    x = x_ref[0] + gt2_ref[0] * y
    ms = jnp.mean(x * x, axis=-1, keepdims=True)
    xn = x * lax.rsqrt(ms + EPS) * gn_ref[...]
    if final_norm:
        o_ref[0] = xn
    else:
        o_ref[0] = x
        h_ref[0] = (xn * (1.0 + sc_ref[0]) + sh_ref[0]).astype(h_ref.dtype)


def _combine(x, yg, dest, wts, mod, gn, mod_next, ctx_first, final_norm):
    b, rows, d = x.shape
    tt = COMBINE_TT
    nj = rows // tt
    t = b * rows
    nt = t // tt
    dest3 = dest.reshape(TOP_K, nt, tt).transpose(1, 0, 2)
    wt = wts.T
    kern = functools.partial(_combine_kernel, final_norm=final_norm)

    def mod_spec(part):
        def imap(i):
            bi = i // nj
            row = jnp.where((i % nj) * tt < CTX_LEN, 8, bi) if ctx_first else bi
            return (row * 6 + part, 0, 0)
        return pl.BlockSpec((1, 1, d), imap)

    row_spec = pl.BlockSpec((1, tt, d), lambda i: (i // nj, i % nj, 0))
    out_specs = [row_spec]
    out_shape = [jax.ShapeDtypeStruct((b, rows, d), F32)]
    if not final_norm:
        out_specs.append(row_spec)
        out_shape.append(jax.ShapeDtypeStruct((b, rows, d), BF16))

    return pl.pallas_call(
        kern,
        grid=(nt,),
        in_specs=[pl.BlockSpec((1, TOP_K, tt), lambda i: (i, 0, 0), memory_space=pltpu.SMEM),
                  pl.BlockSpec((1, TOP_K, tt), lambda i: (jnp.minimum(i + 1, nt - 1), 0, 0),
                               memory_space=pltpu.SMEM),
                  row_spec,
                  pl.BlockSpec((tt, TOP_K), lambda i: (i, 0)),
                  mod_spec(5),
                  pl.BlockSpec((1, d), lambda i: (0, 0)),
                  mod_spec(1), mod_spec(0),
                  pl.BlockSpec(memory_space=pl.ANY)],
        out_specs=out_specs,
        out_shape=out_shape,
        scratch_shapes=[pltpu.VMEM((2, TOP_K, tt, d), yg.dtype), pltpu.SemaphoreType.DMA((2,))],
        compiler_params=_cparams(("arbitrary",)),
        name="combine",
    )(dest3, dest3, x, wt, mod, gn.reshape(1, d), mod_next, mod_next, yg)


def _proj_weight(w_in):
    d = w_in.shape[0]
    splits = (GLA_QK_W, GLA_QK_W, GLA_V_W, GLA_V_W, GLA_GATE_RANK, GLA_GATE_RANK,
              NA_W, NA_W, NA_W, MLA_Q_RANK, MLA_KV_RANK, MLA_ROPE, N_BRANCH * D_MODEL)
    pts = np.cumsum((0,) + splits)
    (gq, gk, gv, gr, gaf, gab, nq, nk, nv, mcq, mckv, mkr, gate) = [w_in[:, pts[i]:pts[i + 1]] for i in range(13)]
    q16 = MLA_ROPE // 4
    mkrp = jnp.concatenate([mkr[:, q16:2 * q16], mkr[:, :q16], mkr[:, 3 * q16:], mkr[:, 2 * q16:3 * q16]], axis=1)
    z = lambda n: jnp.zeros((d, n), w_in.dtype)
    cols = [gate, gq, gk, gv, gr, nq, nk, nv, mcq, mckv,
            mkr, z(LANE - MLA_ROPE), mkrp, z(LANE - MLA_ROPE),
            gaf, gab, z(LANE - 2 * GLA_GATE_RANK)]
    w = jnp.concatenate(cols, axis=1)
    w = jnp.concatenate([w, z(PROJ_W - w.shape[1])], axis=1)
    return w.astype(BF16)


def _rope_tables(rb):
    n = rb - CTX_LEN
    t = np.arange(n)
    nf = MLA_ROPE // 4
    freqs = ROPE_BASE ** (-np.arange(nf, dtype=np.float64) / nf)
    cos = np.zeros((rb, LANE), np.float32)
    sin = np.zeros((rb, LANE), np.float32)
    cos[:CTX_LEN, :MLA_ROPE] = 1.0
    for a, pos in enumerate((t // GRID_W, t % GRID_W)):
        ang = (pos.astype(np.float32)[:, None] * freqs.astype(np.float32)[None, :]).astype(np.float32)
        c, s = np.cos(ang), np.sin(ang)
        base = a * 2 * nf
        cos[CTX_LEN:, base:base + nf] = c
        cos[CTX_LEN:, base + nf:base + 2 * nf] = c
        sin[CTX_LEN:, base:base + nf] = -s
        sin[CTX_LEN:, base + nf:base + 2 * nf] = s
    return jnp.asarray(cos), jnp.asarray(sin)


def _mla_weights(w_q_up, w_kv_up):
    r = w_q_up.shape[0]
    wq = w_q_up.reshape(r, MLA_HEADS, MLA_NOPE + MLA_ROPE)
    nope, rope = wq[..., :MLA_NOPE], wq[..., MLA_NOPE:]
    q16 = MLA_ROPE // 4
    ropep = jnp.concatenate([rope[..., q16:2 * q16], rope[..., :q16], rope[..., 3 * q16:], rope[..., 2 * q16:3 * q16]],
                            axis=-1)
    zpad = jnp.zeros((r, MLA_HEADS, LANE - MLA_ROPE), w_q_up.dtype)
    wq1 = jnp.concatenate([nope, rope, zpad], axis=-1).reshape(r, MLA_HEADS * 2 * LANE).astype(BF16)
    wq2 = jnp.concatenate([ropep, zpad], axis=-1).reshape(r, MLA_HEADS * LANE).astype(BF16)
    rk = w_kv_up.shape[0]
    wkv = w_kv_up.reshape(rk, MLA_HEADS, 2, MLA_NOPE).transpose(0, 2, 1, 3).reshape(rk, 2 * MLA_HEADS * MLA_NOPE)
    return wq1, wq2, wkv.astype(BF16)


def _moe_plan(idx, rank, counts, n_tiles):
    cnt = counts[:, 0].astype(jnp.int32)
    padded = ((cnt + MOE_G - 1) // MOE_G) * MOE_G
    ends = jnp.cumsum(padded)
    starts = ends - padded
    e_ids = jnp.arange(N_EXPERTS, dtype=jnp.int32)
    dest = jnp.sum(jnp.where(idx[..., None] == e_ids, starts, 0), axis=-1) + rank
    tile_start = jnp.arange(n_tiles, dtype=jnp.int32) * MOE_G
    te = jnp.sum((tile_start[:, None] >= ends[None, :]).astype(jnp.int32), axis=1)
    active = te < N_EXPERTS
    te_c = jnp.minimum(te, N_EXPERTS - 1)
    tile_is = te_c[:, None] == e_ids[None, :]
    cnt_t = jnp.sum(jnp.where(tile_is, cnt, 0), axis=1)
    start_t = jnp.sum(jnp.where(tile_is, starts, 0), axis=1)
    valid = jnp.clip(cnt_t - (tile_start - start_t), 0, MOE_G)
    valid = jnp.where(active, valid, 0)
    last_e = jnp.max(jnp.where(cnt > 0, jnp.arange(N_EXPERTS, dtype=jnp.int32), 0))
    te_f = jnp.where(active, te_c, last_e)
    tail_blocks = (n_tiles * MOE_G - ends[-1]) // (MOE_G // 2)
    pad_start = jnp.concatenate([starts + cnt, ends[-1:]]).astype(jnp.int32)
    pad_len = jnp.concatenate([padded - cnt, tail_blocks[None]]).astype(jnp.int32)
    return dest, te_f, valid, pad_start, pad_len


def kernel(x, c, ctx, c_ctx, norm1_g, norm2_g, ada_w, ada_b, w_in, b_gate, gla_wa, gla_ba, gla_norm_g,
           na_rpb, mla_q_norm_g, mla_w_q_up, mla_kv_norm_g, mla_w_kv_up, w_branch_gla, w_branch_na,
           w_branch_mla, w_out, router_w, router_b, moe_w1, moe_b1, moe_w2, moe_b2, final_norm_g):
    b, n, d = x.shape
    rb = CTX_LEN + n
    assert b <= 8 and d == D_MODEL and ctx.shape[1] == CTX_LEN

    cc = jnp.zeros((16, d), F32).at[:b].set(c).at[8].set(c_ctx)
    mod_all = _ada_mod(cc, ada_w, ada_b)
    cos_t, sin_t = _rope_tables(rb)

    for l in range(DEPTH):
        last = l == DEPTH - 1
        mod = mod_all[l].reshape(16 * 6, 1, d)
        if l == 0:
            xs, h = _norm_mod(ctx, x, norm1_g[l], mod)
        p = _matmul(h.reshape(b * rb, d), _proj_weight(w_in[l]), 1024 if (b * rb) % 1024 == 0 else ROW_TILE,
                    PROJ_TN, BF16)
        p3 = p.reshape(b, rb, PROJ_W)

        zpad = jnp.zeros((LANE - 2 * GLA_GATE_RANK, GLA_QK_W), F32)
        zr = jnp.zeros((GLA_GATE_RANK, GLA_QK_W), F32)
        waf = jnp.concatenate([gla_wa[l, 0], zr, zpad], axis=0).astype(BF16)
        wab = jnp.concatenate([zr, gla_wa[l, 1], zpad], axis=0).astype(BF16)
        og = _gla(p3, waf, wab, gla_ba[l], gla_norm_g[l].reshape(1, GLA_V_W))

        on = _na(p3, _na_bias_table(na_rpb[l], n // GRID_W))

        wq1, wq2, wkv = _mla_weights(mla_w_q_up[l], mla_w_kv_up[l])
        q_m, k_m, v_m = _mla_proj(p3, cos_t, sin_t, mla_q_norm_g[l].reshape(1, -1), mla_kv_norm_g[l].reshape(1, -1),
                                  wq1, wq2, wkv)
        om = _mla_attn(q_m, k_m, v_m, ctx_queries=False)
        om_ctx = None if last else _mla_attn(q_m, k_m, v_m, ctx_queries=True)

        xs, hp, logits_t = _merge(
            xs, p3, og, on, om, om_ctx, b_gate[l].reshape(1, -1),
            w_branch_gla[l].astype(BF16), w_branch_na[l].astype(BF16), w_branch_mla[l].astype(BF16),
            w_out[l].astype(BF16), mod, norm2_g[l].reshape(1, d), router_w[l].T,
            jnp.broadcast_to(router_b[l][:, None], (N_EXPERTS, LANE)), skip_ctx=last)

        t_tok = hp.shape[0]
        idx, wts, rank, counts = _route(logits_t)
        n_tiles = (TOP_K * t_tok) // MOE_G + N_EXPERTS
        dest, te, tv, pad_start, pad_len = _moe_plan(idx, rank, counts, n_tiles)
        xg = _dispatch(hp, dest, pad_start, pad_len, n_tiles * MOE_G)
        yg = _ffn(l, te, tv, xg, moe_w1, moe_b1, moe_w2, moe_b2)
        if last:
            (xs,) = _combine(xs, yg, dest, wts, mod, final_norm_g, mod, ctx_first=False, final_norm=True)
        else:
            xs, h = _combine(xs, yg, dest, wts, mod, norm1_g[l + 1], mod_all[l + 1].reshape(16 * 6, 1, d),
                             ctx_first=True, final_norm=False)
    return xs
```

```python
import functools

import numpy as np
import jax
import jax.numpy as jnp
from jax import lax
from jax.experimental import pallas as pl
from jax.experimental.pallas import tpu as pltpu

F32 = jnp.float32
BF16 = jnp.bfloat16

D_MODEL = 2048
DEPTH = 2
GRID_W = 64
CTX_LEN = 256
EPS = 1e-6
ROPE_BASE = 10000.0
NEG_INF = -1e30

GLA_HEADS = 4
GLA_DK = 64
GLA_DV = 128
GLA_GATE_RANK = 16
GLA_TAU = 16.0
GLA_CHUNK = 64
NA_HEADS = 8
NA_HEAD_DIM = 64
NA_KH = 8
NA_KW = 16
NA_SCALE = NA_HEAD_DIM ** -0.5
MLA_HEADS = 8
MLA_Q_RANK = 512
MLA_KV_RANK = 512
MLA_NOPE = 128
MLA_ROPE = 64
MLA_V = 128
MLA_SCALE = (MLA_NOPE + MLA_ROPE) ** -0.5
N_BRANCH = 3
N_EXPERTS = 32
TOP_K = 4
EXPERT_FF = D_MODEL
SWIGLU_LIMIT = 7.0
SWIGLU_ALPHA = 1.702

GLA_QK_W = GLA_HEADS * GLA_DK
GLA_V_W = GLA_HEADS * GLA_DV
NA_W = NA_HEADS * NA_HEAD_DIM
MLA_V_W = MLA_HEADS * MLA_V

LANE = 128
SUBLANE = 8
ROW_TILE = 256
VMEM_LIMIT = 56 * 1024 * 1024
FFN_VMEM_LIMIT = 62 * 1024 * 1024

COL_GATE = 0
COL_GQ = COL_GATE + N_BRANCH * D_MODEL
COL_GK = COL_GQ + GLA_QK_W
COL_GV = COL_GK + GLA_QK_W
COL_GR = COL_GV + GLA_V_W
COL_NQ = COL_GR + GLA_V_W
COL_NK = COL_NQ + NA_W
COL_NV = COL_NK + NA_W
COL_MCQ = COL_NV + NA_W
COL_MCKV = COL_MCQ + MLA_Q_RANK
COL_KR = COL_MCKV + MLA_KV_RANK
COL_KRP = COL_KR + LANE
COL_GA = COL_KRP + LANE
PROJ_TN = 1536
PROJ_W = 7 * PROJ_TN
assert COL_GA + LANE <= PROJ_W

MOE_G = 1024
MOE_SB = 256
MOE_CHAIN = 1024
MOE_FC = 512
DISPATCH_TT = 1024
COMBINE_TT = 256
ROUTE_TT = 512
MLA_TQ = 2048
MLA_CHAIN = 256


def _cparams(sem, vmem_limit=VMEM_LIMIT):
    return pltpu.CompilerParams(dimension_semantics=sem, vmem_limit_bytes=vmem_limit)


def _nt(a, b):
    return lax.dot_general(a, b, (((1,), (1,)), ((), ())), preferred_element_type=F32)


def _tn(a, b):
    return lax.dot_general(a, b, (((0,), (0,)), ((), ())), preferred_element_type=F32)


def _mm(a, b):
    return jnp.dot(a, b, preferred_element_type=F32)


def _sigmoid(x):
    return 1.0 / (1.0 + jnp.exp(-x))


def _split_hi_lo(x):
    hi = x.astype(BF16)
    lo = (x - hi.astype(F32)).astype(BF16)
    return hi, lo


def _ada_kernel(c_ref, w_ref, b_ref, o_ref):
    c = c_ref[...]
    s = (c * _sigmoid(c)).astype(BF16)
    o_ref[0] = _mm(s, w_ref[0].astype(BF16)) + b_ref[0]


def _ada_mod(cc, ada_w, ada_b):
    nl, d, n6 = ada_w.shape
    tn = 1024
    return pl.pallas_call(
        _ada_kernel,
        grid=(nl, n6 // tn),
        in_specs=[pl.BlockSpec((16, d), lambda l, j: (0, 0)),
                  pl.BlockSpec((1, d, tn), lambda l, j: (l, 0, j)),
                  pl.BlockSpec((1, 1, tn), lambda l, j: (l, 0, j))],
        out_specs=pl.BlockSpec((1, 16, tn), lambda l, j: (l, 0, j)),
        out_shape=jax.ShapeDtypeStruct((nl, 16, n6), F32),
        compiler_params=_cparams(("arbitrary", "arbitrary")),
        name="ada_mod",
    )(cc, ada_w, ada_b.reshape(nl, 1, n6))


def _mod_spec(part, ctx_first):
    def imap(b, j):
        row = jnp.where(j == 0, 8, b) if ctx_first else b
        return (row * 6 + part, 0, 0)
    return pl.BlockSpec((1, 1, D_MODEL), imap)


def _norm_mod_kernel(ctx_ref, x_ref, g_ref, sc_ref, sh_ref, xs_ref, o_ref):
    x = jnp.where(pl.program_id(1) == 0, ctx_ref[0], x_ref[0])
    xs_ref[0] = x
    ms = jnp.mean(x * x, axis=-1, keepdims=True)
    y = x * lax.rsqrt(ms + EPS) * g_ref[...]
    o_ref[0] = (y * (1.0 + sc_ref[0]) + sh_ref[0]).astype(o_ref.dtype)


def _norm_mod(ctx, x, g, mod):
    b, n, d = x.shape
    rb = ctx.shape[1] + n
    assert ctx.shape[1] == ROW_TILE
    row_spec = pl.BlockSpec((1, ROW_TILE, d), lambda i, j: (i, j, 0))
    return pl.pallas_call(
        _norm_mod_kernel,
        grid=(b, rb // ROW_TILE),
        in_specs=[pl.BlockSpec((1, ROW_TILE, d), lambda i, j: (i, 0, 0)),
                  pl.BlockSpec((1, ROW_TILE, d), lambda i, j: (i, jnp.maximum(j - 1, 0), 0)),
                  pl.BlockSpec((1, d), lambda i, j: (0, 0)),
                  _mod_spec(1, True), _mod_spec(0, True)],
        out_specs=[row_spec, row_spec],
        out_shape=[jax.ShapeDtypeStruct((b, rb, d), F32), jax.ShapeDtypeStruct((b, rb, d), BF16)],
        compiler_params=_cparams(("arbitrary", "arbitrary")),
        name="norm_mod",
    )(ctx, x, g.reshape(1, d), mod, mod)


def _matmul_kernel(x_ref, w_ref, o_ref):
    o_ref[...] = _mm(x_ref[...], w_ref[...]).astype(o_ref.dtype)


def _matmul(x, w, tm, tn, out_dtype):
    m, k = x.shape
    n = w.shape[1]
    return pl.pallas_call(
        _matmul_kernel,
        grid=(n // tn, m // tm),
        in_specs=[pl.BlockSpec((tm, k), lambda j, i: (i, 0)),
                  pl.BlockSpec((k, tn), lambda j, i: (0, j))],
        out_specs=pl.BlockSpec((tm, tn), lambda j, i: (i, j)),
        out_shape=jax.ShapeDtypeStruct((m, n), out_dtype),
        compiler_params=_cparams(("arbitrary", "arbitrary")),
        name="in_proj",
    )(x, w)


def _gla_kernel(q_ref, k_ref, v_ref, r_ref, ab_ref, waf_ref, wab_ref, ba_ref, g_ref, o_ref,
                laf_ref, lab_ref, of_ref, ob_ref, st_ref, *, n_ctx_chunks, n_chunks):
    c = GLA_CHUNK
    ab = ab_ref[0]
    zf = _mm(ab, waf_ref[...]) + ba_ref[0:1, :]
    zb = _mm(ab, wab_ref[...]) + ba_ref[1:2, :]
    laf_ref[...] = (jnp.minimum(zf, 0.0) - jnp.log(1.0 + jnp.exp(-jnp.abs(zf)))) * (1.0 / GLA_TAU)
    lab_ref[...] = (jnp.minimum(zb, 0.0) - jnp.log(1.0 + jnp.exp(-jnp.abs(zb)))) * (1.0 / GLA_TAU)
    st_ref[...] = jnp.zeros_like(st_ref)

    ri = lax.broadcasted_iota(jnp.int32, (c, c), 0)
    ci = lax.broadcasted_iota(jnp.int32, (c, c), 1)
    tri = [(ri >= ci), (ri <= ci)]
    tri_bf = [t.astype(F32).astype(BF16) for t in tri]
    tri4 = [jnp.concatenate([t] * GLA_HEADS, axis=0) for t in tri]
    lane_q = lax.broadcasted_iota(jnp.int32, (c, GLA_QK_W), 1) // GLA_DK
    lane_v = lax.broadcasted_iota(jnp.int32, (c, GLA_V_W), 1) // GLA_DV
    st_row_h = lax.broadcasted_iota(jnp.int32, (GLA_V_W, GLA_QK_W), 0) // GLA_DV
    st_col_h = lax.broadcasted_iota(jnp.int32, (GLA_V_W, GLA_QK_W), 1) // GLA_DK
    st_mask = st_row_h == st_col_h

    def one_dir(d, chunk, la_ref, out_ref):
        r0 = pl.multiple_of(chunk * c, c)
        la = la_ref[pl.ds(r0, c), :]
        hi, lo = _split_hi_lo(la)
        bc = _mm(tri_bf[d], hi) + _mm(tri_bf[d], lo)
        btot = bc[c - 1:c, :] if d == 0 else bc[0:1, :]
        q = q_ref[0, pl.ds(r0, c), :].astype(F32) * (GLA_DK ** -0.5)
        k = k_ref[0, pl.ds(r0, c), :].astype(F32)
        v = v_ref[0, pl.ds(r0, c), :]
        qd = (q * jnp.exp(bc)).astype(BF16)
        ki = (k * jnp.exp(-bc)).astype(BF16)
        kd = (k * jnp.exp(btot - bc)).astype(BF16)
        zero = jnp.zeros_like(qd)
        qs = jnp.concatenate([jnp.where(lane_q == h, qd, zero) for h in range(GLA_HEADS)], axis=0)
        att = _nt(qs, ki)
        att = jnp.where(tri4[d], att, 0.0).astype(BF16)
        rr = _mm(att, v)
        o = _nt(qd, st_ref[d].astype(BF16))
        for h in range(GLA_HEADS):
            o = o + jnp.where(lane_v == h, rr[h * c:(h + 1) * c, :], 0.0)
        out_ref[pl.ds(r0, c), :] = o
        upd = _tn(v, kd)
        st_ref[d] = jnp.where(st_mask, st_ref[d] * jnp.exp(btot) + upd, 0.0)

    def step(i, carry):
        one_dir(0, i, laf_ref, of_ref)
        cb = jnp.where(i < n_ctx_chunks, n_ctx_chunks - 1 - i, n_chunks + n_ctx_chunks - 1 - i)
        one_dir(1, cb, lab_ref, ob_ref)
        return carry

    lax.fori_loop(0, n_chunks, step, 0, unroll=4)

    def epilogue(j, carry):
        r0 = pl.multiple_of(j * ROW_TILE, ROW_TILE)
        o = of_ref[pl.ds(r0, ROW_TILE), :] + ob_ref[pl.ds(r0, ROW_TILE), :]
        r = r_ref[0, pl.ds(r0, ROW_TILE), :].astype(F32)
        gate = r * _sigmoid(r)
        for h in range(GLA_HEADS):
            sl = slice(h * GLA_DV, (h + 1) * GLA_DV)
            oh = o[:, sl]
            ms = jnp.mean(oh * oh, axis=-1, keepdims=True)
            y = oh * lax.rsqrt(ms + EPS) * g_ref[:, sl]
            o_ref[0, pl.ds(r0, ROW_TILE), sl] = (y * gate[:, sl]).astype(o_ref.dtype)
        return carry

    lax.fori_loop(0, (n_chunks * c) // ROW_TILE, epilogue, 0)


def _gla(p3, waf, wab, ba, g):
    b, rb, _ = p3.shape
    n_chunks = rb // GLA_CHUNK
    kern = functools.partial(_gla_kernel, n_ctx_chunks=CTX_LEN // GLA_CHUNK, n_chunks=n_chunks)

    def col(width, off):
        return pl.BlockSpec((1, rb, width), lambda i: (i, 0, off // width))

    return pl.pallas_call(
        kern,
        grid=(b,),
        in_specs=[col(GLA_QK_W, COL_GQ), col(GLA_QK_W, COL_GK), col(GLA_V_W, COL_GV), col(GLA_V_W, COL_GR),
                  col(LANE, COL_GA),
                  pl.BlockSpec((LANE, GLA_QK_W), lambda i: (0, 0)),
                  pl.BlockSpec((LANE, GLA_QK_W), lambda i: (0, 0)),
                  pl.BlockSpec((2, GLA_QK_W), lambda i: (0, 0)),
                  pl.BlockSpec((1, GLA_V_W), lambda i: (0, 0))],
        out_specs=pl.BlockSpec((1, rb, GLA_V_W), lambda i: (i, 0, 0)),
        out_shape=jax.ShapeDtypeStruct((b, rb, GLA_V_W), BF16),
        scratch_shapes=[pltpu.VMEM((rb, GLA_QK_W), F32), pltpu.VMEM((rb, GLA_QK_W), F32),
                        pltpu.VMEM((rb, GLA_V_W), F32), pltpu.VMEM((rb, GLA_V_W), F32),
                        pltpu.VMEM((2, GLA_V_W, GLA_QK_W), F32)],
        compiler_params=_cparams(("arbitrary",)),
        name="gla",
    )(p3, p3, p3, p3, p3, waf, wab, ba, g)


def _softmax_pv(parts, ones_block=False):
    m = parts[0][0].max(axis=-1, keepdims=True)
    for s, _ in parts[1:]:
        m = jnp.maximum(m, s.max(axis=-1, keepdims=True))
    acc = None
    den = None
    for s, v in parts:
        if ones_block:
            pv = _mm(jnp.exp((s - m).astype(BF16)), v)
        else:
            e = jnp.exp(s - m)
            l = e.sum(axis=-1, keepdims=True)
            den = l if den is None else den + l
            pv = _mm(e.astype(BF16), v)
        acc = pv if acc is None else acc + pv
    if ones_block:
        dv = acc.shape[1] // 2
        return acc[:, :dv] / acc[:, dv:]
    return acc / den


assert float(np.log2(NA_SCALE)).is_integer()
NA_QR = 4
NA_KR = NA_QR + NA_KH


def _na_block_rows(n_rows):
    r0s = (0, 2 * NA_QR, n_rows - NA_QR)
    return [(r0, int(np.clip(r0 - NA_KH // 2, 0, n_rows - NA_KR))) for r0 in r0s]


def _na_kernel(q_ref, k_ref, v_ref, bias_ref, o_ref, *, n_rows):
    nq = NA_QR * GRID_W
    nk = NA_KR * GRID_W
    n_blocks = n_rows // NA_QR
    lane = lax.broadcasted_iota(jnp.int32, (nq, LANE), 1)
    head_mask = [lane < NA_HEAD_DIM, lane >= NA_HEAD_DIM]
    kc = k_ref[0, 0:CTX_LEN, :]
    vc = v_ref[0, 0:CTX_LEN, :]
    vc1 = jnp.concatenate([vc, jnp.ones_like(vc)], axis=1)

    def block(bi, carry):
        r0 = bi * NA_QR
        ks = jnp.clip(r0 - NA_KH // 2, 0, n_rows - NA_KR)
        var = jnp.where(bi == 0, 0, jnp.where(bi == n_blocks - 1, 2, 1))
        q0 = pl.multiple_of(CTX_LEN + r0 * GRID_W, GRID_W)
        k0 = pl.multiple_of(CTX_LEN + ks * GRID_W, GRID_W)
        q = q_ref[0, pl.ds(q0, nq), :] * NA_SCALE
        kl = k_ref[0, pl.ds(k0, nk), :]
        vl = v_ref[0, pl.ds(k0, nk), :]
        vl1 = jnp.concatenate([vl, jnp.ones_like(vl)], axis=1)
        outs = []
        for hh in range(2):
            qm = jnp.where(head_mask[hh], q, jnp.zeros_like(q))
            s_loc = _nt(qm, kl) + bias_ref[hh, var]
            s_ctx = _nt(qm, kc)
            outs.append(_softmax_pv([(s_loc, vl1), (s_ctx, vc1)], ones_block=True))
        o_ref[0, pl.ds(q0, nq), :] = jnp.where(head_mask[0], outs[0], outs[1]).astype(o_ref.dtype)
        return carry

    lax.fori_loop(0, n_blocks, block, 0, unroll=4)

    qc = q_ref[0, 0:CTX_LEN, :]
    lane_c = lax.broadcasted_iota(jnp.int32, (CTX_LEN, LANE), 1)
    outs = []
    for hh in range(2):
        msk = (lane_c < NA_HEAD_DIM) if hh == 0 else (lane_c >= NA_HEAD_DIM)
        qm = jnp.where(msk, qc, jnp.zeros_like(qc))
        outs.append(_softmax_pv([(_nt(qm, kc) * NA_SCALE, vc)]))
    o_ref[0, 0:CTX_LEN, :] = jnp.where(lane_c < NA_HEAD_DIM, outs[0], outs[1]).astype(o_ref.dtype)


def _na_bias_table(rpb, n_rows):
    qcol = np.arange(GRID_W)[:, None]
    kcol = np.arange(GRID_W)[None, :]
    wstart = np.clip(qcol - NA_KW // 2, 0, GRID_W - NA_KW)
    in_win = (kcol >= wstart) & (kcol < wstart + NA_KW)
    dc = kcol - qcol + NA_KW - 1
    col_hot = ((dc[:, :, None] == np.arange(2 * NA_KW - 1)) & in_win[:, :, None]).astype(np.float32)
    geo = _na_block_rows(n_rows)
    row_hot = np.zeros((len(geo), NA_QR, NA_KR, 2 * NA_KH - 1), np.float32)
    for v, (r0, ks) in enumerate(geo):
        for a in range(NA_QR):
            rs = int(np.clip(r0 + a - NA_KH // 2, 0, n_rows - NA_KH))
            for i in range(NA_KR):
                if rs <= ks + i < rs + NA_KH:
                    row_hot[v, a, i, ks + i - (r0 + a) + NA_KH - 1] = 1.0
    t = jnp.einsum('hrc,vair,qkc->hvaqik', rpb.astype(F32), jnp.asarray(row_hot), jnp.asarray(col_hot),
                   precision=lax.Precision.HIGHEST)
    inside = (row_hot.sum(-1) > 0)[:, :, None, :, None] & in_win[None, None, :, None, :]
    t = jnp.where(inside[None], t, NEG_INF)
    return t.reshape(NA_HEADS, len(geo), NA_QR * GRID_W, NA_KR * GRID_W)


def _na(p3, bias):
    b, rb, _ = p3.shape
    n_rows = (rb - CTX_LEN) // GRID_W
    kern = functools.partial(_na_kernel, n_rows=n_rows)

    def col(off):
        return pl.BlockSpec((1, rb, LANE), lambda p, i: (i, 0, off // LANE + p))

    return pl.pallas_call(
        kern,
        grid=(NA_HEADS // 2, b),
        in_specs=[col(COL_NQ), col(COL_NK), col(COL_NV),
                  pl.BlockSpec((2,) + bias.shape[1:], lambda p, i: (p, 0, 0, 0))],
        out_specs=pl.BlockSpec((1, rb, LANE), lambda p, i: (i, 0, p)),
        out_shape=jax.ShapeDtypeStruct((b, rb, NA_W), BF16),
        compiler_params=_cparams(("arbitrary", "arbitrary")),
        name="na",
    )(p3, p3, p3, bias)


def _mla_proj_kernel(cq_ref, ckv_ref, kr_ref, krp_ref, cos_ref, sin_ref, gq_ref, gkv_ref,
                     wq_ref, wq2_ref, wkv_ref, q_ref, k_ref, v_ref):
    def norm(x, g):
        x = x.astype(F32)
        ms = jnp.mean(x * x, axis=-1, keepdims=True)
        return (x * lax.rsqrt(ms + EPS) * g).astype(BF16)

    cos = cos_ref[...]
    sin = sin_ref[...]
    nq = norm(cq_ref[0], gq_ref[...])
    yq = _mm(nq, wq_ref[...])
    yq2 = _mm(nq, wq2_ref[...])
    nkv = norm(ckv_ref[0], gkv_ref[...])
    ykv = _mm(nkv, wkv_ref[...])
    k_rot = (kr_ref[0].astype(F32) * cos + krp_ref[0].astype(F32) * sin).astype(BF16)
    for h in range(MLA_HEADS):
        a = 2 * h * LANE
        q_ref[0, :, a:a + LANE] = (yq[:, a:a + LANE] * MLA_SCALE).astype(BF16)
        rot = yq[:, a + LANE:a + 2 * LANE] * cos + yq2[:, h * LANE:(h + 1) * LANE] * sin
        q_ref[0, :, a + LANE:a + 2 * LANE] = (rot * MLA_SCALE).astype(BF16)
        k_ref[0, :, a:a + LANE] = ykv[:, h * LANE:(h + 1) * LANE].astype(BF16)
        k_ref[0, :, a + LANE:a + 2 * LANE] = k_rot
        v0 = MLA_HEADS * MLA_NOPE + h * MLA_V
        v_ref[0, :, a:a + LANE] = ykv[:, v0:v0 + MLA_V].astype(BF16)
        v_ref[0, :, a + LANE:a + 2 * LANE] = jnp.ones((ykv.shape[0], LANE), BF16)


def _mla_proj(p3, cos_t, sin_t, gq, gkv, wq, wq2, wkv):
    b, rb, _ = p3.shape
    tm = ROW_TILE

    def col(width, off):
        return pl.BlockSpec((1, tm, width), lambda i, j: (i, j, off // width))

    def full(a):
        return pl.BlockSpec(a.shape, lambda i, j: (0, 0))

    hw = MLA_HEADS * 2 * LANE
    nj = rb // tm
    return pl.pallas_call(
        _mla_proj_kernel,
        grid=(b, nj),
        in_specs=[col(MLA_Q_RANK, COL_MCQ), col(MLA_KV_RANK, COL_MCKV), col(LANE, COL_KR), col(LANE, COL_KRP),
                  pl.BlockSpec((tm, LANE), lambda i, j: (j, 0)), pl.BlockSpec((tm, LANE), lambda i, j: (j, 0)),
                  full(gq), full(gkv), full(wq), full(wq2), full(wkv)],
        out_specs=[pl.BlockSpec((1, tm, hw), lambda i, j: (i, (j + nj - 1) % nj, 0)),
                   pl.BlockSpec((1, tm, hw), lambda i, j: (i, (j + nj - 1) % nj, 0)),
                   pl.BlockSpec((1, tm, hw), lambda i, j: (i, (j + nj - 1) % nj, 0))],
        out_shape=[jax.ShapeDtypeStruct((b, rb, hw), BF16), jax.ShapeDtypeStruct((b, rb, hw), BF16),
                   jax.ShapeDtypeStruct((b, rb, hw), BF16)],
        compiler_params=_cparams(("arbitrary", "arbitrary")),
        name="mla_proj",
    )(p3, p3, p3, p3, cos_t, sin_t, gq, gkv, wq, wq2, wkv)


def _mla_attn_kernel(q_ref, k_ref, v_ref, o_ref):
    k = k_ref[0]
    v = v_ref[0]
    tq = q_ref.shape[1]
    sub = min(tq, MLA_CHAIN)
    for r0 in range(0, tq, sub):
        s = _nt(q_ref[0, r0:r0 + sub, :], k)
        e = jnp.exp((s - s.max(axis=-1, keepdims=True)).astype(BF16))
        pv = _mm(e, v)
        o_ref[0, r0:r0 + sub, :] = (pv[:, :MLA_V] / pv[:, MLA_V:]).astype(o_ref.dtype)


def _mla_attn(q, k, v, ctx_queries):
    b, rb, _ = q.shape
    n_lat = rb - CTX_LEN
    if ctx_queries:
        tq, nq, n_keys = CTX_LEN, 1, CTX_LEN
        q_off = kv_blk = n_lat // CTX_LEN
    else:
        tq = MLA_TQ if n_lat % MLA_TQ == 0 else ROW_TILE
        nq, n_keys, q_off, kv_blk = n_lat // tq, rb, 0, 0
    return pl.pallas_call(
        _mla_attn_kernel,
        grid=(b, MLA_HEADS, nq),
        in_specs=[pl.BlockSpec((1, tq, 2 * LANE), lambda i, h, j: (i, j + q_off, h)),
                  pl.BlockSpec((1, n_keys, 2 * LANE), lambda i, h, j: (i, kv_blk, h)),
                  pl.BlockSpec((1, n_keys, 2 * LANE), lambda i, h, j: (i, kv_blk, h))],
        out_specs=pl.BlockSpec((1, tq, MLA_V), lambda i, h, j: (i, j, h)),
        out_shape=jax.ShapeDtypeStruct((b, nq * tq, MLA_V_W), BF16),
        compiler_params=_cparams(("arbitrary", "arbitrary", "arbitrary")),
        name="mla_attn_ctx" if ctx_queries else "mla_attn",
    )(q, k, v)


def _merge_kernel(x_ref, oa_ref, ob_ref, oc_ref, occ_ref, ga_ref, gb_ref, gc_ref, bg_ref,
                  wa_ref, wb_ref, wc_ref, wo_ref, gt1_ref, sc2_ref, sh2_ref, g2_ref, rw_ref, rb_ref,
                  xo_ref, h_ref, lg_ref, *, ctx_tile):
    d = D_MODEL

    wh, wl = _split_hi_lo(rw_ref[...])
    tm = x_ref.shape[1]
    sub = tm
    for r0 in range(0, tm, sub):
        rows = slice(r0, r0 + sub)

        def gated(g_ref, k, br):
            z = g_ref[0, rows, :] + bg_ref[:, k * d:(k + 1) * d].astype(BF16)
            return _sigmoid(z) * br.astype(BF16)

        oc = oc_ref[0, rows, :]
        if ctx_tile:
            oc = jnp.where(pl.program_id(1) == 0, occ_ref[0, rows, :], oc)
        m = gated(ga_ref, 0, _mm(oa_ref[0, rows, :], wa_ref[...]))
        m = m + gated(gb_ref, 1, _mm(ob_ref[0, rows, :], wb_ref[...]))
        m = m + gated(gc_ref, 2, _mm(oc, wc_ref[...]))
        y = _mm(m, wo_ref[...])
        x = x_ref[0, rows, :] + gt1_ref[0] * y
        xo_ref[0, rows, :] = x
        ms = jnp.mean(x * x, axis=-1, keepdims=True)
        h = x * lax.rsqrt(ms + EPS) * g2_ref[...]
        h = h * (1.0 + sc2_ref[0]) + sh2_ref[0]
        h_ref[rows, :] = h.astype(h_ref.dtype)
        hh, hl = _split_hi_lo(h)
        lg_ref[:, rows] = _nt(wh, hh) + _nt(wh, hl) + _nt(wl, hh) + rb_ref[:, 0:1]


def _merge(x, p3, og, on, om, om_ctx, bg, wa, wb, wc, wo, mod, g2, rwt, rbias, skip_ctx):
    b, rb, d = x.shape
    tm = ROW_TILE
    jo = 1 if skip_ctx else 0
    nj = rb // tm - jo
    rows_out = nj * tm

    def rows(width, cblk=0):
        return pl.BlockSpec((1, tm, width), lambda i, j: (i, j + jo, cblk))

    om_spec = pl.BlockSpec((1, tm, MLA_V_W), lambda i, j: (i, jnp.maximum(j + jo - 1, 0), 0))
    omc_spec = pl.BlockSpec((1, tm, MLA_V_W), lambda i, j: (i, 0, 0))
    kern = functools.partial(_merge_kernel, ctx_tile=not skip_ctx)

    def full(a):
        return pl.BlockSpec(a.shape, lambda i, j: (0,) * a.ndim, pipeline_mode=pl.Buffered(1))

    def mod_spec(part):
        def imap(i, j):
            row = i if skip_ctx else jnp.where(j == 0, 8, i)
            return (row * 6 + part, 0, 0)
        return pl.BlockSpec((1, 1, d), imap)

    return pl.pallas_call(
        kern,
        grid=(b, nj),
        in_specs=[rows(d), rows(GLA_V_W), rows(NA_W), om_spec, omc_spec,
                  rows(d, 0), rows(d, 1), rows(d, 2), full(bg),
                  full(wa), full(wb), full(wc), full(wo),
                  mod_spec(2), mod_spec(4), mod_spec(3), full(g2), full(rwt), full(rbias)],
        out_specs=[pl.BlockSpec((1, tm, d), lambda i, j: (i, j, 0)),
                   pl.BlockSpec((tm, d), lambda i, j: (i * nj + j, 0)),
                   pl.BlockSpec((N_EXPERTS, tm), lambda i, j: (0, i * nj + j))],
        out_shape=[jax.ShapeDtypeStruct((b, rows_out, d), F32),
                   jax.ShapeDtypeStruct((b * rows_out, d), BF16),
                   jax.ShapeDtypeStruct((N_EXPERTS, b * rows_out), F32)],
        compiler_params=_cparams(("arbitrary", "arbitrary")),
        name="merge",
    )(x, og, on, om, om if skip_ctx else om_ctx, p3, p3, p3, bg, wa, wb, wc, wo, mod, mod, mod, g2, rwt, rbias)


def _route_kernel(l_ref, idx_ref, w_ref, rank_ref, cnt_ref, carry_ref):
    i = pl.program_id(0)
    tt = l_ref.shape[1]

    @pl.when(i == 0)
    def _():
        carry_ref[...] = jnp.zeros_like(carry_ref)

    l = l_ref[...]
    eio = lax.broadcasted_iota(jnp.int32, (N_EXPERTS, tt), 0)
    vals, idxs = [], []
    for _ in range(TOP_K):
        m = l.max(axis=0, keepdims=True)
        ik = jnp.min(jnp.where(l == m, eio, N_EXPERTS), axis=0, keepdims=True)
        vals.append(m)
        idxs.append(ik)
        l = jnp.where(eio == ik, -jnp.inf, l)
    es = [jnp.exp(v - vals[0]) for v in vals]
    den = es[0] + es[1] + es[2] + es[3]
    sel = jnp.zeros((N_EXPERTS, tt), F32)
    for ik in idxs:
        sel = sel + (eio == ik).astype(F32)
    si = lax.broadcasted_iota(jnp.int32, (tt, tt), 0)
    ti = lax.broadcasted_iota(jnp.int32, (tt, tt), 1)
    before = (si < ti).astype(F32).astype(BF16)
    rank_full = _mm(sel.astype(BF16), before) + carry_ref[:, 0:1]
    for k in range(TOP_K):
        idx_ref[k:k + 1, :] = idxs[k]
        w_ref[k:k + 1, :] = es[k] / den
        rk = jnp.sum(jnp.where(eio == idxs[k], rank_full, 0.0), axis=0, keepdims=True)
        rank_ref[k:k + 1, :] = rk.astype(jnp.int32)
    carry_ref[...] = carry_ref[...] + jnp.sum(sel, axis=1, keepdims=True)
    cnt_ref[...] = carry_ref[...]


def _route(logits_t):
    ne, t = logits_t.shape
    tt = ROUTE_TT if t % ROUTE_TT == 0 else ROW_TILE
    spec4 = pl.BlockSpec((TOP_K, tt), lambda i: (0, i))
    return pl.pallas_call(
        _route_kernel,
        grid=(t // tt,),
        in_specs=[pl.BlockSpec((ne, tt), lambda i: (0, i))],
        out_specs=[spec4, spec4, spec4, pl.BlockSpec((ne, LANE), lambda i: (0, 0))],
        out_shape=[jax.ShapeDtypeStruct((TOP_K, t), jnp.int32), jax.ShapeDtypeStruct((TOP_K, t), F32),
                   jax.ShapeDtypeStruct((TOP_K, t), jnp.int32), jax.ShapeDtypeStruct((ne, LANE), F32)],
        scratch_shapes=[pltpu.VMEM((ne, LANE), F32)],
        compiler_params=_cparams(("arbitrary",)),
        name="route",
    )(logits_t)


def _pad_fill(ps_ref, pn_ref, zero_ref, xg_ref, sem, wait):
    def copy(pos, rows):
        cp = pltpu.make_async_copy(zero_ref.at[pl.ds(0, rows), :], xg_ref.at[pl.ds(pos, rows), :], sem)
        cp.wait() if wait else cp.start()

    def per_expert(e, carry):
        pos = ps_ref[e]
        head = (-pos) & (SUBLANE - 1)
        for r in range(SUBLANE - 1):
            @pl.when(r < head)
            def _(r=r):
                copy(pos + r, 1)

        pos = pos + head
        n = pn_ref[e] - head
        bit = MOE_G // 2
        while bit >= SUBLANE:
            on = (n & bit) != 0

            @pl.when(on)
            def _(pos=pos, bit=bit):
                copy(pl.multiple_of(pos, SUBLANE), bit)

            pos = pos + jnp.where(on, bit, 0)
            bit //= 2
        return carry

    lax.fori_loop(0, N_EXPERTS, per_expert, 0)

    zr = zero_ref.shape[0]

    def tail(i, carry):
        pos = pl.multiple_of(ps_ref[N_EXPERTS] + i * zr, zr)
        cp = pltpu.make_async_copy(zero_ref, xg_ref.at[pl.ds(pos, zr), :], sem)
        cp.wait() if wait else cp.start()
        return carry

    lax.fori_loop(0, pn_ref[N_EXPERTS], tail, 0)


def _dispatch_kernel(ps_ref, pn_ref, dest_ref, h_ref, xg_ref, zero_ref, hf_ref, sem, zsem):
    tt = h_ref.shape[0]

    @pl.when(pl.program_id(0) == 0)
    def _():
        zero_ref[...] = jnp.zeros_like(zero_ref)
        _pad_fill(ps_ref, pn_ref, zero_ref, xg_ref, zsem, wait=False)

    hf_ref[...] = h_ref[...].astype(hf_ref.dtype)

    def issue(t, carry):
        for k in range(TOP_K):
            pltpu.make_async_copy(hf_ref.at[pl.ds(t, 1), :], xg_ref.at[pl.ds(dest_ref[0, k, t], 1), :],
                                  sem).start(priority=k % 2)
        return carry

    lax.fori_loop(0, tt, issue, 0, unroll=4)
    for k in range(TOP_K):
        pltpu.make_async_copy(hf_ref, xg_ref.at[pl.ds(0, tt), :], sem).wait()

    @pl.when(pl.program_id(0) == 0)
    def _():
        _pad_fill(ps_ref, pn_ref, zero_ref, xg_ref, zsem, wait=True)


def _dispatch(hp, dest, pad_start, pad_len, n_slots):
    t, w = hp.shape
    tt = DISPATCH_TT if t % DISPATCH_TT == 0 else ROW_TILE
    dest3 = dest.reshape(TOP_K, t // tt, tt).transpose(1, 0, 2)
    grid_spec = pltpu.PrefetchScalarGridSpec(
        num_scalar_prefetch=2,
        grid=(t // tt,),
        in_specs=[pl.BlockSpec((1, TOP_K, tt), lambda i, ps, pn: (i, 0, 0), memory_space=pltpu.SMEM),
                  pl.BlockSpec((tt, w), lambda i, ps, pn: (i, 0))],
        out_specs=pl.BlockSpec(memory_space=pl.ANY),
        scratch_shapes=[pltpu.VMEM((MOE_G // 2, w), F32), pltpu.VMEM((tt, w), F32),
                        pltpu.SemaphoreType.DMA(()), pltpu.SemaphoreType.DMA(())],
    )
    return pl.pallas_call(
        _dispatch_kernel,
        grid_spec=grid_spec,
        out_shape=jax.ShapeDtypeStruct((n_slots, w), F32),
        compiler_params=_cparams(("arbitrary",)),
        name="dispatch",
    )(pad_start, pad_len, dest3, hp)


def _ffn_kernel(te_ref, tv_ref, x_ref, w1g_ref, w1l_ref, b1g_ref, b1l_ref, w2_ref, b2_ref, y_ref,
                xb_ref, *, n_fc, n_tiles):
    i = pl.program_id(0)
    j = pl.program_id(1)
    valid = tv_ref[i]
    g = y_ref.shape[0]
    chunk = x_ref.shape[0]

    @pl.when((i < n_tiles) & (tv_ref[jnp.minimum(i + 1, n_tiles)] > 0))
    def _():
        xb_ref[i % 2, pl.ds(pl.multiple_of(j * chunk, chunk), chunk), :] = x_ref[...].astype(BF16)

    cur = (i + 1) % 2

    @pl.when((valid > 0) & (j == 0))
    def _():
        y_ref[...] = jnp.broadcast_to(b2_ref[0, 0], y_ref.shape)

    for nr in range(MOE_SB, g + 1, MOE_SB):
        @pl.when((valid > nr - MOE_SB) & (valid <= nr))
        def _(nr=nr):
            wg = w1g_ref[0, 0].astype(BF16)
            wl = w1l_ref[0, 0].astype(BF16)
            w2 = w2_ref[0, 0].astype(BF16)
            for r0 in range(0, nr, MOE_CHAIN):
                r1 = min(r0 + MOE_CHAIN, nr)
                x = xb_ref[cur, r0:r1, :]
                ug = _mm(x, wg) + b1g_ref[0, 0]
                ul = _mm(x, wl) + b1l_ref[0, 0]
                xg = jnp.minimum(ug, SWIGLU_LIMIT)
                xl = jnp.clip(ul, -SWIGLU_LIMIT, SWIGLU_LIMIT)
                act = xg * _sigmoid(SWIGLU_ALPHA * xg) * (xl + 1.0)
                y_ref[r0:r1, :] = y_ref[r0:r1, :] + _mm(act.astype(BF16), w2)

    @pl.when((j == n_fc - 1) & (valid == 0))
    def _():
        y_ref[...] = jnp.zeros_like(y_ref)


def _ffn(layer, tile_expert, tile_valid, xg, w1, b1, w2, b2):
    n_slots, d = xg.shape
    nl, ne, _, ff2 = w1.shape
    ff = ff2 // 2
    n_fc = ff // MOE_FC
    n_tiles = n_slots // MOE_G
    chunk = MOE_G // n_fc
    kern = functools.partial(_ffn_kernel, n_fc=n_fc, n_tiles=n_tiles)
    tile_expert = jnp.concatenate([tile_expert[:1], tile_expert])
    tile_valid = jnp.concatenate([jnp.zeros((1,), tile_valid.dtype), tile_valid])

    def jj(j, tv, i):
        return jnp.where(tv[i] > 0, j, n_fc - 1)

    def x_map(i, j, te, tv):
        staged = (i < n_tiles) & (tv[jnp.minimum(i + 1, n_tiles)] > 0)
        return (jnp.where(staged, jnp.minimum(i, n_tiles - 1) * n_fc + j, 0), 0)

    grid_spec = pltpu.PrefetchScalarGridSpec(
        num_scalar_prefetch=2,
        grid=(n_tiles + 1, n_fc),
        in_specs=[pl.BlockSpec((chunk, d), x_map),
                  pl.BlockSpec((1, 1, d, MOE_FC), lambda i, j, te, tv: (layer, te[i], 0, jj(j, tv, i))),
                  pl.BlockSpec((1, 1, d, MOE_FC), lambda i, j, te, tv: (layer, te[i], 0, jj(j, tv, i) + n_fc)),
                  pl.BlockSpec((1, 1, 1, MOE_FC), lambda i, j, te, tv: (layer, te[i], 0, jj(j, tv, i))),
                  pl.BlockSpec((1, 1, 1, MOE_FC), lambda i, j, te, tv: (layer, te[i], 0, jj(j, tv, i) + n_fc)),
                  pl.BlockSpec((1, 1, MOE_FC, d), lambda i, j, te, tv: (layer, te[i], jj(j, tv, i), 0)),
                  pl.BlockSpec((1, 1, 1, d), lambda i, j, te, tv: (layer, te[i], 0, 0))],
        out_specs=pl.BlockSpec((MOE_G, d), lambda i, j, te, tv: (jnp.maximum(i - 1, 0), 0)),
        scratch_shapes=[pltpu.VMEM((2, MOE_G, d), BF16)],
    )
    return pl.pallas_call(
        kern,
        grid_spec=grid_spec,
        out_shape=jax.ShapeDtypeStruct((n_slots, d), F32),
        compiler_params=_cparams(("arbitrary", "arbitrary"), FFN_VMEM_LIMIT),
        name="moe_ffn",
    )(tile_expert, tile_valid, xg, w1, w1, b1.reshape(nl, ne, 1, ff2), b1.reshape(nl, ne, 1, ff2), w2,
      b2.reshape(nl, ne, 1, d))


def _combine_kernel(dest_ref, destn_ref, x_ref, wt_ref, gt2_ref, gn_ref, sc_ref, sh_ref, yg_ref, o_ref, *rest,
                    final_norm):
    h_ref = None if final_norm else rest[0]
    buf_ref, sem = rest[-2:]
    i = pl.program_id(0)
    n = pl.num_programs(0)
    tt = x_ref.shape[1]
    slot = i % 2

    def issue(d_ref, s):
        def body(t, carry):
            for k in range(TOP_K):
                pltpu.make_async_copy(yg_ref.at[pl.ds(d_ref[0, k, t], 1), :],
                                      buf_ref.at[s, k, pl.ds(t, 1), :], sem.at[s]).start(priority=k % 2)
            return carry
        lax.fori_loop(0, tt, body, 0, unroll=4)

    @pl.when(i == 0)
    def _():
        issue(dest_ref, 0)

    for s in range(2):
        @pl.when((i + 1 < n) & (slot == 1 - s))
        def _(s=s):
            issue(destn_ref, s)

    for k in range(TOP_K):
        pltpu.make_async_copy(yg_ref.at[pl.ds(0, tt), :], buf_ref.at[slot, k], sem.at[slot]).wait()

    wt = wt_ref[...]
    y = buf_ref[slot, 0] * wt[:, 0:1]
    for k in range(1, TOP_K):
        y = y + buf_ref[slot, k] * wt[:, k:k + 1]
    x = x_ref[0] + gt2_ref[0] * y
    ms = jnp.mean(x * x, axis=-1, keepdims=True)
    xn = x * lax.rsqrt(ms + EPS) * gn_ref[...]
    if final_norm:
        o_ref[0] = xn
    else:
        o_ref[0] = x
        h_ref[0] = (xn * (1.0 + sc_ref[0]) + sh_ref[0]).astype(h_ref.dtype)


def _combine(x, yg, dest, wts, mod, gn, mod_next, ctx_first, final_norm):
    b, rows, d = x.shape
    tt = COMBINE_TT
    nj = rows // tt
    t = b * rows
    nt = t // tt
    dest3 = dest.reshape(TOP_K, nt, tt).transpose(1, 0, 2)
    wt = wts.T
    kern = functools.partial(_combine_kernel, final_norm=final_norm)

    def mod_spec(part):
        def imap(i):
            bi = i // nj
            row = jnp.where((i % nj) * tt < CTX_LEN, 8, bi) if ctx_first else bi
            return (row * 6 + part, 0, 0)
        return pl.BlockSpec((1, 1, d), imap)

    row_spec = pl.BlockSpec((1, tt, d), lambda i: (i // nj, i % nj, 0))
    out_specs = [row_spec]
    out_shape = [jax.ShapeDtypeStruct((b, rows, d), F32)]
    if not final_norm:
        out_specs.append(row_spec)
        out_shape.append(jax.ShapeDtypeStruct((b, rows, d), BF16))

    return pl.pallas_call(
        kern,
        grid=(nt,),
        in_specs=[pl.BlockSpec((1, TOP_K, tt), lambda i: (i, 0, 0), memory_space=pltpu.SMEM),
                  pl.BlockSpec((1, TOP_K, tt), lambda i: (jnp.minimum(i + 1, nt - 1), 0, 0),
                               memory_space=pltpu.SMEM),
                  row_spec,
                  pl.BlockSpec((tt, TOP_K), lambda i: (i, 0)),
                  mod_spec(5),
                  pl.BlockSpec((1, d), lambda i: (0, 0)),
                  mod_spec(1), mod_spec(0),
                  pl.BlockSpec(memory_space=pl.ANY)],
        out_specs=out_specs,
        out_shape=out_shape,
        scratch_shapes=[pltpu.VMEM((2, TOP_K, tt, d), yg.dtype), pltpu.SemaphoreType.DMA((2,))],
        compiler_params=_cparams(("arbitrary",)),
        name="combine",
    )(dest3, dest3, x, wt, mod, gn.reshape(1, d), mod_next, mod_next, yg)


def _proj_weight(w_in):
    d = w_in.shape[0]
    splits = (GLA_QK_W, GLA_QK_W, GLA_V_W, GLA_V_W, GLA_GATE_RANK, GLA_GATE_RANK,
              NA_W, NA_W, NA_W, MLA_Q_RANK, MLA_KV_RANK, MLA_ROPE, N_BRANCH * D_MODEL)
    pts = np.cumsum((0,) + splits)
    (gq, gk, gv, gr, gaf, gab, nq, nk, nv, mcq, mckv, mkr, gate) = [w_in[:, pts[i]:pts[i + 1]] for i in range(13)]
    q16 = MLA_ROPE // 4
    mkrp = jnp.concatenate([mkr[:, q16:2 * q16], mkr[:, :q16], mkr[:, 3 * q16:], mkr[:, 2 * q16:3 * q16]], axis=1)
    z = lambda n: jnp.zeros((d, n), w_in.dtype)
    cols = [gate, gq, gk, gv, gr, nq, nk, nv, mcq, mckv,
            mkr, z(LANE - MLA_ROPE), mkrp, z(LANE - MLA_ROPE),
            gaf, gab, z(LANE - 2 * GLA_GATE_RANK)]
    w = jnp.concatenate(cols, axis=1)
    w = jnp.concatenate([w, z(PROJ_W - w.shape[1])], axis=1)
    return w.astype(BF16)


def _rope_tables(rb):
    n = rb - CTX_LEN
    t = np.arange(n)
    nf = MLA_ROPE // 4
    freqs = ROPE_BASE ** (-np.arange(nf, dtype=np.float64) / nf)
    cos = np.zeros((rb, LANE), np.float32)
    sin = np.zeros((rb, LANE), np.float32)
    cos[:CTX_LEN, :MLA_ROPE] = 1.0
    for a, pos in enumerate((t // GRID_W, t % GRID_W)):
        ang = (pos.astype(np.float32)[:, None] * freqs.astype(np.float32)[None, :]).astype(np.float32)
        c, s = np.cos(ang), np.sin(ang)
        base = a * 2 * nf
        cos[CTX_LEN:, base:base + nf] = c
        cos[CTX_LEN:, base + nf:base + 2 * nf] = c
        sin[CTX_LEN:, base:base + nf] = -s
        sin[CTX_LEN:, base + nf:base + 2 * nf] = s
    return jnp.asarray(cos), jnp.asarray(sin)


def _mla_weights(w_q_up, w_kv_up):
    r = w_q_up.shape[0]
    wq = w_q_up.reshape(r, MLA_HEADS, MLA_NOPE + MLA_ROPE)
    nope, rope = wq[..., :MLA_NOPE], wq[..., MLA_NOPE:]
    q16 = MLA_ROPE // 4
    ropep = jnp.concatenate([rope[..., q16:2 * q16], rope[..., :q16], rope[..., 3 * q16:], rope[..., 2 * q16:3 * q16]],
                            axis=-1)
    zpad = jnp.zeros((r, MLA_HEADS, LANE - MLA_ROPE), w_q_up.dtype)
    wq1 = jnp.concatenate([nope, rope, zpad], axis=-1).reshape(r, MLA_HEADS * 2 * LANE).astype(BF16)
    wq2 = jnp.concatenate([ropep, zpad], axis=-1).reshape(r, MLA_HEADS * LANE).astype(BF16)
    rk = w_kv_up.shape[0]
    wkv = w_kv_up.reshape(rk, MLA_HEADS, 2, MLA_NOPE).transpose(0, 2, 1, 3).reshape(rk, 2 * MLA_HEADS * MLA_NOPE)
    return wq1, wq2, wkv.astype(BF16)


def _moe_plan(idx, rank, counts, n_tiles):
    cnt = counts[:, 0].astype(jnp.int32)
    padded = ((cnt + MOE_G - 1) // MOE_G) * MOE_G
    ends = jnp.cumsum(padded)
    starts = ends - padded
    e_ids = jnp.arange(N_EXPERTS, dtype=jnp.int32)
    dest = jnp.sum(jnp.where(idx[..., None] == e_ids, starts, 0), axis=-1) + rank
    tile_start = jnp.arange(n_tiles, dtype=jnp.int32) * MOE_G
    te = jnp.sum((tile_start[:, None] >= ends[None, :]).astype(jnp.int32), axis=1)
    active = te < N_EXPERTS
    te_c = jnp.minimum(te, N_EXPERTS - 1)
    tile_is = te_c[:, None] == e_ids[None, :]
    cnt_t = jnp.sum(jnp.where(tile_is, cnt, 0), axis=1)
    start_t = jnp.sum(jnp.where(tile_is, starts, 0), axis=1)
    valid = jnp.clip(cnt_t - (tile_start - start_t), 0, MOE_G)
    valid = jnp.where(active, valid, 0)
    last_e = jnp.max(jnp.where(cnt > 0, jnp.arange(N_EXPERTS, dtype=jnp.int32), 0))
    te_f = jnp.where(active, te_c, last_e)
    tail_blocks = (n_tiles * MOE_G - ends[-1]) // (MOE_G // 2)
    pad_start = jnp.concatenate([starts + cnt, ends[-1:]]).astype(jnp.int32)
    pad_len = jnp.concatenate([padded - cnt, tail_blocks[None]]).astype(jnp.int32)
    return dest, te_f, valid, pad_start, pad_len


def kernel(x, c, ctx, c_ctx, norm1_g, norm2_g, ada_w, ada_b, w_in, b_gate, gla_wa, gla_ba, gla_norm_g,
           na_rpb, mla_q_norm_g, mla_w_q_up, mla_kv_norm_g, mla_w_kv_up, w_branch_gla, w_branch_na,
           w_branch_mla, w_out, router_w, router_b, moe_w1, moe_b1, moe_w2, moe_b2, final_norm_g):
    b, n, d = x.shape
    rb = CTX_LEN + n
    assert b <= 8 and d == D_MODEL and ctx.shape[1] == CTX_LEN

    cc = jnp.zeros((16, d), F32).at[:b].set(c).at[8].set(c_ctx)
    mod_all = _ada_mod(cc, ada_w, ada_b)
    cos_t, sin_t = _rope_tables(rb)

    for l in range(DEPTH):
        last = l == DEPTH - 1
        mod = mod_all[l].reshape(16 * 6, 1, d)
        if l == 0:
            xs, h = _norm_mod(ctx, x, norm1_g[l], mod)
        p = _matmul(h.reshape(b * rb, d), _proj_weight(w_in[l]), 1024 if (b * rb) % 1024 == 0 else ROW_TILE,
                    PROJ_TN, BF16)
        p3 = p.reshape(b, rb, PROJ_W)

        zpad = jnp.zeros((LANE - 2 * GLA_GATE_RANK, GLA_QK_W), F32)
        zr = jnp.zeros((GLA_GATE_RANK, GLA_QK_W), F32)
        waf = jnp.concatenate([gla_wa[l, 0], zr, zpad], axis=0).astype(BF16)
        wab = jnp.concatenate([zr, gla_wa[l, 1], zpad], axis=0).astype(BF16)
        og = _gla(p3, waf, wab, gla_ba[l], gla_norm_g[l].reshape(1, GLA_V_W))

        on = _na(p3, _na_bias_table(na_rpb[l], n // GRID_W))

        wq1, wq2, wkv = _mla_weights(mla_w_q_up[l], mla_w_kv_up[l])
        q_m, k_m, v_m = _mla_proj(p3, cos_t, sin_t, mla_q_norm_g[l].reshape(1, -1), mla_kv_norm_g[l].reshape(1, -1),
                                  wq1, wq2, wkv)
        om = _mla_attn(q_m, k_m, v_m, ctx_queries=False)
        om_ctx = None if last else _mla_attn(q_m, k_m, v_m, ctx_queries=True)

        xs, hp, logits_t = _merge(
            xs, p3, og, on, om, om_ctx, b_gate[l].reshape(1, -1),
            w_branch_gla[l].astype(BF16), w_branch_na[l].astype(BF16), w_branch_mla[l].astype(BF16),
            w_out[l].astype(BF16), mod, norm2_g[l].reshape(1, d), router_w[l].T,
            jnp.broadcast_to(router_b[l][:, None], (N_EXPERTS, LANE)), skip_ctx=last)

        t_tok = hp.shape[0]
        idx, wts, rank, counts = _route(logits_t)
        n_tiles = (TOP_K * t_tok) // MOE_G + N_EXPERTS
        dest, te, tv, pad_start, pad_len = _moe_plan(idx, rank, counts, n_tiles)
        xg = _dispatch(hp, dest, pad_start, pad_len, n_tiles * MOE_G)
        yg = _ffn(l, te, tv, xg, moe_w1, moe_b1, moe_w2, moe_b2)
        if last:
            (xs,) = _combine(xs, yg, dest, wts, mod, final_norm_g, mod, ctx_first=False, final_norm=True)
        else:
            xs, h = _combine(xs, yg, dest, wts, mod, norm1_g[l + 1], mod_all[l + 1].reshape(16 * 6, 1, d),
                             ctx_first=True, final_norm=False)
    return xs
```
